```python
import math
import jax, jax.numpy as jnp
from jax import lax
import numpy as np

D_MODEL = 1024
BATCH = 8
SEQ = 4096
DEPTH = 2

GRID_W = 64
CTX_LEN = 256
HEAD_DIM = 64
ROPE_HALF = HEAD_DIM // 4
ROPE_BASE = 10000.0
QBLOCK = 128
EPS = 1e-6
NEG_INF = -1e30
GLA_HEADS = 4
GLA_DK = 64
GLA_DV = 128
GLA_RANK = 16
GLA_TAU = 16.0
GLA_CHUNK = 64
WIN_HEADS = 8
WIN_KV = 2
WINDOW = 128
GLB_HEADS = 8
GLB_KV = 4
DIF_HEADS = 4
DIF_KV = 2
DIF_DV = 2 * HEAD_DIM
BRANCH_W = GLA_HEADS * GLA_DV
N_BRANCH = 4
IN_WIDTHS = (
    GLA_HEADS * GLA_DK, GLA_HEADS * GLA_DK, BRANCH_W, GLA_RANK, GLA_RANK, BRANCH_W,
    WIN_HEADS * HEAD_DIM, WIN_KV * HEAD_DIM, WIN_KV * HEAD_DIM, BRANCH_W,
    GLB_HEADS * HEAD_DIM, GLB_KV * HEAD_DIM, GLB_KV * HEAD_DIM, BRANCH_W,
    DIF_HEADS * 2 * HEAD_DIM, DIF_KV * 2 * HEAD_DIM, DIF_KV * DIF_DV, BRANCH_W,
)
IN_COLS = sum(IN_WIDTHS)

kernel_name = 'hybrid_gla_window_global_diff_trunk'


def ln_plain(x):
    xf = x.astype(jnp.float32)
    mu = jnp.mean(xf, axis=-1, keepdims=True)
    var = jnp.mean(jnp.square(xf - mu), axis=-1, keepdims=True)
    return ((xf - mu) * lax.rsqrt(var + EPS)).astype(x.dtype)


def ln_affine(x, g, b):
    return ln_plain(x) * g + b


def rms_norm(x, g):
    xf = x.astype(jnp.float32)
    y = xf * lax.rsqrt(jnp.mean(jnp.square(xf), axis=-1, keepdims=True) + EPS)
    return (y * g).astype(x.dtype)


def rope_2d(x, cos, sin):
    xr = x.reshape(x.shape[:-1] + (2, 2, ROPE_HALF))
    x1, x2 = xr[..., 0, :], xr[..., 1, :]
    c, s = cos.astype(x.dtype), sin.astype(x.dtype)
    return jnp.stack([x1 * c - x2 * s, x2 * c + x1 * s], axis=-2).reshape(x.shape)


def split_heads(t, n_kv, n_grp, d):
    b, l, _ = t.shape
    return t.reshape(b, l, n_kv, n_grp, d).transpose(0, 2, 3, 1, 4)


def merge_heads(o):
    b, hk, g, l, d = o.shape
    return o.transpose(0, 3, 1, 2, 4).reshape(b, l, hk * g * d)


def split_cols(p):
    parts, start = [], 0
    for w in IN_WIDTHS:
        parts.append(p[..., start:start + w])
        start += w
    return parts


def sweep_query_blocks(fn, q):
    n = q.shape[-2]
    nb = n // QBLOCK
    qb = jnp.moveaxis(q.reshape(q.shape[:-2] + (nb, QBLOCK, q.shape[-1])), -3, 0)
    out = jnp.moveaxis(lax.map(fn, qb), 0, -3)
    return out.reshape(out.shape[:-3] + (n, out.shape[-1]))


def gla_chunked(q, k, v, log_a, s0):
    b_, h_, l_, dk = q.shape
    dv = v.shape[-1]
    nc = l_ // GLA_CHUNK
    f32 = jnp.float32
    qc = q.astype(f32).reshape(b_, h_, nc, GLA_CHUNK, dk)
    kc = k.astype(f32).reshape(b_, h_, nc, GLA_CHUNK, dk)
    vc = v.astype(f32).reshape(b_, h_, nc, GLA_CHUNK, dv)
    cum = jnp.cumsum(log_a.astype(f32).reshape(b_, h_, nc, GLA_CHUNK, dk), axis=3)
    cum_last = cum[:, :, :, -1:, :]
    q_t = qc * jnp.exp(cum)
    k_t = kc * jnp.exp(-cum)
    tri = jnp.tril(jnp.ones((GLA_CHUNK, GLA_CHUNK), dtype=bool))
    a = jnp.where(tri, jnp.einsum('bhncd,bhnsd->bhncs', q_t, k_t), 0.0)
    o_intra = jnp.einsum('bhncs,bhnsv->bhncv', a, vc)
    kv = jnp.einsum('bhncd,bhncv->bhndv', kc * jnp.exp(cum_last - cum), vc)
    decay = jnp.exp(cum_last[:, :, :, 0, :])

    def step(s, inp):
        dec, kv_c = inp
        return dec[..., None] * s + kv_c, s

    _, s_starts = lax.scan(step, s0.astype(f32), (jnp.moveaxis(decay, 2, 0), jnp.moveaxis(kv, 2, 0)))
    s_starts = jnp.moveaxis(s_starts, 0, 2)
    o_inter = jnp.einsum('bhncd,bhndv->bhncv', q_t, s_starts)
    return (o_intra + o_inter).reshape(b_, h_, l_, dv)


def gla_final_state(k, v, log_a):
    cum = jnp.cumsum(log_a.astype(jnp.float32), axis=2)
    w = jnp.exp(cum[:, :, -1:, :] - cum)
    return jnp.einsum('bhld,bhlv->bhdv', k.astype(jnp.float32) * w, v.astype(jnp.float32))


def head_rms(o, g, dtype):
    of = o.astype(jnp.float32)
    y = of * lax.rsqrt(jnp.mean(jnp.square(of), axis=-1, keepdims=True) + EPS)
    b_, h_, l_, d_ = y.shape
    return (y.transpose(0, 2, 1, 3).reshape(b_, l_, h_ * d_) * g).astype(dtype)


def gla_mixer(px, pc, w_gate, b_gate, norm_g, need_ctx):
    def prep(q, k, v, g_f, g_b):
        qh = split_heads(q, GLA_HEADS, 1, GLA_DK)[:, :, 0] * (GLA_DK ** -0.5)
        kh = split_heads(k, GLA_HEADS, 1, GLA_DK)[:, :, 0]
        vh = split_heads(v, GLA_HEADS, 1, GLA_DV)[:, :, 0]
        la_f = jax.nn.log_sigmoid((g_f @ w_gate[0] + b_gate[0]).astype(jnp.float32)) / GLA_TAU
        la_b = jax.nn.log_sigmoid((g_b @ w_gate[1] + b_gate[1]).astype(jnp.float32)) / GLA_TAU
        la_f = split_heads(la_f, GLA_HEADS, 1, GLA_DK)[:, :, 0]
        la_b = split_heads(la_b, GLA_HEADS, 1, GLA_DK)[:, :, 0]
        return qh, kh, vh, la_f, la_b

    def flip(t):
        return jnp.flip(t, axis=2)

    qx, kx, vx, lfx, lbx = prep(*px)
    qc, kc, vc, lfc, lbc = prep(*pc)
    s_f = gla_final_state(kc, vc, lfc)
    s_b = gla_final_state(flip(kc), flip(vc), flip(lbc))
    o_x = gla_chunked(qx, kx, vx, lfx, s_f) + flip(gla_chunked(flip(qx), flip(kx), flip(vx), flip(lbx), s_b))
    out_x = head_rms(o_x, norm_g, px[2].dtype)
    out_c = None
    if need_ctx:
        zero = jnp.zeros_like(s_f)
        o_c = gla_chunked(qc, kc, vc, lfc, zero) + flip(gla_chunked(flip(qc), flip(kc), flip(vc), flip(lbc), zero))
        out_c = head_rms(o_c, norm_g, pc[2].dtype)
    return out_x, out_c


def window_mixer(px, pc, sink, cos, sin, need_ctx):
    grp = WIN_HEADS // WIN_KV
    scale = HEAD_DIM ** -0.5
    q = rope_2d(split_heads(px[0], WIN_KV, grp, HEAD_DIM), cos, sin) * scale
    k = rope_2d(split_heads(px[1], WIN_KV, 1, HEAD_DIM)[:, :, 0], cos, sin)
    v = split_heads(px[2], WIN_KV, 1, HEAD_DIM)[:, :, 0]
    kc = split_heads(pc[1], WIN_KV, 1, HEAD_DIM)[:, :, 0]
    vc = split_heads(pc[2], WIN_KV, 1, HEAD_DIM)[:, :, 0]
    b_, hk, _, n, d = q.shape
    nb = n // WINDOW
    nw = 3 * WINDOW
    lc = kc.shape[2]
    qb = q.reshape(b_, hk, grp, nb, WINDOW, d)
    pad = ((0, 0), (0, 0), (WINDOW, WINDOW), (0, 0))
    kp = jnp.pad(k, pad).reshape(b_, hk, nb + 2, WINDOW, d)
    vp = jnp.pad(v, pad).reshape(b_, hk, nb + 2, WINDOW, d)
    kw = jnp.concatenate([kp[:, :, :-2], kp[:, :, 1:-1], kp[:, :, 2:]], axis=3)
    vw = jnp.concatenate([vp[:, :, :-2], vp[:, :, 1:-1], vp[:, :, 2:]], axis=3)
    s_win = jnp.einsum('bhgnqd,bhnkd->bhgnqk', qb, kw).astype(jnp.float32)
    qi = jnp.arange(WINDOW)[:, None]
    kj = jnp.arange(nw)[None, :]
    key_pos = jnp.arange(nb)[:, None, None] * WINDOW - WINDOW + kj[None]
    band = (kj >= qi) & (kj <= qi + 2 * WINDOW)
    mask = band[None] & (key_pos >= 0) & (key_pos < n)
    s_win = jnp.where(mask, s_win, NEG_INF)
    s_ctx = jnp.einsum('bhgnqd,bhkd->bhgnqk', qb, kc).astype(jnp.float32)
    sink_f = sink.astype(jnp.float32).reshape(1, hk, grp, 1, 1, 1)
    s_sink = jnp.broadcast_to(sink_f, s_win.shape[:-1] + (1,))
    p = jax.nn.softmax(jnp.concatenate([s_win, s_ctx, s_sink], axis=-1), axis=-1)
    o = (jnp.einsum('bhgnqk,bhnkd->bhgnqd', p[..., :nw].astype(v.dtype), vw)
         + jnp.einsum('bhgnqk,bhkd->bhgnqd', p[..., nw:nw + lc].astype(v.dtype), vc))
    out_x = merge_heads(o.reshape(b_, hk, grp, n, d))
    out_c = None
    if need_ctx:
        qc = split_heads(pc[0], WIN_KV, grp, HEAD_DIM) * scale
        s = jnp.einsum('bhgqd,bhkd->bhgqk', qc, kc).astype(jnp.float32)
        s_sink_c = jnp.broadcast_to(sink_f[..., 0], s.shape[:-1] + (1,))
        pcx = jax.nn.softmax(jnp.concatenate([s, s_sink_c], axis=-1), axis=-1)[..., :-1]
        out_c = merge_heads(jnp.einsum('bhgqk,bhkd->bhgqd', pcx.astype(vc.dtype), vc))
    return out_x, out_c


def global_mixer(px, pc, q_gain, k_gain, cos, sin, need_ctx):
    grp = GLB_HEADS // GLB_KV
    scale = HEAD_DIM ** -0.5
    q = rope_2d(rms_norm(split_heads(px[0], GLB_KV, grp, HEAD_DIM), q_gain), cos, sin) * scale
    k = rope_2d(rms_norm(split_heads(px[1], GLB_KV, 1, HEAD_DIM)[:, :, 0], k_gain), cos, sin)
    v = split_heads(px[2], GLB_KV, 1, HEAD_DIM)[:, :, 0]
    kc = rms_norm(split_heads(pc[1], GLB_KV, 1, HEAD_DIM)[:, :, 0], k_gain)
    vc = split_heads(pc[2], GLB_KV, 1, HEAD_DIM)[:, :, 0]
    k_all = jnp.concatenate([kc, k], axis=2)
    v_all = jnp.concatenate([vc, v], axis=2)

    def block(qb):
        s = jnp.einsum('bhgqd,bhkd->bhgqk', qb, k_all).astype(jnp.float32)
        p = jax.nn.softmax(s, axis=-1)
        return jnp.einsum('bhgqk,bhkd->bhgqd', p.astype(v_all.dtype), v_all)

    out_x = merge_heads(sweep_query_blocks(block, q))
    out_c = None
    if need_ctx:
        qc = rms_norm(split_heads(pc[0], GLB_KV, grp, HEAD_DIM), q_gain) * scale
        p = jax.nn.softmax(jnp.einsum('bhgqd,bhkd->bhgqk', qc, kc).astype(jnp.float32), axis=-1)
        out_c = merge_heads(jnp.einsum('bhgqk,bhkd->bhgqd', p.astype(vc.dtype), vc))
    return out_x, out_c


def diff_mixer(px, pc, lam_p, sub_g, lam_init, cos, sin, need_ctx):
    grp = DIF_HEADS // DIF_KV
    scale = HEAD_DIM ** -0.5

    def q_heads(t):
        b_, l_, _ = t.shape
        return t.reshape(b_, l_, DIF_KV, grp, 2, HEAD_DIM).transpose(0, 2, 3, 4, 1, 5)

    def k_heads(t):
        b_, l_, _ = t.shape
        return t.reshape(b_, l_, DIF_KV, 2, HEAD_DIM).transpose(0, 2, 3, 1, 4)

    q = rope_2d(q_heads(px[0]), cos, sin) * scale
    k = rope_2d(k_heads(px[1]), cos, sin)
    v = split_heads(px[2], DIF_KV, 1, DIF_DV)[:, :, 0]
    kc = k_heads(pc[1])
    vc = split_heads(pc[2], DIF_KV, 1, DIF_DV)[:, :, 0]
    lp = lam_p.astype(jnp.float32)
    lam = jnp.exp(jnp.sum(lp[0] * lp[1])) - jnp.exp(jnp.sum(lp[2] * lp[3])) + lam_init
    k_all = jnp.concatenate([kc, k], axis=3)
    v_all = jnp.concatenate([vc, v], axis=2)

    def diff_weights(s):
        p = jax.nn.softmax(s.astype(jnp.float32), axis=-1)
        return p[:, :, :, 0] - lam * p[:, :, :, 1]

    def block(qb):
        w = diff_weights(jnp.einsum('bhgmqd,bhmkd->bhgmqk', qb, k_all))
        return jnp.einsum('bhgqk,bhkd->bhgqd', w.astype(v_all.dtype), v_all)

    o = sweep_query_blocks(block, q)
    out_x = merge_heads(rms_norm(o, sub_g) * (1.0 - lam_init))
    out_c = None
    if need_ctx:
        qc = q_heads(pc[0]) * scale
        w = diff_weights(jnp.einsum('bhgmqd,bhmkd->bhgmqk', qc, kc))
        o_c = jnp.einsum('bhgqk,bhkd->bhgqd', w.astype(vc.dtype), vc)
        out_c = merge_heads(rms_norm(o_c, sub_g) * (1.0 - lam_init))
    return out_x, out_c


def merge_branches(h, branches, w_merge, w_up, w_out):
    terms = [jax.nn.sigmoid(h @ w_merge[i]) * (o @ w_up[i]) for i, o in enumerate(branches)]
    return sum(terms[1:], terms[0]) @ w_out


def hybrid_layer(x, xc, c, c_ctx, w_ada, b_ada, w_in, gla_w_gate, gla_b_gate, gla_norm, win_sink,
                 glb_q_norm, glb_k_norm, diff_lambda, diff_norm, w_merge, w_up, w_out, ln_g, ln_b,
                 cos, sin, lam_init, need_ctx):
    alpha = (2 * DEPTH) ** 0.25
    shift_x, scale_x, gate_x = jnp.split(jax.nn.silu(c) @ w_ada + b_ada, 3, axis=-1)
    shift_c, scale_c, gate_c = jnp.split(jax.nn.silu(c_ctx) @ w_ada + b_ada, 3, axis=-1)
    h = ln_plain(x) * (1.0 + scale_x[:, None, :]) + shift_x[:, None, :]
    hc = ln_plain(xc) * (1.0 + scale_c) + shift_c
    px = split_cols(h @ w_in)
    pc = split_cols(hc @ w_in)
    oa, oca = gla_mixer(px[0:5], pc[0:5], gla_w_gate, gla_b_gate, gla_norm, need_ctx)
    ob, ocb = window_mixer(px[6:9], pc[6:9], win_sink, cos, sin, need_ctx)
    og, ocg = global_mixer(px[10:13], pc[10:13], glb_q_norm, glb_k_norm, cos, sin, need_ctx)
    od, ocd = diff_mixer(px[14:17], pc[14:17], diff_lambda, diff_norm, lam_init, cos, sin, need_ctx)
    z_idx = (5, 9, 13, 17)
    branches = [o * jax.nn.silu(px[i]) for o, i in zip((oa, ob, og, od), z_idx)]
    out_x = merge_branches(h, branches, w_merge, w_up, w_out)
    x_new = ln_affine(alpha * x + gate_x[:, None, :] * out_x, ln_g, ln_b)
    xc_new = None
    if need_ctx:
        branches_c = [o * jax.nn.silu(pc[i]) for o, i in zip((oca, ocb, ocg, ocd), z_idx)]
        out_c = merge_branches(hc, branches_c, w_merge, w_up, w_out)
        xc_new = ln_affine(alpha * xc + gate_c * out_c, ln_g, ln_b)
    return x_new, xc_new


def setup_inputs(seed: int = 0) -> dict:
    key = jax.random.key(seed)
    ks = jax.random.split(key, 20)
    f32 = jnp.float32
    L, D = DEPTH, D_MODEL
    beta = (8 * DEPTH) ** -0.25

    def nrm(k, shape, s):
        return jax.random.normal(k, shape, f32) * s

    return {
        'x': nrm(ks[0], (BATCH, SEQ, D), 1.0),
        'c': nrm(ks[1], (BATCH, D), 1.0),
        'ctx': nrm(ks[2], (BATCH, CTX_LEN, D), 1.0),
        'c_ctx': nrm(ks[3], (D,), 1.0),
        'w_ada': nrm(ks[4], (L, D, 3 * D), 0.5 * D ** -0.5),
        'b_ada': nrm(ks[5], (L, 3 * D), 0.02),
        'w_in': nrm(ks[6], (L, D, IN_COLS), D ** -0.5),
        'gla_w_gate': nrm(ks[7], (L, 2, GLA_RANK, GLA_HEADS * GLA_DK), GLA_RANK ** -0.5),
        'gla_b_gate': nrm(ks[8], (L, 2, GLA_HEADS * GLA_DK), 0.1),
        'gla_norm': 1.0 + nrm(ks[9], (L, BRANCH_W), 0.02),
        'win_sink': nrm(ks[10], (L, WIN_HEADS), 1.0),
        'glb_q_norm': 1.0 + nrm(ks[11], (L, HEAD_DIM), 0.02),
        'glb_k_norm': 1.0 + nrm(ks[12], (L, HEAD_DIM), 0.02),
        'diff_lambda': nrm(ks[13], (L, 4, HEAD_DIM), 0.1),
        'diff_norm': 1.0 + nrm(ks[14], (L, DIF_DV), 0.02),
        'w_merge': nrm(ks[15], (L, N_BRANCH, D, D), D ** -0.5),
        'w_up': nrm(ks[16], (L, N_BRANCH, BRANCH_W, D), beta * BRANCH_W ** -0.5),
        'w_out': nrm(ks[17], (L, D, D), beta * D ** -0.5),
        'ln_g': 1.0 + nrm(ks[18], (L, D), 0.02),
        'ln_b': nrm(ks[19], (L, D), 0.02),
    }


def reference(x, c, ctx, c_ctx, w_ada, b_ada, w_in, gla_w_gate, gla_b_gate, gla_norm, win_sink,
              glb_q_norm, glb_k_norm, diff_lambda, diff_norm, w_merge, w_up, w_out, ln_g, ln_b):
    n = x.shape[1]
    n_rows = n // GRID_W
    row = jnp.broadcast_to(jnp.arange(n_rows)[:, None], (n_rows, GRID_W)).reshape(n).astype(jnp.float32)
    col = jnp.broadcast_to(jnp.arange(GRID_W)[None, :], (n_rows, GRID_W)).reshape(n).astype(jnp.float32)
    freqs = ROPE_BASE ** (-jnp.arange(ROPE_HALF, dtype=jnp.float32) / ROPE_HALF)
    ang = jnp.stack([row[:, None] * freqs, col[:, None] * freqs], axis=1)
    cos, sin = jnp.cos(ang), jnp.sin(ang)
    xc = ctx
    for l in range(DEPTH):
        lam_init = 0.8 - 0.6 * math.exp(-0.3 * l)
        x, xc = hybrid_layer(x, xc, c, c_ctx, w_ada[l], b_ada[l], w_in[l], gla_w_gate[l], gla_b_gate[l],
                             gla_norm[l], win_sink[l], glb_q_norm[l], glb_k_norm[l], diff_lambda[l],
                             diff_norm[l], w_merge[l], w_up[l], w_out[l], ln_g[l], ln_b[l],
                             cos, sin, lam_init, l < DEPTH - 1)
    return x
```

```python
import functools
import math

import numpy as np
import jax
import jax.numpy as jnp
from jax import lax
from jax.experimental import pallas as pl
from jax.experimental.pallas import tpu as pltpu

F32 = jnp.float32
BF16 = jnp.bfloat16

DEPTH = 2
GRID_W = 64
HEAD_DIM = 64
ROPE_HALF = HEAD_DIM // 4
ROPE_BASE = 10000.0
EPS = 1e-6
NEG_INF = -1e30
GLA_HEADS, GLA_DK, GLA_DV, GLA_RANK, GLA_TAU, GLA_CHUNK = 4, 64, 128, 16, 16.0, 64
WIN_HEADS, WIN_KV, WINDOW = 8, 2, 128
GLB_HEADS, GLB_KV = 8, 4
DIF_HEADS, DIF_KV, DIF_DV = 4, 2, 128
BRANCH_W = 512

LANES = 128
TQ = 256
TK = 512
VMEM_LIMIT = 56 << 20

_A_Q, _A_K, _A_V, _A_GF, _A_Z = 0, 256, 512, 1024, 1056
_B_Q, _B_K, _B_V, _B_Z = 1568, 2080, 2208, 2336
_C_Q, _C_K, _C_V, _C_Z = 2848, 3360, 3616, 3872
_D_Q, _D_K, _D_V, _D_Z = 4384, 4896, 5152, 5408

_PERM_B = np.array([(hk * 4 + g) * 64 + d for g in range(4) for hk in range(2) for d in range(64)], np.int32)
_PERM_C = np.array([((2 * kp + r) * 2 + g) * 64 + d
                    for kp in range(2) for g in range(2) for r in range(2) for d in range(64)], np.int32)

_ar = lambda o, n: np.arange(o, o + n, dtype=np.int32)
_PLAIN_IDX = np.concatenate([
    _ar(_A_V, 512), _ar(_A_Z, 512), _B_Z + _PERM_B, _C_Z + _PERM_C, _ar(_D_Z, 512),
    _ar(_A_Q, 256), _ar(_A_K, 256), _ar(_C_V, 256), _ar(_D_V, 256), _ar(_B_V, 128),
    _ar(_A_GF, 32), np.full(96, -1, np.int32)])
_PLAIN_SCALE = np.ones(_PLAIN_IDX.shape, np.float32)
_PLAIN_SCALE[2560:2816] = GLA_DK ** -0.5
_ROPE_IDX = np.concatenate([_B_Q + _PERM_B, _ar(_D_Q, 512), _ar(_D_K, 256), _ar(_B_K, 128)])
_ROPE_SCALE = np.ones(_ROPE_IDX.shape, np.float32)
_ROPE_SCALE[0:1024] = HEAD_DIM ** -0.5
_NR_IDX = np.concatenate([_C_Q + _PERM_C, _ar(_C_K, 256)])
PL_AV, PL_AZ, PL_BZ, PL_CZ, PL_DZ = 0, 1, 2, 3, 4
PL_AQ, PL_AK, PL_CV, PL_DV = 10, 11, 12, 13
PL_BV, PL_AG = 28, 29
PLAIN_W, ROPE_W, NR_W = 3840, 1408, 768


def _cparams(sem):
    return pltpu.CompilerParams(dimension_semantics=sem, vmem_limit_bytes=VMEM_LIMIT)


def _sigmoid(x):
    return 1.0 / (1.0 + jnp.exp(-x))


def _dot(a, b):
    return jnp.dot(a, b, preferred_element_type=F32)


def _dot_nt(a, b):
    return lax.dot_general(a, b, (((1,), (1,)), ((), ())), preferred_element_type=F32)


def _dot_tn(a, b):
    return lax.dot_general(a, b, (((0,), (0,)), ((), ())), preferred_element_type=F32)


def _ada_kernel(c_ref, w_ref, b_ref, o_ref):
    c = c_ref[...]
    s = c * _sigmoid(c)
    o_ref[...] = jnp.dot(s, w_ref[...], preferred_element_type=F32,
                         precision=lax.Precision.HIGHEST) + b_ref[...]


def _ada(cs, w_ada, b_ada):
    r, d = cs.shape
    n = w_ada.shape[1]
    bn = 1024
    return pl.pallas_call(
        _ada_kernel,
        grid=(n // bn,),
        in_specs=[pl.BlockSpec((r, d), lambda j: (0, 0)),
                  pl.BlockSpec((d, bn), lambda j: (0, j)),
                  pl.BlockSpec((1, bn), lambda j: (0, j))],
        out_specs=pl.BlockSpec((r, bn), lambda j: (0, j)),
        out_shape=jax.ShapeDtypeStruct((r, n), F32),
        compiler_params=_cparams(("arbitrary",)),
        name="ada",
    )(cs, w_ada, b_ada.reshape(1, n))


def _ln_kernel(x_ref, mod_ref, h_ref):
    x = x_ref[0]
    mu = jnp.mean(x, axis=-1, keepdims=True)
    xc = x - mu
    var = jnp.mean(xc * xc, axis=-1, keepdims=True)
    y = xc * lax.rsqrt(var + EPS)
    shift = mod_ref[0, 0, 0:1, :]
    scale = mod_ref[0, 0, 1:2, :]
    h_ref[0] = (y * (1.0 + scale) + shift).astype(BF16)


def _ln_mod(x_all, mod):
    b, l, d = x_all.shape
    return pl.pallas_call(
        _ln_kernel,
        grid=(b, l // TQ),
        in_specs=[pl.BlockSpec((1, TQ, d), lambda i, j: (i, j, 0)),
                  pl.BlockSpec((1, 1, 3, d), lambda i, j: (i, jnp.minimum(j, 1), 0, 0))],
        out_specs=pl.BlockSpec((1, TQ, d), lambda i, j: (i, j, 0)),
        out_shape=jax.ShapeDtypeStruct((b, l, d), BF16),
        compiler_params=_cparams(("parallel", "parallel")),
        name="ln_mod",
    )(x_all, mod)


def _rope(y, cos, sin, lo16):
    sw = jnp.where(lo16, pltpu.roll(y, LANES - ROPE_HALF, 1), pltpu.roll(y, ROPE_HALF, 1))
    return y * cos + sw * sin


def _inproj_plain_kernel(h_ref, w_ref, o_ref):
    o_ref[0] = _dot(h_ref[0], w_ref[...]).astype(BF16)


def _inproj_rope_kernel(h_ref, w_ref, cos_ref, sin_ref, o_ref, *, width):
    acc = _dot(h_ref[0], w_ref[...])
    cos, sin = cos_ref[...], sin_ref[...]
    lo16 = (lax.broadcasted_iota(jnp.int32, cos.shape, 1) % (2 * ROPE_HALF)) < ROPE_HALF
    for c in range(width // LANES):
        y = acc[:, c * LANES:(c + 1) * LANES]
        o_ref[0, :, c * LANES:(c + 1) * LANES] = _rope(y, cos, sin, lo16).astype(BF16)


def _inproj_nr_kernel(h_ref, w_ref, cos_ref, sin_ref, gain_ref, seg_ref, o_ref, *, width):
    acc = _dot(h_ref[0], w_ref[...])
    cos, sin = cos_ref[...], sin_ref[...]
    lo16 = (lax.broadcasted_iota(jnp.int32, cos.shape, 1) % (2 * ROPE_HALF)) < ROPE_HALF
    for c in range(width // LANES):
        y = acc[:, c * LANES:(c + 1) * LANES]
        ss = _dot((y * y).astype(BF16), seg_ref[...])
        y = y * lax.rsqrt(ss * (1.0 / HEAD_DIM) + EPS) * gain_ref[:, c * LANES:(c + 1) * LANES]
        o_ref[0, :, c * LANES:(c + 1) * LANES] = _rope(y, cos, sin, lo16).astype(BF16)


def _inproj(h, w, mode, cos=None, sin=None, gain=None, seg=None):
    b, l, d = h.shape
    n = w.shape[1]
    bm = l // 4
    if mode == "plain":
        bn = 768
        return pl.pallas_call(
            _inproj_plain_kernel,
            grid=(b, l // bm, n // bn),
            in_specs=[pl.BlockSpec((1, bm, d), lambda i, r, j: (i, r, 0)),
                      pl.BlockSpec((d, bn), lambda i, r, j: (0, j))],
            out_specs=pl.BlockSpec((1, bm, bn), lambda i, r, j: (i, r, j)),
            out_shape=jax.ShapeDtypeStruct((b, l, n), BF16),
            compiler_params=_cparams(("parallel", "parallel", "arbitrary")),
            name="inproj_plain",
        )(h, w)
    tab = pl.BlockSpec((bm, LANES), lambda i, r: (r, 0))
    in_specs = [pl.BlockSpec((1, bm, d), lambda i, r: (i, r, 0)),
                pl.BlockSpec((d, n), lambda i, r: (0, 0)), tab, tab]
    args = [h, w, cos, sin]
    if mode == "rope":
        body = functools.partial(_inproj_rope_kernel, width=n)
    else:
        body = functools.partial(_inproj_nr_kernel, width=n)
        in_specs += [pl.BlockSpec((1, n), lambda i, r: (0, 0)),
                     pl.BlockSpec((LANES, LANES), lambda i, r: (0, 0))]
        args += [gain, seg]
    return pl.pallas_call(
        body,
        grid=(b, l // bm),
        in_specs=in_specs,
        out_specs=pl.BlockSpec((1, bm, n), lambda i, r: (i, r, 0)),
        out_shape=jax.ShapeDtypeStruct((b, l, n), BF16),
        compiler_params=_cparams(("parallel", "parallel")),
        name="inproj_" + mode,
    )(*args)


def _stack_halves(q):
    lane = lax.broadcasted_iota(jnp.int32, q.shape, 1)
    zero = jnp.zeros_like(q)
    return jnp.concatenate([jnp.where(lane < HEAD_DIM, q, zero), jnp.where(lane >= HEAD_DIM, q, zero)], axis=0)


def _flash_step(qs, k, v, carry):
    m, l, acc = carry
    s = _dot_nt(qs, k)
    m_new = jnp.maximum(m, jnp.max(s, axis=1, keepdims=True))
    alpha = jnp.exp(m - m_new)
    p = jnp.exp(s - m_new)
    l = alpha * l + jnp.sum(p, axis=1, keepdims=True)
    acc = alpha * acc + _dot(p.astype(BF16), v)
    return m_new, l, acc


def _flash(qs, k_ref, v_ref, n_ctx, n_lat_chunks):
    rows = qs.shape[0]
    carry = (jnp.full((rows, 1), NEG_INF, F32), jnp.zeros((rows, 1), F32), jnp.zeros((rows, LANES), F32))
    carry = _flash_step(qs, k_ref[0, 0:n_ctx, :], v_ref[0, 0:n_ctx, :], carry)

    def body(c, carry):
        r0 = pl.multiple_of(n_ctx + c * TK, LANES)
        return _flash_step(qs, k_ref[0, pl.ds(r0, TK), :], v_ref[0, pl.ds(r0, TK), :], carry)

    _, l, acc = lax.fori_loop(0, n_lat_chunks, body, carry)
    return acc / l


def _global_kernel(q_ref, k_ref, v_ref, o_ref, *, n_ctx, n_lat):
    j = pl.program_id(2)
    n_chunks = jnp.where(j == 0, 0, n_lat // TK)
    lane = lax.broadcasted_iota(jnp.int32, (TQ, LANES), 1)
    for g in range(2):
        qs = _stack_halves(q_ref[0, :, g * LANES:(g + 1) * LANES])
        o = _flash(qs, k_ref, v_ref, n_ctx, n_chunks)
        o_ref[0, :, g * LANES:(g + 1) * LANES] = jnp.where(lane < HEAD_DIM, o[0:TQ], o[TQ:2 * TQ]).astype(BF16)


def _global_mixer(p_nr, p_plain, n_ctx):
    b, l, _ = p_nr.shape
    kern = functools.partial(_global_kernel, n_ctx=n_ctx, n_lat=l - n_ctx)
    return pl.pallas_call(
        kern,
        grid=(b, 2, l // TQ),
        in_specs=[pl.BlockSpec((1, TQ, 256), lambda i, kp, j: (i, j, kp)),
                  pl.BlockSpec((1, l, LANES), lambda i, kp, j: (i, 0, 4 + kp)),
                  pl.BlockSpec((1, l, LANES), lambda i, kp, j: (i, 0, 2 * PL_CV + kp))],
        out_specs=pl.BlockSpec((1, TQ, 256), lambda i, kp, j: (i, j, kp)),
        out_shape=jax.ShapeDtypeStruct((b, l, BRANCH_W), BF16),
        compiler_params=_cparams(("parallel", "parallel", "arbitrary")),
        name="global_mixer",
    )(p_nr, p_nr, p_plain)


def _diff_kernel(q_ref, k_ref, v_ref, lam_ref, g_ref, o_ref, *, n_ctx, n_lat, lam_init):
    j = pl.program_id(2)
    n_chunks = jnp.where(j == 0, 0, n_lat // TK)
    lp = lam_ref[...]
    lam = (jnp.exp(jnp.sum(lp[0:1] * lp[1:2], axis=1, keepdims=True))
           - jnp.exp(jnp.sum(lp[2:3] * lp[3:4], axis=1, keepdims=True)) + lam_init)
    for g in range(2):
        qs = _stack_halves(q_ref[0, :, g * LANES:(g + 1) * LANES])
        o = _flash(qs, k_ref, v_ref, n_ctx, n_chunks)
        o = o[0:TQ] - lam * o[TQ:2 * TQ]
        y = o * lax.rsqrt(jnp.mean(o * o, axis=-1, keepdims=True) + EPS) * g_ref[...]
        o_ref[0, :, g * LANES:(g + 1) * LANES] = (y * (1.0 - lam_init)).astype(BF16)


def _diff_mixer(p_rope, p_plain, lam_p, sub_g, lam_init, n_ctx):
    b, l, _ = p_rope.shape
    kern = functools.partial(_diff_kernel, n_ctx=n_ctx, n_lat=l - n_ctx, lam_init=lam_init)
    return pl.pallas_call(
        kern,
        grid=(b, 2, l // TQ),
        in_specs=[pl.BlockSpec((1, TQ, 256), lambda i, hk, j: (i, j, 2 + hk)),
                  pl.BlockSpec((1, l, LANES), lambda i, hk, j: (i, 0, 8 + hk)),
                  pl.BlockSpec((1, l, LANES), lambda i, hk, j: (i, 0, 2 * PL_DV + hk)),
                  pl.BlockSpec((4, HEAD_DIM), lambda i, hk, j: (0, 0)),
                  pl.BlockSpec((1, DIF_DV), lambda i, hk, j: (0, 0))],
        out_specs=pl.BlockSpec((1, TQ, 256), lambda i, hk, j: (i, j, hk)),
        out_shape=jax.ShapeDtypeStruct((b, l, BRANCH_W), BF16),
        compiler_params=_cparams(("parallel", "parallel", "arbitrary")),
        name="diff_mixer",
    )(p_rope, p_rope, p_plain, lam_p, sub_g.reshape(1, DIF_DV))


def _window_kernel(sink_ref, q_ref, k_ref, v_ref, o_ref, *, n_ctx, total):
    j = pl.program_id(1)
    span = TQ + 2 * WINDOW
    start = pl.multiple_of(jnp.clip(j * TQ - WINDOW, 0, total - span), LANES)
    kc, vc = k_ref[0, 0:n_ctx, :], v_ref[0, 0:n_ctx, :]
    kw, vw = k_ref[0, pl.ds(start, span), :], v_ref[0, pl.ds(start, span), :]
    row = lax.broadcasted_iota(jnp.int32, (2 * TQ, span), 0)
    qpos = j * TQ + jnp.where(row >= TQ, row - TQ, row)
    kpos = start + lax.broadcasted_iota(jnp.int32, (2 * TQ, span), 1)
    valid = (kpos >= n_ctx) & (qpos >= n_ctx) & (jnp.abs(kpos - qpos) <= WINDOW)
    srow = lax.broadcasted_iota(jnp.int32, (2 * TQ, 1), 0)
    lane = lax.broadcasted_iota(jnp.int32, (TQ, LANES), 1)
    for g in range(WIN_HEADS // WIN_KV):
        qs = _stack_halves(q_ref[0, :, g * LANES:(g + 1) * LANES])
        sink = jnp.where(srow < TQ, sink_ref[g], sink_ref[WIN_HEADS // WIN_KV + g])
        s_c = _dot_nt(qs, kc)
        s_w = jnp.where(valid, _dot_nt(qs, kw), NEG_INF)
        m = jnp.maximum(jnp.maximum(jnp.max(s_c, axis=1, keepdims=True), jnp.max(s_w, axis=1, keepdims=True)), sink)
        p_c = jnp.exp(s_c - m)
        p_w = jnp.exp(s_w - m)
        den = jnp.sum(p_c, axis=1, keepdims=True) + jnp.sum(p_w, axis=1, keepdims=True) + jnp.exp(sink - m)
        o = (_dot(p_c.astype(BF16), vc) + _dot(p_w.astype(BF16), vw)) / den
        o_ref[0, :, g * LANES:(g + 1) * LANES] = jnp.where(lane < HEAD_DIM, o[0:TQ], o[TQ:2 * TQ]).astype(BF16)


def _window_mixer(p_rope, p_plain, sink, n_ctx):
    b, l, _ = p_rope.shape
    kern = functools.partial(_window_kernel, n_ctx=n_ctx, total=l)
    grid_spec = pltpu.PrefetchScalarGridSpec(
        num_scalar_prefetch=1,
        grid=(b, l // TQ),
        in_specs=[pl.BlockSpec((1, TQ, 512), lambda i, j, s: (i, j, 0)),
                  pl.BlockSpec((1, l, LANES), lambda i, j, s: (i, 0, 10)),
                  pl.BlockSpec((1, l, LANES), lambda i, j, s: (i, 0, PL_BV))],
        out_specs=pl.BlockSpec((1, TQ, 512), lambda i, j, s: (i, j, 0)),
    )
    return pl.pallas_call(
        kern,
        grid_spec=grid_spec,
        out_shape=jax.ShapeDtypeStruct((b, l, BRANCH_W), BF16),
        compiler_params=_cparams(("parallel", "arbitrary")),
        name="window_mixer",
    )(sink, p_rope, p_rope, p_plain)


def _gla_kernel(q_ref, k_ref, v_ref, g_ref, wg_ref, bg_ref, gn_ref, o_ref, acc_s, st_s, *, total, n_ctx):
    ck = GLA_CHUNK
    n_chunks = total // ck
    ctx_chunks = n_ctx // ck
    hw = GLA_HEADS * GLA_DK

    acc_s[...] = jnp.zeros_like(acc_s)
    st_s[...] = jnp.zeros_like(st_s)

    rr = lax.broadcasted_iota(jnp.int32, (ck, ck), 0)
    cc = lax.broadcasted_iota(jnp.int32, (ck, ck), 1)
    tri = (rr >= cc, rr <= cc)
    tri_b = tuple(jnp.where(t, 1.0, 0.0).astype(BF16) for t in tri)
    tri2 = tuple(jnp.concatenate([t, t], axis=0) for t in tri)
    lane = lax.broadcasted_iota(jnp.int32, (LANES, LANES), 1)

    def body(i, _):
        cf = i
        cb = jnp.where(i < ctx_chunks, ctx_chunks - 1 - i, n_chunks - 1 - (i - ctx_chunks))
        for d, c in ((0, cf), (1, cb)):
            r0 = pl.multiple_of(c * ck, ck)
            y = _dot(g_ref[0, pl.ds(r0, ck), :], wg_ref[:, d * hw:(d + 1) * hw]) + bg_ref[:, d * hw:(d + 1) * hw]
            la = (jnp.minimum(y, 0.0) - jnp.log(1.0 + jnp.exp(-jnp.abs(y)))) * (1.0 / GLA_TAU)
            hi = la.astype(BF16)
            lo = (la - hi.astype(F32)).astype(BF16)
            cum = _dot(tri_b[d], hi) + _dot(tri_b[d], lo)
            tot = cum[ck - 1:ck, :] if d == 0 else cum[0:1, :]
            qf = q_ref[0, pl.ds(r0, ck), :].astype(F32)
            kf = k_ref[0, pl.ds(r0, ck), :].astype(F32)
            qt = (qf * jnp.exp(cum)).astype(BF16)
            kt = (kf * jnp.exp(-cum)).astype(BF16)
            kw = (kf * jnp.exp(tot - cum)).astype(BF16)
            dec = jnp.exp(tot)
            for p in range(2):
                sl = slice(p * LANES, (p + 1) * LANES)
                qs = _stack_halves(qt[:, sl])
                a2 = jnp.where(tri2[d], _dot_nt(qs, kt[:, sl]), 0.0).astype(BF16)
                st = st_s[2 * d + p]
                inter = _dot_nt(qs, st.astype(BF16))
                upd = []
                for hh in range(2):
                    vs = slice((2 * p + hh) * GLA_DV, (2 * p + hh + 1) * GLA_DV)
                    vh = v_ref[0, pl.ds(r0, ck), vs]
                    o = _dot(a2[hh * ck:(hh + 1) * ck], vh) + inter[hh * ck:(hh + 1) * ck]
                    acc_s[pl.ds(r0, ck), vs] += o
                    upd.append(_dot_tn(vh, kw[:, sl]))
                st_s[2 * d + p] = st * dec[:, sl] + jnp.where(lane < GLA_DK, upd[0], upd[1])
        return 0

    lax.fori_loop(0, n_chunks, body, 0)

    def finish(i, _):
        r0 = pl.multiple_of(i * TQ, TQ)
        for h in range(GLA_HEADS):
            vs = slice(h * GLA_DV, (h + 1) * GLA_DV)
            o = acc_s[pl.ds(r0, TQ), vs]
            y = o * lax.rsqrt(jnp.mean(o * o, axis=-1, keepdims=True) + EPS) * gn_ref[:, vs]
            o_ref[0, pl.ds(r0, TQ), vs] = y.astype(BF16)
        return 0

    lax.fori_loop(0, total // TQ, finish, 0)


def _gla_mixer(p_plain, wg, bg, gn, n_ctx):
    b, l, _ = p_plain.shape
    kern = functools.partial(_gla_kernel, total=l, n_ctx=n_ctx)
    const = lambda i: (0, 0)
    return pl.pallas_call(
        kern,
        grid=(b,),
        in_specs=[pl.BlockSpec((1, l, 256), lambda i: (i, 0, PL_AQ)),
                  pl.BlockSpec((1, l, 256), lambda i: (i, 0, PL_AK)),
                  pl.BlockSpec((1, l, 512), lambda i: (i, 0, PL_AV)),
                  pl.BlockSpec((1, l, LANES), lambda i: (i, 0, PL_AG)),
                  pl.BlockSpec((LANES, 512), const),
                  pl.BlockSpec((1, 512), const),
                  pl.BlockSpec((1, 512), const)],
        out_specs=pl.BlockSpec((1, l, BRANCH_W), lambda i: (i, 0, 0)),
        out_shape=jax.ShapeDtypeStruct((b, l, BRANCH_W), BF16),
        scratch_shapes=[pltpu.VMEM((l, BRANCH_W), F32), pltpu.VMEM((4, GLA_DV, LANES), F32)],
        compiler_params=_cparams(("parallel",)),
        name="gla_mixer",
    )(p_plain, p_plain, p_plain, p_plain, wg, bg, gn)


def _merge_kernel(h_ref, oa_ref, ob_ref, oc_ref, od_ref, za_ref, zb_ref, zc_ref, zd_ref, x_ref, mod_ref,
                  wm_ref, wup_ref, wout_ref, lng_ref, lnb_ref, out_ref, *, alpha):
    h = h_ref[0]
    acc = None
    for i, (o_ref, z_ref) in enumerate(((oa_ref, za_ref), (ob_ref, zb_ref), (oc_ref, zc_ref), (od_ref, zd_ref))):
        z = z_ref[0].astype(F32)
        br = (o_ref[0].astype(F32) * (z * _sigmoid(z))).astype(BF16)
        term = _sigmoid(_dot(h, wm_ref[i])) * _dot(br, wup_ref[i])
        acc = term if acc is None else acc + term
    y = _dot(acc.astype(BF16), wout_ref[...])
    r = alpha * x_ref[0] + mod_ref[0, 0, 2:3, :] * y
    mu = jnp.mean(r, axis=-1, keepdims=True)
    rc = r - mu
    var = jnp.mean(rc * rc, axis=-1, keepdims=True)
    out_ref[0] = rc * lax.rsqrt(var + EPS) * lng_ref[...] + lnb_ref[...]


def _merge(h, outs, p_plain, x_all, mod, wm, wup, wout, ln_g, ln_b, skip_ctx):
    b, l, d = h.shape
    off = 1 if skip_ctx else 0
    nt = l // TQ - off
    row = lambda i, j: (i, j + off, 0)
    zspec = lambda blk: pl.BlockSpec((1, TQ, BRANCH_W), lambda i, j: (i, j + off, blk))
    c2 = lambda i, j: (0, 0)
    c3 = lambda i, j: (0, 0, 0)
    in_specs = ([pl.BlockSpec((1, TQ, d), row)]
                + [pl.BlockSpec((1, TQ, BRANCH_W), row)] * 4
                + [zspec(PL_AZ), zspec(PL_BZ), zspec(PL_CZ), zspec(PL_DZ)]
                + [pl.BlockSpec((1, TQ, d), row),
                   pl.BlockSpec((1, 1, 3, d), lambda i, j: (i, jnp.minimum(j + off, 1), 0, 0)),
                   pl.BlockSpec((4, d, d), c3), pl.BlockSpec((4, BRANCH_W, d), c3), pl.BlockSpec((d, d), c2),
                   pl.BlockSpec((1, d), c2), pl.BlockSpec((1, d), c2)])
    return pl.pallas_call(
        functools.partial(_merge_kernel, alpha=(2 * DEPTH) ** 0.25),
        grid=(b, nt),
        in_specs=in_specs,
        out_specs=pl.BlockSpec((1, TQ, d), lambda i, j: (i, j, 0)),
        out_shape=jax.ShapeDtypeStruct((b, nt * TQ, d), F32),
        compiler_params=_cparams(("parallel", "parallel")),
        name="merge",
    )(h, *outs, p_plain, p_plain, p_plain, p_plain, x_all, mod, wm, wup, wout,
      ln_g.reshape(1, d), ln_b.reshape(1, d))


def _gather_cols(w, idx, scale=None):
    cols = jnp.take(w, jnp.asarray(np.maximum(idx, 0)), axis=1)
    cols = jnp.where(jnp.asarray(idx >= 0)[None, :], cols, 0.0)
    if scale is not None:
        cols = cols * jnp.asarray(scale)[None, :]
    return cols.astype(BF16)


def _rope_tables(n_ctx, n_lat):
    t = np.arange(n_lat)
    freqs = ROPE_BASE ** (-np.arange(ROPE_HALF, dtype=np.float32) / ROPE_HALF)
    pos = np.stack([(t // GRID_W).astype(np.float32), (t % GRID_W).astype(np.float32)], axis=1)
    ang = jnp.asarray(pos[:, :, None] * freqs[None, None, :], F32)
    cos, sin = jnp.cos(ang), jnp.sin(ang)
    cos_h = jnp.concatenate([cos, cos], axis=-1).reshape(n_lat, HEAD_DIM)
    sin_h = jnp.concatenate([-sin, sin], axis=-1).reshape(n_lat, HEAD_DIM)
    cos_t = jnp.concatenate([jnp.ones((n_ctx, HEAD_DIM), F32), cos_h], axis=0)
    sin_t = jnp.concatenate([jnp.zeros((n_ctx, HEAD_DIM), F32), sin_h], axis=0)
    return jnp.tile(cos_t, (1, 2)), jnp.tile(sin_t, (1, 2))


def kernel(x, c, ctx, c_ctx, w_ada, b_ada, w_in, gla_w_gate, gla_b_gate, gla_norm, win_sink, glb_q_norm,
           glb_k_norm, diff_lambda, diff_norm, w_merge, w_up, w_out, ln_g, ln_b):
    b, n_lat, d = x.shape
    n_ctx = ctx.shape[1]
    assert n_ctx == TQ and n_lat % TK == 0 and d == 1024
    x_all = jnp.concatenate([ctx, x], axis=1)
    cs = jnp.zeros((16, d), F32).at[0:b].set(c).at[b].set(c_ctx)
    cos_t, sin_t = _rope_tables(n_ctx, n_lat)
    seg = jnp.asarray(np.kron(np.eye(2, dtype=np.float32), np.ones((HEAD_DIM, HEAD_DIM), np.float32)), BF16)

    for layer in range(DEPTH):
        last = layer == DEPTH - 1
        lam_init = 0.8 - 0.6 * math.exp(-0.3 * layer)
        w_l = w_in[layer]
        w_plain = _gather_cols(w_l, _PLAIN_IDX, _PLAIN_SCALE)
        w_rope = _gather_cols(w_l, _ROPE_IDX, _ROPE_SCALE)
        w_nr = _gather_cols(w_l, _NR_IDX)
        gain_nr = jnp.concatenate([jnp.tile(glb_q_norm[layer] * HEAD_DIM ** -0.5, GLB_HEADS),
                                   jnp.tile(glb_k_norm[layer], GLB_KV)]).reshape(1, NR_W)
        wg = jnp.zeros((LANES, 2 * GLA_HEADS * GLA_DK), F32)
        wg = wg.at[0:GLA_RANK, 0:256].set(gla_w_gate[layer, 0]).at[GLA_RANK:2 * GLA_RANK, 256:512].set(gla_w_gate[layer, 1])
        bg = gla_b_gate[layer].reshape(1, 512)
        wup = jnp.stack([w_up[layer, 0], w_up[layer, 1][_PERM_B], w_up[layer, 2][_PERM_C], w_up[layer, 3]]).astype(BF16)

        ada = _ada(cs, w_ada[layer], b_ada[layer])
        mod_x = ada[0:b].reshape(b, 1, 3, d)
        mod_c = jnp.broadcast_to(ada[b].reshape(1, 1, 3, d), (b, 1, 3, d))
        mod = jnp.concatenate([mod_c, mod_x], axis=1)

        h = _ln_mod(x_all, mod)
        p_plain = _inproj(h, w_plain, "plain")
        p_rope = _inproj(h, w_rope, "rope", cos_t, sin_t)
        p_nr = _inproj(h, w_nr, "nr", cos_t, sin_t, gain_nr, seg)

        o_a = _gla_mixer(p_plain, wg.astype(BF16), bg, gla_norm[layer].reshape(1, BRANCH_W), n_ctx)
        o_b = _window_mixer(p_rope, p_plain, win_sink[layer], n_ctx)
        o_c = _global_mixer(p_nr, p_plain, n_ctx)
        o_d = _diff_mixer(p_rope, p_plain, diff_lambda[layer], diff_norm[layer], lam_init, n_ctx)

        x_all = _merge(h, (o_a, o_b, o_c, o_d), p_plain, x_all, mod, w_merge[layer].astype(BF16), wup,
                       w_out[layer].astype(BF16), ln_g[layer], ln_b[layer], skip_ctx=last)
    return x_all
```

```python
import functools
import math

import numpy as np
import jax
import jax.numpy as jnp
from jax import lax
from jax.experimental import pallas as pl
from jax.experimental.pallas import tpu as pltpu

F32 = jnp.float32
BF16 = jnp.bfloat16

DEPTH = 2
GRID_W = 64
HEAD_DIM = 64
ROPE_HALF = HEAD_DIM // 4
ROPE_BASE = 10000.0
EPS = 1e-6
NEG_INF = -1e30
GLA_HEADS, GLA_DK, GLA_DV, GLA_RANK, GLA_TAU, GLA_CHUNK = 4, 64, 128, 16, 16.0, 64
WIN_HEADS, WIN_KV, WINDOW = 8, 2, 128
GLB_HEADS, GLB_KV = 8, 4
DIF_HEADS, DIF_KV, DIF_DV = 4, 2, 128
BRANCH_W = 512

LANES = 128
TQ = 256
TK = 512
VMEM_LIMIT = 56 << 20

_A_Q, _A_K, _A_V, _A_GF, _A_Z = 0, 256, 512, 1024, 1056
_B_Q, _B_K, _B_V, _B_Z = 1568, 2080, 2208, 2336
_C_Q, _C_K, _C_V, _C_Z = 2848, 3360, 3616, 3872
_D_Q, _D_K, _D_V, _D_Z = 4384, 4896, 5152, 5408

_PERM_B = np.array([(hk * 4 + g) * 64 + d for g in range(4) for hk in range(2) for d in range(64)], np.int32)
_PERM_C = np.array([((2 * kp + r) * 2 + g) * 64 + d
                    for kp in range(2) for g in range(2) for r in range(2) for d in range(64)], np.int32)

_ar = lambda o, n: np.arange(o, o + n, dtype=np.int32)
_PLAIN_IDX = np.concatenate([
    _ar(_A_V, 512), _ar(_A_Z, 512), _B_Z + _PERM_B, _C_Z + _PERM_C, _ar(_D_Z, 512),
    _ar(_A_Q, 256), _ar(_A_K, 256), _ar(_C_V, 256), _ar(_D_V, 256), _ar(_B_V, 128),
    _ar(_A_GF, 32), np.full(96, -1, np.int32)])
_PLAIN_SCALE = np.ones(_PLAIN_IDX.shape, np.float32)
_PLAIN_SCALE[2560:2816] = GLA_DK ** -0.5
_ROPE_IDX = np.concatenate([_B_Q + _PERM_B, _ar(_D_Q, 512), _ar(_D_K, 256), _ar(_B_K, 128)])
_ROPE_SCALE = np.ones(_ROPE_IDX.shape, np.float32)
_ROPE_SCALE[0:1024] = HEAD_DIM ** -0.5
_NR_IDX = np.concatenate([_C_Q + _PERM_C, _ar(_C_K, 256)])
PL_AV, PL_AZ, PL_BZ, PL_CZ, PL_DZ = 0, 1, 2, 3, 4
PL_AQ, PL_AK, PL_CV, PL_DV = 10, 11, 12, 13
PL_BV, PL_AG = 28, 29
PLAIN_W, ROPE_W, NR_W = 3840, 1408, 768
ROPE_QW = 1024
LOG2E = math.log2(math.e)


def _cparams(sem):
    return pltpu.CompilerParams(dimension_semantics=sem, vmem_limit_bytes=VMEM_LIMIT)


def _sigmoid(x):
    return 1.0 / (1.0 + jnp.exp(-x))


def _dot(a, b):
    return jnp.dot(a, b, preferred_element_type=F32)


def _dot_nt(a, b):
    return lax.dot_general(a, b, (((1,), (1,)), ((), ())), preferred_element_type=F32)


def _dot_tn(a, b):
    return lax.dot_general(a, b, (((0,), (0,)), ((), ())), preferred_element_type=F32)


def _ada_kernel(c_ref, w_ref, b_ref, o_ref):
    c = c_ref[...]
    s = c * _sigmoid(c)
    o_ref[...] = jnp.dot(s, w_ref[...], preferred_element_type=F32,
                         precision=lax.Precision.HIGHEST) + b_ref[...]


def _ada(cs, w_ada, b_ada):
    r, d = cs.shape
    n = w_ada.shape[1]
    bn = 1024
    return pl.pallas_call(
        _ada_kernel,
        grid=(n // bn,),
        in_specs=[pl.BlockSpec((r, d), lambda j: (0, 0)),
                  pl.BlockSpec((d, bn), lambda j: (0, j)),
                  pl.BlockSpec((1, bn), lambda j: (0, j))],
        out_specs=pl.BlockSpec((r, bn), lambda j: (0, j)),
        out_shape=jax.ShapeDtypeStruct((r, n), F32),
        compiler_params=_cparams(("arbitrary",)),
        name="ada",
    )(cs, w_ada, b_ada.reshape(1, n))


def _ln_kernel(x_ref, mod_ref, h_ref):
    x = x_ref[0]
    mu = jnp.mean(x, axis=-1, keepdims=True)
    xc = x - mu
    var = jnp.mean(xc * xc, axis=-1, keepdims=True)
    y = xc * lax.rsqrt(var + EPS)
    shift = mod_ref[0, 0, 0:1, :]
    scale = mod_ref[0, 0, 1:2, :]
    h_ref[0] = (y * (1.0 + scale) + shift).astype(BF16)


def _ln_mod(x_all, mod):
    b, l, d = x_all.shape
    return pl.pallas_call(
        _ln_kernel,
        grid=(b, l // TQ),
        in_specs=[pl.BlockSpec((1, TQ, d), lambda i, j: (i, j, 0)),
                  pl.BlockSpec((1, 1, 3, d), lambda i, j: (i, jnp.minimum(j, 1), 0, 0))],
        out_specs=pl.BlockSpec((1, TQ, d), lambda i, j: (i, j, 0)),
        out_shape=jax.ShapeDtypeStruct((b, l, d), BF16),
        compiler_params=_cparams(("parallel", "parallel")),
        name="ln_mod",
    )(x_all, mod)


def _rope(y, cos, sin, lo16):
    sw = jnp.where(lo16, pltpu.roll(y, LANES - ROPE_HALF, 1), pltpu.roll(y, ROPE_HALF, 1))
    return y * cos + sw * sin


def _inproj_plain_kernel(h_ref, w_ref, o_ref):
    o_ref[0] = _dot(h_ref[0], w_ref[...]).astype(BF16)


def _inproj_rope_kernel(h_ref, w_ref, cos_ref, sin_ref, o_ref, *, width, q_width):
    acc = _dot(h_ref[0], w_ref[...])
    cos, sin = cos_ref[...], sin_ref[...]
    lo16 = (lax.broadcasted_iota(jnp.int32, cos.shape, 1) % (2 * ROPE_HALF)) < ROPE_HALF
    for c in range(width // LANES):
        y = _rope(acc[:, c * LANES:(c + 1) * LANES], cos, sin, lo16)
        if c * LANES < q_width:
            y = y * LOG2E
        o_ref[0, :, c * LANES:(c + 1) * LANES] = y.astype(BF16)


def _inproj_nr_kernel(h_ref, w_ref, cos_ref, sin_ref, gain_ref, seg_ref, o_ref, *, width):
    acc = _dot(h_ref[0], w_ref[...])
    cos, sin = cos_ref[...], sin_ref[...]
    lo16 = (lax.broadcasted_iota(jnp.int32, cos.shape, 1) % (2 * ROPE_HALF)) < ROPE_HALF
    for c in range(width // LANES):
        y = acc[:, c * LANES:(c + 1) * LANES]
        ss = _dot((y * y).astype(BF16), seg_ref[...])
        y = y * lax.rsqrt(ss * (1.0 / HEAD_DIM) + EPS) * gain_ref[:, c * LANES:(c + 1) * LANES]
        o_ref[0, :, c * LANES:(c + 1) * LANES] = _rope(y, cos, sin, lo16).astype(BF16)


def _inproj(h, w, mode, cos=None, sin=None, gain=None, seg=None):
    b, l, d = h.shape
    n = w.shape[1]
    bm = l // 4
    if mode == "plain":
        bn = 768
        return pl.pallas_call(
            _inproj_plain_kernel,
            grid=(b, l // bm, n // bn),
            in_specs=[pl.BlockSpec((1, bm, d), lambda i, r, j: (i, r, 0)),
                      pl.BlockSpec((d, bn), lambda i, r, j: (0, j))],
            out_specs=pl.BlockSpec((1, bm, bn), lambda i, r, j: (i, r, j)),
            out_shape=jax.ShapeDtypeStruct((b, l, n), BF16),
            compiler_params=_cparams(("parallel", "parallel", "arbitrary")),
            name="inproj_plain",
        )(h, w)
    tab = pl.BlockSpec((bm, LANES), lambda i, r: (r, 0))
    in_specs = [pl.BlockSpec((1, bm, d), lambda i, r: (i, r, 0)),
                pl.BlockSpec((d, n), lambda i, r: (0, 0)), tab, tab]
    args = [h, w, cos, sin]
    if mode == "rope":
        body = functools.partial(_inproj_rope_kernel, width=n, q_width=ROPE_QW)
    else:
        body = functools.partial(_inproj_nr_kernel, width=n)
        in_specs += [pl.BlockSpec((1, n), lambda i, r: (0, 0)),
                     pl.BlockSpec((LANES, LANES), lambda i, r: (0, 0))]
        args += [gain, seg]
    return pl.pallas_call(
        body,
        grid=(b, l // bm),
        in_specs=in_specs,
        out_specs=pl.BlockSpec((1, bm, n), lambda i, r: (i, r, 0)),
        out_shape=jax.ShapeDtypeStruct((b, l, n), BF16),
        compiler_params=_cparams(("parallel", "parallel")),
        name="inproj_" + mode,
    )(*args)


def _stack_halves(q):
    lane = lax.broadcasted_iota(jnp.int32, q.shape, 1)
    zero = jnp.zeros_like(q)
    return jnp.concatenate([jnp.where(lane < HEAD_DIM, q, zero), jnp.where(lane >= HEAD_DIM, q, zero)], axis=0)


FLASH_ROWS = 2 * TQ


def _flash_scratch(n_streams):
    stat = pltpu.VMEM((FLASH_ROWS, LANES), F32)
    one = [pltpu.VMEM((FLASH_ROWS, LANES), BF16), stat, stat, stat,
           pltpu.VMEM((FLASH_ROWS, TK), F32), pltpu.VMEM((FLASH_ROWS, TK), BF16)]
    return one * n_streams


def _flash(q_blocks, kv_blocks, k_ref, v_ref, n_ctx, n_lat_chunks, with_latent, scratch):
    ns = len(q_blocks)
    streams = [scratch[6 * i:6 * (i + 1)] for i in range(ns)]
    kv_lanes = [slice(kb * LANES, (kb + 1) * LANES) for kb in kv_blocks]
    for q, (qs_s, m_s, l_s, acc_s, _, _) in zip(q_blocks, streams):
        qs_s[...] = _stack_halves(q)
        m_s[...] = jnp.full((FLASH_ROWS, LANES), NEG_INF, F32)
        l_s[...] = jnp.zeros((FLASH_ROWS, LANES), F32)
        acc_s[...] = jnp.zeros((FLASH_ROWS, LANES), F32)

    def step(r0, tk):
        for (qs_s, _, _, _, s_s, _), kv in zip(streams, kv_lanes):
            s_s[:, 0:tk] = _dot_nt(qs_s[...], k_ref[0, pl.ds(r0, tk), kv])
        for (_, m_s, l_s, acc_s, s_s, p_s), kv in zip(streams, kv_lanes):
            s = s_s[:, 0:tk]
            m_old = m_s[...]
            m_new = jnp.maximum(m_old, jnp.max(s, axis=1, keepdims=True))
            alpha = jnp.exp2(m_old - m_new)
            p = jnp.exp2(s - jnp.concatenate([m_new] * (tk // LANES), axis=1))
            p_s[:, 0:tk] = p.astype(BF16)
            psum = p[:, 0:LANES]
            for t in range(1, tk // LANES):
                psum = psum + p[:, t * LANES:(t + 1) * LANES]
            l_s[...] = alpha * l_s[...] + psum
            m_s[...] = m_new
            acc_s[...] = alpha * acc_s[...] + _dot(p_s[:, 0:tk], v_ref[0, pl.ds(r0, tk), kv])

    step(0, n_ctx)

    def body(c, _):
        step(pl.multiple_of(n_ctx + c * TK, LANES), TK)
        return 0

    @pl.when(with_latent)
    def _():
        lax.fori_loop(0, n_lat_chunks, body, 0)

    return [acc_s[...] / jnp.sum(l_s[...], axis=1, keepdims=True) for _, _, l_s, acc_s, _, _ in streams]


def _global_kernel(q_ref, k_ref, v_ref, o_ref, *scratch, n_ctx, n_lat):
    lane = lax.broadcasted_iota(jnp.int32, (TQ, LANES), 1)
    blocks = range(GLB_HEADS // 2)
    outs = _flash([q_ref[0, :, c * LANES:(c + 1) * LANES] for c in blocks], [c // 2 for c in blocks],
                  k_ref, v_ref, n_ctx, n_lat // TK, pl.program_id(1) > 0, scratch)
    for c, o in enumerate(outs):
        o_ref[0, :, c * LANES:(c + 1) * LANES] = jnp.where(lane < HEAD_DIM, o[0:TQ], o[TQ:2 * TQ]).astype(BF16)


def _global_mixer(p_nr, p_plain, n_ctx):
    b, l, _ = p_nr.shape
    kern = functools.partial(_global_kernel, n_ctx=n_ctx, n_lat=l - n_ctx)
    return pl.pallas_call(
        kern,
        grid=(b, l // TQ),
        in_specs=[pl.BlockSpec((1, TQ, 512), lambda i, j: (i, j, 0)),
                  pl.BlockSpec((1, l, 256), lambda i, j: (i, 0, 2)),
                  pl.BlockSpec((1, l, 256), lambda i, j: (i, 0, PL_CV))],
        out_specs=pl.BlockSpec((1, TQ, 512), lambda i, j: (i, j, 0)),
        out_shape=jax.ShapeDtypeStruct((b, l, BRANCH_W), BF16),
        scratch_shapes=_flash_scratch(GLB_HEADS // 2),
        compiler_params=_cparams(("parallel", "arbitrary")),
        name="global_mixer",
    )(p_nr, p_nr, p_plain)


def _diff_kernel(q_ref, k_ref, v_ref, lam_ref, g_ref, o_ref, *scratch, n_ctx, n_lat, lam_init):
    lp = lam_ref[...]
    lam = (jnp.exp(jnp.sum(lp[0:1] * lp[1:2], axis=1, keepdims=True))
           - jnp.exp(jnp.sum(lp[2:3] * lp[3:4], axis=1, keepdims=True)) + lam_init)
    blocks = range(DIF_HEADS)
    outs = _flash([q_ref[0, :, c * LANES:(c + 1) * LANES] for c in blocks], [c // 2 for c in blocks],
                  k_ref, v_ref, n_ctx, n_lat // TK, pl.program_id(1) > 0, scratch)
    for c, o in enumerate(outs):
        o = o[0:TQ] - lam * o[TQ:2 * TQ]
        y = o * lax.rsqrt(jnp.mean(o * o, axis=-1, keepdims=True) + EPS) * g_ref[...]
        o_ref[0, :, c * LANES:(c + 1) * LANES] = (y * (1.0 - lam_init)).astype(BF16)


def _diff_mixer(p_rope, p_plain, lam_p, sub_g, lam_init, n_ctx):
    b, l, _ = p_rope.shape
    kern = functools.partial(_diff_kernel, n_ctx=n_ctx, n_lat=l - n_ctx, lam_init=lam_init)
    return pl.pallas_call(
        kern,
        grid=(b, l // TQ),
        in_specs=[pl.BlockSpec((1, TQ, 512), lambda i, j: (i, j, 1)),
                  pl.BlockSpec((1, l, 256), lambda i, j: (i, 0, 4)),
                  pl.BlockSpec((1, l, 256), lambda i, j: (i, 0, PL_DV)),
                  pl.BlockSpec((4, HEAD_DIM), lambda i, j: (0, 0)),
                  pl.BlockSpec((1, DIF_DV), lambda i, j: (0, 0))],
        out_specs=pl.BlockSpec((1, TQ, 512), lambda i, j: (i, j, 0)),
        out_shape=jax.ShapeDtypeStruct((b, l, BRANCH_W), BF16),
        scratch_shapes=_flash_scratch(DIF_HEADS),
        compiler_params=_cparams(("parallel", "arbitrary")),
        name="diff_mixer",
    )(p_rope, p_rope, p_plain, lam_p, sub_g.reshape(1, DIF_DV))


def _window_kernel(sink_ref, q_ref, k_ref, v_ref, o_ref, *, n_ctx, total):
    j = pl.program_id(1)
    span = TQ + 2 * WINDOW
    start = pl.multiple_of(jnp.clip(j * TQ - WINDOW, 0, total - span), LANES)
    kc, vc = k_ref[0, 0:n_ctx, :], v_ref[0, 0:n_ctx, :]
    kw, vw = k_ref[0, pl.ds(start, span), :], v_ref[0, pl.ds(start, span), :]
    row = lax.broadcasted_iota(jnp.int32, (2 * TQ, span), 0)
    qpos = j * TQ + jnp.where(row >= TQ, row - TQ, row)
    kpos = start + lax.broadcasted_iota(jnp.int32, (2 * TQ, span), 1)
    valid = (kpos >= n_ctx) & (qpos >= n_ctx) & (jnp.abs(kpos - qpos) <= WINDOW)
    srow = lax.broadcasted_iota(jnp.int32, (2 * TQ, 1), 0)
    lane = lax.broadcasted_iota(jnp.int32, (TQ, LANES), 1)
    for g in range(WIN_HEADS // WIN_KV):
        qs = _stack_halves(q_ref[0, :, g * LANES:(g + 1) * LANES])
        sink = jnp.where(srow < TQ, sink_ref[g], sink_ref[WIN_HEADS // WIN_KV + g]) * LOG2E
        s_c = _dot_nt(qs, kc)
        s_w = jnp.where(valid, _dot_nt(qs, kw), NEG_INF)
        m = jnp.maximum(jnp.maximum(jnp.max(s_c, axis=1, keepdims=True), jnp.max(s_w, axis=1, keepdims=True)), sink)
        p_c = jnp.exp2(s_c - m)
        p_w = jnp.exp2(s_w - m)
        den = jnp.sum(p_c, axis=1, keepdims=True) + jnp.sum(p_w, axis=1, keepdims=True) + jnp.exp2(sink - m)
        o = (_dot(p_c.astype(BF16), vc) + _dot(p_w.astype(BF16), vw)) / den
        o_ref[0, :, g * LANES:(g + 1) * LANES] = jnp.where(lane < HEAD_DIM, o[0:TQ], o[TQ:2 * TQ]).astype(BF16)


def _window_mixer(p_rope, p_plain, sink, n_ctx):
    b, l, _ = p_rope.shape
    kern = functools.partial(_window_kernel, n_ctx=n_ctx, total=l)
    grid_spec = pltpu.PrefetchScalarGridSpec(
        num_scalar_prefetch=1,
        grid=(b, l // TQ),
        in_specs=[pl.BlockSpec((1, TQ, 512), lambda i, j, s: (i, j, 0)),
                  pl.BlockSpec((1, l, LANES), lambda i, j, s: (i, 0, 10)),
                  pl.BlockSpec((1, l, LANES), lambda i, j, s: (i, 0, PL_BV))],
        out_specs=pl.BlockSpec((1, TQ, 512), lambda i, j, s: (i, j, 0)),
    )
    return pl.pallas_call(
        kern,
        grid_spec=grid_spec,
        out_shape=jax.ShapeDtypeStruct((b, l, BRANCH_W), BF16),
        compiler_params=_cparams(("parallel", "arbitrary")),
        name="window_mixer",
    )(sink, p_rope, p_rope, p_plain)


def _gla_kernel(q_ref, k_ref, v_ref, g_ref, wg_ref, bg_ref, gn_ref, o_ref, acc_s, st_s, *, total, n_ctx):
    ck = GLA_CHUNK
    n_chunks = total // ck
    ctx_chunks = n_ctx // ck
    hw = GLA_HEADS * GLA_DK

    acc_s[...] = jnp.zeros_like(acc_s)
    st_s[...] = jnp.zeros_like(st_s)

    rr = lax.broadcasted_iota(jnp.int32, (ck, ck), 0)
    cc = lax.broadcasted_iota(jnp.int32, (ck, ck), 1)
    tri = (rr >= cc, rr <= cc)
    tri_b = tuple(jnp.where(t, 1.0, 0.0).astype(BF16) for t in tri)
    tri2 = tuple(jnp.concatenate([t, t], axis=0) for t in tri)
    lane = lax.broadcasted_iota(jnp.int32, (LANES, LANES), 1)

    def body(i, _):
        cf = i
        cb = jnp.where(i < ctx_chunks, ctx_chunks - 1 - i, n_chunks - 1 - (i - ctx_chunks))
        for d, c in ((0, cf), (1, cb)):
            r0 = pl.multiple_of(c * ck, ck)
            y = _dot(g_ref[0, pl.ds(r0, ck), :], wg_ref[:, d * hw:(d + 1) * hw]) + bg_ref[:, d * hw:(d + 1) * hw]
            la = (jnp.minimum(y, 0.0) - jnp.log(1.0 + jnp.exp(-jnp.abs(y)))) * (1.0 / GLA_TAU)
            hi = la.astype(BF16)
            lo = (la - hi.astype(F32)).astype(BF16)
            cum = _dot(tri_b[d], hi) + _dot(tri_b[d], lo)
            tot = cum[ck - 1:ck, :] if d == 0 else cum[0:1, :]
            qf = q_ref[0, pl.ds(r0, ck), :].astype(F32)
            kf = k_ref[0, pl.ds(r0, ck), :].astype(F32)
            qt = (qf * jnp.exp(cum)).astype(BF16)
            kt = (kf * jnp.exp(-cum)).astype(BF16)
            kw = (kf * jnp.exp(tot - cum)).astype(BF16)
            dec = jnp.exp(tot)
            for p in range(2):
                sl = slice(p * LANES, (p + 1) * LANES)
                qs = _stack_halves(qt[:, sl])
                a2 = jnp.where(tri2[d], _dot_nt(qs, kt[:, sl]), 0.0).astype(BF16)
                st = st_s[2 * d + p]
                inter = _dot_nt(qs, st.astype(BF16))
                upd = []
                for hh in range(2):
                    vs = slice((2 * p + hh) * GLA_DV, (2 * p + hh + 1) * GLA_DV)
                    vh = v_ref[0, pl.ds(r0, ck), vs]
                    o = _dot(a2[hh * ck:(hh + 1) * ck], vh) + inter[hh * ck:(hh + 1) * ck]
                    acc_s[pl.ds(r0, ck), vs] += o
                    upd.append(_dot_tn(vh, kw[:, sl]))
                st_s[2 * d + p] = st * dec[:, sl] + jnp.where(lane < GLA_DK, upd[0], upd[1])
        return 0

    lax.fori_loop(0, n_chunks, body, 0)

    def finish(i, _):
        r0 = pl.multiple_of(i * TQ, TQ)
        for h in range(GLA_HEADS):
            vs = slice(h * GLA_DV, (h + 1) * GLA_DV)
            o = acc_s[pl.ds(r0, TQ), vs]
            y = o * lax.rsqrt(jnp.mean(o * o, axis=-1, keepdims=True) + EPS) * gn_ref[:, vs]
            o_ref[0, pl.ds(r0, TQ), vs] = y.astype(BF16)
        return 0

    lax.fori_loop(0, total // TQ, finish, 0)


def _gla_mixer(p_plain, wg, bg, gn, n_ctx):
    b, l, _ = p_plain.shape
    kern = functools.partial(_gla_kernel, total=l, n_ctx=n_ctx)
    const = lambda i: (0, 0)
    return pl.pallas_call(
        kern,
        grid=(b,),
        in_specs=[pl.BlockSpec((1, l, 256), lambda i: (i, 0, PL_AQ)),
                  pl.BlockSpec((1, l, 256), lambda i: (i, 0, PL_AK)),
                  pl.BlockSpec((1, l, 512), lambda i: (i, 0, PL_AV)),
                  pl.BlockSpec((1, l, LANES), lambda i: (i, 0, PL_AG)),
                  pl.BlockSpec((LANES, 512), const),
                  pl.BlockSpec((1, 512), const),
                  pl.BlockSpec((1, 512), const)],
        out_specs=pl.BlockSpec((1, l, BRANCH_W), lambda i: (i, 0, 0)),
        out_shape=jax.ShapeDtypeStruct((b, l, BRANCH_W), BF16),
        scratch_shapes=[pltpu.VMEM((l, BRANCH_W), F32), pltpu.VMEM((4, GLA_DV, LANES), F32)],
        compiler_params=_cparams(("parallel",)),
        name="gla_mixer",
    )(p_plain, p_plain, p_plain, p_plain, wg, bg, gn)


def _merge_kernel(h_ref, oa_ref, ob_ref, oc_ref, od_ref, za_ref, zb_ref, zc_ref, zd_ref, x_ref, mod_ref,
                  wm_ref, wup_ref, wout_ref, lng_ref, lnb_ref, out_ref, *, alpha):
    h = h_ref[0]
    acc = None
    for i, (o_ref, z_ref) in enumerate(((oa_ref, za_ref), (ob_ref, zb_ref), (oc_ref, zc_ref), (od_ref, zd_ref))):
        z = z_ref[0].astype(F32)
        br = (o_ref[0].astype(F32) * (z * _sigmoid(z))).astype(BF16)
        term = _sigmoid(_dot(h, wm_ref[i])) * _dot(br, wup_ref[i])
        acc = term if acc is None else acc + term
    y = _dot(acc.astype(BF16), wout_ref[...])
    r = alpha * x_ref[0] + mod_ref[0, 0, 2:3, :] * y
    mu = jnp.mean(r, axis=-1, keepdims=True)
    rc = r - mu
    var = jnp.mean(rc * rc, axis=-1, keepdims=True)
    out_ref[0] = rc * lax.rsqrt(var + EPS) * lng_ref[...] + lnb_ref[...]


def _merge(h, outs, p_plain, x_all, mod, wm, wup, wout, ln_g, ln_b, skip_ctx):
    b, l, d = h.shape
    off = 1 if skip_ctx else 0
    nt = l // TQ - off
    row = lambda i, j: (i, j + off, 0)
    zspec = lambda blk: pl.BlockSpec((1, TQ, BRANCH_W), lambda i, j: (i, j + off, blk))
    c2 = lambda i, j: (0, 0)
    c3 = lambda i, j: (0, 0, 0)
    in_specs = ([pl.BlockSpec((1, TQ, d), row)]
                + [pl.BlockSpec((1, TQ, BRANCH_W), row)] * 4
                + [zspec(PL_AZ), zspec(PL_BZ), zspec(PL_CZ), zspec(PL_DZ)]
                + [pl.BlockSpec((1, TQ, d), row),
                   pl.BlockSpec((1, 1, 3, d), lambda i, j: (i, jnp.minimum(j + off, 1), 0, 0)),
                   pl.BlockSpec((4, d, d), c3), pl.BlockSpec((4, BRANCH_W, d), c3), pl.BlockSpec((d, d), c2),
                   pl.BlockSpec((1, d), c2), pl.BlockSpec((1, d), c2)])
    return pl.pallas_call(
        functools.partial(_merge_kernel, alpha=(2 * DEPTH) ** 0.25),
        grid=(b, nt),
        in_specs=in_specs,
        out_specs=pl.BlockSpec((1, TQ, d), lambda i, j: (i, j, 0)),
        out_shape=jax.ShapeDtypeStruct((b, nt * TQ, d), F32),
        compiler_params=_cparams(("parallel", "parallel")),
        name="merge",
    )(h, *outs, p_plain, p_plain, p_plain, p_plain, x_all, mod, wm, wup, wout,
      ln_g.reshape(1, d), ln_b.reshape(1, d))


def _gather_cols(w, idx, scale=None):
    cols = jnp.take(w, jnp.asarray(np.maximum(idx, 0)), axis=1)
    cols = jnp.where(jnp.asarray(idx >= 0)[None, :], cols, 0.0)
    if scale is not None:
        cols = cols * jnp.asarray(scale)[None, :]
    return cols.astype(BF16)


def _rope_tables(n_ctx, n_lat):
    t = np.arange(n_lat)
    freqs = ROPE_BASE ** (-np.arange(ROPE_HALF, dtype=np.float32) / ROPE_HALF)
    pos = np.stack([(t // GRID_W).astype(np.float32), (t % GRID_W).astype(np.float32)], axis=1)
    ang = jnp.asarray(pos[:, :, None] * freqs[None, None, :], F32)
    cos, sin = jnp.cos(ang), jnp.sin(ang)
    cos_h = jnp.concatenate([cos, cos], axis=-1).reshape(n_lat, HEAD_DIM)
    sin_h = jnp.concatenate([-sin, sin], axis=-1).reshape(n_lat, HEAD_DIM)
    cos_t = jnp.concatenate([jnp.ones((n_ctx, HEAD_DIM), F32), cos_h], axis=0)
    sin_t = jnp.concatenate([jnp.zeros((n_ctx, HEAD_DIM), F32), sin_h], axis=0)
    return jnp.tile(cos_t, (1, 2)), jnp.tile(sin_t, (1, 2))


def kernel(x, c, ctx, c_ctx, w_ada, b_ada, w_in, gla_w_gate, gla_b_gate, gla_norm, win_sink, glb_q_norm,
           glb_k_norm, diff_lambda, diff_norm, w_merge, w_up, w_out, ln_g, ln_b):
    b, n_lat, d = x.shape
    n_ctx = ctx.shape[1]
    assert n_ctx == TQ and n_lat % TK == 0 and d == 1024
    x_all = jnp.concatenate([ctx, x], axis=1)
    cs = jnp.zeros((16, d), F32).at[0:b].set(c).at[b].set(c_ctx)
    cos_t, sin_t = _rope_tables(n_ctx, n_lat)
    seg = jnp.asarray(np.kron(np.eye(2, dtype=np.float32), np.ones((HEAD_DIM, HEAD_DIM), np.float32)), BF16)

    for layer in range(DEPTH):
        last = layer == DEPTH - 1
        lam_init = 0.8 - 0.6 * math.exp(-0.3 * layer)
        w_l = w_in[layer]
        w_plain = _gather_cols(w_l, _PLAIN_IDX, _PLAIN_SCALE)
        w_rope = _gather_cols(w_l, _ROPE_IDX, _ROPE_SCALE)
        w_nr = _gather_cols(w_l, _NR_IDX)
        gain_nr = jnp.concatenate([jnp.tile(glb_q_norm[layer] * (HEAD_DIM ** -0.5 * LOG2E), GLB_HEADS),
                                   jnp.tile(glb_k_norm[layer], GLB_KV)]).reshape(1, NR_W)
        wg = jnp.zeros((LANES, 2 * GLA_HEADS * GLA_DK), F32)
        wg = wg.at[0:GLA_RANK, 0:256].set(gla_w_gate[layer, 0]).at[GLA_RANK:2 * GLA_RANK, 256:512].set(gla_w_gate[layer, 1])
        bg = gla_b_gate[layer].reshape(1, 512)
        wup = jnp.stack([w_up[layer, 0], w_up[layer, 1][_PERM_B], w_up[layer, 2][_PERM_C], w_up[layer, 3]]).astype(BF16)

        ada = _ada(cs, w_ada[layer], b_ada[layer])
        mod_x = ada[0:b].reshape(b, 1, 3, d)
        mod_c = jnp.broadcast_to(ada[b].reshape(1, 1, 3, d), (b, 1, 3, d))
        mod = jnp.concatenate([mod_c, mod_x], axis=1)

        h = _ln_mod(x_all, mod)
        p_plain = _inproj(h, w_plain, "plain")
        p_rope = _inproj(h, w_rope, "rope", cos_t, sin_t)
        p_nr = _inproj(h, w_nr, "nr", cos_t, sin_t, gain_nr, seg)

        o_a = _gla_mixer(p_plain, wg.astype(BF16), bg, gla_norm[layer].reshape(1, BRANCH_W), n_ctx)
        o_b = _window_mixer(p_rope, p_plain, win_sink[layer], n_ctx)
        o_c = _global_mixer(p_nr, p_plain, n_ctx)
        o_d = _diff_mixer(p_rope, p_plain, diff_lambda[layer], diff_norm[layer], lam_init, n_ctx)

        x_all = _merge(h, (o_a, o_b, o_c, o_d), p_plain, x_all, mod, w_merge[layer].astype(BF16), wup,
                       w_out[layer].astype(BF16), ln_g[layer], ln_b[layer], skip_ctx=last)
    return x_all
```

```python
import functools
import math

import numpy as np
import jax
import jax.numpy as jnp
from jax import lax
from jax.experimental import pallas as pl
from jax.experimental.pallas import tpu as pltpu

F32 = jnp.float32
BF16 = jnp.bfloat16

DEPTH = 2
GRID_W = 64
HEAD_DIM = 64
ROPE_HALF = HEAD_DIM // 4
ROPE_BASE = 10000.0
EPS = 1e-6
NEG_INF = -1e30
GLA_HEADS, GLA_DK, GLA_DV, GLA_RANK, GLA_TAU, GLA_CHUNK = 4, 64, 128, 16, 16.0, 64
WIN_HEADS, WIN_KV, WINDOW = 8, 2, 128
GLB_HEADS, GLB_KV = 8, 4
DIF_HEADS, DIF_KV, DIF_DV = 4, 2, 128
BRANCH_W = 512

LANES = 128
TQ = 256
TK = 512
VMEM_LIMIT = 56 << 20

_A_Q, _A_K, _A_V, _A_GF, _A_Z = 0, 256, 512, 1024, 1056
_B_Q, _B_K, _B_V, _B_Z = 1568, 2080, 2208, 2336
_C_Q, _C_K, _C_V, _C_Z = 2848, 3360, 3616, 3872
_D_Q, _D_K, _D_V, _D_Z = 4384, 4896, 5152, 5408

_PERM_B = np.array([(hk * 4 + g) * 64 + d for g in range(4) for hk in range(2) for d in range(64)], np.int32)
_PERM_C = np.array([((2 * kp + r) * 2 + g) * 64 + d
                    for kp in range(2) for g in range(2) for r in range(2) for d in range(64)], np.int32)

_ar = lambda o, n: np.arange(o, o + n, dtype=np.int32)
_PLAIN_IDX = np.concatenate([
    _ar(_A_V, 512), _ar(_A_Z, 512), _B_Z + _PERM_B, _C_Z + _PERM_C, _ar(_D_Z, 512),
    _ar(_A_Q, 256), _ar(_A_K, 256), _ar(_C_V, 256), _ar(_D_V, 256), _ar(_B_V, 128),
    _ar(_A_GF, 32), np.full(96, -1, np.int32)])
_PLAIN_SCALE = np.ones(_PLAIN_IDX.shape, np.float32)
_PLAIN_SCALE[2560:2816] = GLA_DK ** -0.5
_ROPE_IDX = np.concatenate([_B_Q + _PERM_B, _ar(_D_Q, 512), _ar(_D_K, 256), _ar(_B_K, 128)])
_ROPE_SCALE = np.ones(_ROPE_IDX.shape, np.float32)
_ROPE_SCALE[0:1024] = HEAD_DIM ** -0.5
_NR_IDX = np.concatenate([_C_Q + _PERM_C, _ar(_C_K, 256)])
PL_AV, PL_AZ, PL_BZ, PL_CZ, PL_DZ = 0, 1, 2, 3, 4
PL_AQ, PL_AK, PL_CV, PL_DV = 10, 11, 12, 13
PL_BV, PL_AG = 28, 29
PLAIN_W, ROPE_W, NR_W = 3840, 1408, 768
ROPE_QW = 1024
LOG2E = math.log2(math.e)


def _cparams(sem):
    return pltpu.CompilerParams(dimension_semantics=sem, vmem_limit_bytes=VMEM_LIMIT)


def _sigmoid(x):
    return 1.0 / (1.0 + jnp.exp(-x))


def _dot(a, b):
    return jnp.dot(a, b, preferred_element_type=F32)


def _dot_nt(a, b):
    return lax.dot_general(a, b, (((1,), (1,)), ((), ())), preferred_element_type=F32)


def _dot_tn(a, b):
    return lax.dot_general(a, b, (((0,), (0,)), ((), ())), preferred_element_type=F32)


def _ada_kernel(c_ref, w_ref, b_ref, o_ref):
    c = c_ref[...]
    s = c * _sigmoid(c)
    o_ref[...] = jnp.dot(s, w_ref[...], preferred_element_type=F32,
                         precision=lax.Precision.HIGHEST) + b_ref[...]


def _ada(cs, w_ada, b_ada):
    r, d = cs.shape
    n = w_ada.shape[1]
    bn = 1024
    return pl.pallas_call(
        _ada_kernel,
        grid=(n // bn,),
        in_specs=[pl.BlockSpec((r, d), lambda j: (0, 0)),
                  pl.BlockSpec((d, bn), lambda j: (0, j)),
                  pl.BlockSpec((1, bn), lambda j: (0, j))],
        out_specs=pl.BlockSpec((r, bn), lambda j: (0, j)),
        out_shape=jax.ShapeDtypeStruct((r, n), F32),
        compiler_params=_cparams(("arbitrary",)),
        name="ada",
    )(cs, w_ada, b_ada.reshape(1, n))


def _ln_kernel(x_ref, mod_ref, h_ref):
    x = x_ref[0]
    mu = jnp.mean(x, axis=-1, keepdims=True)
    xc = x - mu
    var = jnp.mean(xc * xc, axis=-1, keepdims=True)
    y = xc * lax.rsqrt(var + EPS)
    shift = mod_ref[0, 0, 0:1, :]
    scale = mod_ref[0, 0, 1:2, :]
    h_ref[0] = (y * (1.0 + scale) + shift).astype(BF16)


def _ln_mod(x_all, mod):
    b, l, d = x_all.shape
    return pl.pallas_call(
        _ln_kernel,
        grid=(b, l // TQ),
        in_specs=[pl.BlockSpec((1, TQ, d), lambda i, j: (i, j, 0)),
                  pl.BlockSpec((1, 1, 3, d), lambda i, j: (i, jnp.minimum(j, 1), 0, 0))],
        out_specs=pl.BlockSpec((1, TQ, d), lambda i, j: (i, j, 0)),
        out_shape=jax.ShapeDtypeStruct((b, l, d), BF16),
        compiler_params=_cparams(("parallel", "parallel")),
        name="ln_mod",
    )(x_all, mod)


def _rope(y, cos, sin, lo16):
    sw = jnp.where(lo16, pltpu.roll(y, LANES - ROPE_HALF, 1), pltpu.roll(y, ROPE_HALF, 1))
    return y * cos + sw * sin


def _inproj_plain_kernel(h_ref, w_ref, o_ref):
    o_ref[0] = _dot(h_ref[0], w_ref[...]).astype(BF16)


def _inproj_rope_kernel(h_ref, w_ref, cos_ref, sin_ref, o_ref, *, width, q_width):
    acc = _dot(h_ref[0], w_ref[...])
    cos, sin = cos_ref[...], sin_ref[...]
    lo16 = (lax.broadcasted_iota(jnp.int32, cos.shape, 1) % (2 * ROPE_HALF)) < ROPE_HALF
    for c in range(width // LANES):
        y = _rope(acc[:, c * LANES:(c + 1) * LANES], cos, sin, lo16)
        if c * LANES < q_width:
            y = y * LOG2E
        o_ref[0, :, c * LANES:(c + 1) * LANES] = y.astype(BF16)


def _inproj_nr_kernel(h_ref, w_ref, cos_ref, sin_ref, gain_ref, seg_ref, o_ref, *, width):
    acc = _dot(h_ref[0], w_ref[...])
    cos, sin = cos_ref[...], sin_ref[...]
    lo16 = (lax.broadcasted_iota(jnp.int32, cos.shape, 1) % (2 * ROPE_HALF)) < ROPE_HALF
    for c in range(width // LANES):
        y = acc[:, c * LANES:(c + 1) * LANES]
        ss = _dot((y * y).astype(BF16), seg_ref[...])
        y = y * lax.rsqrt(ss * (1.0 / HEAD_DIM) + EPS) * gain_ref[:, c * LANES:(c + 1) * LANES]
        o_ref[0, :, c * LANES:(c + 1) * LANES] = _rope(y, cos, sin, lo16).astype(BF16)


def _inproj(h, w, mode, cos=None, sin=None, gain=None, seg=None):
    b, l, d = h.shape
    n = w.shape[1]
    bm = l // 4
    if mode == "plain":
        bn = 768
        return pl.pallas_call(
            _inproj_plain_kernel,
            grid=(b, l // bm, n // bn),
            in_specs=[pl.BlockSpec((1, bm, d), lambda i, r, j: (i, r, 0)),
                      pl.BlockSpec((d, bn), lambda i, r, j: (0, j))],
            out_specs=pl.BlockSpec((1, bm, bn), lambda i, r, j: (i, r, j)),
            out_shape=jax.ShapeDtypeStruct((b, l, n), BF16),
            compiler_params=_cparams(("parallel", "parallel", "arbitrary")),
            name="inproj_plain",
        )(h, w)
    tab = pl.BlockSpec((bm, LANES), lambda i, r: (r, 0))
    in_specs = [pl.BlockSpec((1, bm, d), lambda i, r: (i, r, 0)),
                pl.BlockSpec((d, n), lambda i, r: (0, 0)), tab, tab]
    args = [h, w, cos, sin]
    if mode == "rope":
        body = functools.partial(_inproj_rope_kernel, width=n, q_width=ROPE_QW)
    else:
        body = functools.partial(_inproj_nr_kernel, width=n)
        in_specs += [pl.BlockSpec((1, n), lambda i, r: (0, 0)),
                     pl.BlockSpec((LANES, LANES), lambda i, r: (0, 0))]
        args += [gain, seg]
    return pl.pallas_call(
        body,
        grid=(b, l // bm),
        in_specs=in_specs,
        out_specs=pl.BlockSpec((1, bm, n), lambda i, r: (i, r, 0)),
        out_shape=jax.ShapeDtypeStruct((b, l, n), BF16),
        compiler_params=_cparams(("parallel", "parallel")),
        name="inproj_" + mode,
    )(*args)


def _stack_halves(q):
    lane = lax.broadcasted_iota(jnp.int32, q.shape, 1)
    zero = jnp.zeros_like(q)
    return jnp.concatenate([jnp.where(lane < HEAD_DIM, q, zero), jnp.where(lane >= HEAD_DIM, q, zero)], axis=0)


FLASH_ROWS = 2 * TQ


FLASH_REFS = 6


def _flash_scratch(n_streams):
    stat = pltpu.VMEM((FLASH_ROWS, LANES), F32)
    one = [pltpu.VMEM((FLASH_ROWS, LANES), BF16), stat, stat, stat,
           pltpu.VMEM((FLASH_ROWS, TK), F32), pltpu.VMEM((FLASH_ROWS, TK), BF16)]
    return one * n_streams


def _flash(q_blocks, kv_blocks, k_ref, v_ref, n_ctx, n_lat_chunks, with_latent, scratch):
    ns = len(q_blocks)
    streams = [scratch[FLASH_REFS * i:FLASH_REFS * (i + 1)] for i in range(ns)]
    kv_lanes = [slice(kb * LANES, (kb + 1) * LANES) for kb in kv_blocks]
    for q, (qs_s, m_s, l_s, acc_s, _, _) in zip(q_blocks, streams):
        qs_s[...] = _stack_halves(q)
        m_s[...] = jnp.full((FLASH_ROWS, LANES), NEG_INF, F32)
        l_s[...] = jnp.zeros((FLASH_ROWS, LANES), F32)
        acc_s[...] = jnp.zeros((FLASH_ROWS, LANES), F32)

    def scores(i, r0, tk):
        qs_s, _, _, _, s_s, _ = streams[i]
        s_s[:, 0:tk] = _dot_nt(qs_s[...], k_ref[0, pl.ds(r0, tk), kv_lanes[i]])

    def accumulate(i, r0, tk):
        _, m_s, l_s, acc_s, s_s, p_s = streams[i]
        s = s_s[:, 0:tk]
        m_old = m_s[...]
        m_new = jnp.maximum(m_old, jnp.max(s, axis=1, keepdims=True))
        alpha = jnp.exp2(m_old - m_new)
        p = jnp.exp2(s - jnp.concatenate([m_new] * (tk // LANES), axis=1))
        p_s[:, 0:tk] = p.astype(BF16)
        psum = p[:, 0:LANES]
        for t in range(1, tk // LANES):
            psum = psum + p[:, t * LANES:(t + 1) * LANES]
        l_s[...] = alpha * l_s[...] + psum
        m_s[...] = m_new
        acc_s[...] = alpha * acc_s[...] + _dot(p_s[:, 0:tk], v_ref[0, pl.ds(r0, tk), kv_lanes[i]])

    def stage(r0, tk, has_next):
        for i in range(ns):
            accumulate(i, r0, tk)
            if has_next:
                scores(i, r0 + tk, TK)

    for i in range(ns):
        scores(i, 0, n_ctx)
    stage(0, n_ctx, True)

    def body(c, _):
        stage(pl.multiple_of(n_ctx + c * TK, LANES), TK, True)
        return 0

    @pl.when(with_latent)
    def _():
        lax.fori_loop(0, n_lat_chunks - 1, body, 0)
        stage(n_ctx + (n_lat_chunks - 1) * TK, TK, False)

    return [acc_s[...] / jnp.sum(l_s[...], axis=1, keepdims=True) for _, _, l_s, acc_s, _, _ in streams]


def _global_kernel(q_ref, k_ref, v_ref, o_ref, *scratch, n_ctx, n_lat):
    lane = lax.broadcasted_iota(jnp.int32, (TQ, LANES), 1)
    blocks = range(GLB_HEADS // 2)
    outs = _flash([q_ref[0, :, c * LANES:(c + 1) * LANES] for c in blocks], [c // 2 for c in blocks],
                  k_ref, v_ref, n_ctx, n_lat // TK, pl.program_id(1) > 0, scratch)
    for c, o in enumerate(outs):
        o_ref[0, :, c * LANES:(c + 1) * LANES] = jnp.where(lane < HEAD_DIM, o[0:TQ], o[TQ:2 * TQ]).astype(BF16)


def _global_mixer(p_nr, p_plain, n_ctx):
    b, l, _ = p_nr.shape
    kern = functools.partial(_global_kernel, n_ctx=n_ctx, n_lat=l - n_ctx)
    return pl.pallas_call(
        kern,
        grid=(b, l // TQ),
        in_specs=[pl.BlockSpec((1, TQ, 512), lambda i, j: (i, j, 0)),
                  pl.BlockSpec((1, l, 256), lambda i, j: (i, 0, 2)),
                  pl.BlockSpec((1, l, 256), lambda i, j: (i, 0, PL_CV))],
        out_specs=pl.BlockSpec((1, TQ, 512), lambda i, j: (i, j, 0)),
        out_shape=jax.ShapeDtypeStruct((b, l, BRANCH_W), BF16),
        scratch_shapes=_flash_scratch(GLB_HEADS // 2),
        compiler_params=_cparams(("parallel", "arbitrary")),
        name="global_mixer",
    )(p_nr, p_nr, p_plain)


def _diff_kernel(q_ref, k_ref, v_ref, lam_ref, g_ref, o_ref, *scratch, n_ctx, n_lat, lam_init):
    lp = lam_ref[...]
    lam = (jnp.exp(jnp.sum(lp[0:1] * lp[1:2], axis=1, keepdims=True))
           - jnp.exp(jnp.sum(lp[2:3] * lp[3:4], axis=1, keepdims=True)) + lam_init)
    blocks = range(DIF_HEADS)
    outs = _flash([q_ref[0, :, c * LANES:(c + 1) * LANES] for c in blocks], [c // 2 for c in blocks],
                  k_ref, v_ref, n_ctx, n_lat // TK, pl.program_id(1) > 0, scratch)
    for c, o in enumerate(outs):
        o = o[0:TQ] - lam * o[TQ:2 * TQ]
        y = o * lax.rsqrt(jnp.mean(o * o, axis=-1, keepdims=True) + EPS) * g_ref[...]
        o_ref[0, :, c * LANES:(c + 1) * LANES] = (y * (1.0 - lam_init)).astype(BF16)


def _diff_mixer(p_rope, p_plain, lam_p, sub_g, lam_init, n_ctx):
    b, l, _ = p_rope.shape
    kern = functools.partial(_diff_kernel, n_ctx=n_ctx, n_lat=l - n_ctx, lam_init=lam_init)
    return pl.pallas_call(
        kern,
        grid=(b, l // TQ),
        in_specs=[pl.BlockSpec((1, TQ, 512), lambda i, j: (i, j, 1)),
                  pl.BlockSpec((1, l, 256), lambda i, j: (i, 0, 4)),
                  pl.BlockSpec((1, l, 256), lambda i, j: (i, 0, PL_DV)),
                  pl.BlockSpec((4, HEAD_DIM), lambda i, j: (0, 0)),
                  pl.BlockSpec((1, DIF_DV), lambda i, j: (0, 0))],
        out_specs=pl.BlockSpec((1, TQ, 512), lambda i, j: (i, j, 0)),
        out_shape=jax.ShapeDtypeStruct((b, l, BRANCH_W), BF16),
        scratch_shapes=_flash_scratch(DIF_HEADS),
        compiler_params=_cparams(("parallel", "arbitrary")),
        name="diff_mixer",
    )(p_rope, p_rope, p_plain, lam_p, sub_g.reshape(1, DIF_DV))


def _window_kernel(sink_ref, q_ref, k_ref, v_ref, o_ref, *, n_ctx, total):
    j = pl.program_id(1)
    span = TQ + 2 * WINDOW
    start = pl.multiple_of(jnp.clip(j * TQ - WINDOW, 0, total - span), LANES)
    kc, vc = k_ref[0, 0:n_ctx, :], v_ref[0, 0:n_ctx, :]
    kw, vw = k_ref[0, pl.ds(start, span), :], v_ref[0, pl.ds(start, span), :]
    row = lax.broadcasted_iota(jnp.int32, (2 * TQ, span), 0)
    qpos = j * TQ + jnp.where(row >= TQ, row - TQ, row)
    kpos = start + lax.broadcasted_iota(jnp.int32, (2 * TQ, span), 1)
    valid = (kpos >= n_ctx) & (qpos >= n_ctx) & (jnp.abs(kpos - qpos) <= WINDOW)
    srow = lax.broadcasted_iota(jnp.int32, (2 * TQ, 1), 0)
    lane = lax.broadcasted_iota(jnp.int32, (TQ, LANES), 1)
    for g in range(WIN_HEADS // WIN_KV):
        qs = _stack_halves(q_ref[0, :, g * LANES:(g + 1) * LANES])
        sink = jnp.where(srow < TQ, sink_ref[g], sink_ref[WIN_HEADS // WIN_KV + g]) * LOG2E
        s_c = _dot_nt(qs, kc)
        s_w = jnp.where(valid, _dot_nt(qs, kw), NEG_INF)
        m = jnp.maximum(jnp.maximum(jnp.max(s_c, axis=1, keepdims=True), jnp.max(s_w, axis=1, keepdims=True)), sink)
        p_c = jnp.exp2(s_c - m)
        p_w = jnp.exp2(s_w - m)
        den = jnp.sum(p_c, axis=1, keepdims=True) + jnp.sum(p_w, axis=1, keepdims=True) + jnp.exp2(sink - m)
        o = (_dot(p_c.astype(BF16), vc) + _dot(p_w.astype(BF16), vw)) / den
        o_ref[0, :, g * LANES:(g + 1) * LANES] = jnp.where(lane < HEAD_DIM, o[0:TQ], o[TQ:2 * TQ]).astype(BF16)


def _window_mixer(p_rope, p_plain, sink, n_ctx):
    b, l, _ = p_rope.shape
    kern = functools.partial(_window_kernel, n_ctx=n_ctx, total=l)
    grid_spec = pltpu.PrefetchScalarGridSpec(
        num_scalar_prefetch=1,
        grid=(b, l // TQ),
        in_specs=[pl.BlockSpec((1, TQ, 512), lambda i, j, s: (i, j, 0)),
                  pl.BlockSpec((1, l, LANES), lambda i, j, s: (i, 0, 10)),
                  pl.BlockSpec((1, l, LANES), lambda i, j, s: (i, 0, PL_BV))],
        out_specs=pl.BlockSpec((1, TQ, 512), lambda i, j, s: (i, j, 0)),
    )
    return pl.pallas_call(
        kern,
        grid_spec=grid_spec,
        out_shape=jax.ShapeDtypeStruct((b, l, BRANCH_W), BF16),
        compiler_params=_cparams(("parallel", "arbitrary")),
        name="window_mixer",
    )(sink, p_rope, p_rope, p_plain)


GLA_GROUP = 4


def _gla_kernel(q_ref, k_ref, v_ref, g_ref, wg_ref, bg_ref, gn_ref, o_ref, accf_s, accb_s, st_s, *, total, n_ctx):
    ck = GLA_CHUNK
    n_groups = total // (ck * GLA_GROUP)
    ctx_groups = n_ctx // (ck * GLA_GROUP)
    hw = GLA_HEADS * GLA_DK
    acc_refs = (accf_s, accb_s)

    st_s[...] = jnp.zeros_like(st_s)

    rr = lax.broadcasted_iota(jnp.int32, (ck, ck), 0)
    cc = lax.broadcasted_iota(jnp.int32, (ck, ck), 1)
    tri = (rr >= cc, rr <= cc)
    tri_b = tuple(jnp.where(t, 1.0, 0.0).astype(BF16) for t in tri)
    tri2 = tuple(jnp.concatenate([t, t], axis=0) for t in tri)
    lane = lax.broadcasted_iota(jnp.int32, (LANES, LANES), 1)

    def body(i, _):
        gf = i
        gb = jnp.where(i < ctx_groups, ctx_groups - 1 - i, n_groups - 1 - (i - ctx_groups))
        chains = []
        for u in range(GLA_GROUP):
            for d, grp in ((0, gf), (1, gb)):
                c = grp * GLA_GROUP + (u if d == 0 else GLA_GROUP - 1 - u)
                chains.append((d, pl.multiple_of(c * ck, ck)))

        gate = [_dot(g_ref[0, pl.ds(r0, ck), :], wg_ref[:, d * hw:(d + 1) * hw]) + bg_ref[:, d * hw:(d + 1) * hw]
                for d, r0 in chains]
        cums = []
        for (d, r0), y in zip(chains, gate):
            la = (jnp.minimum(y, 0.0) - jnp.log(1.0 + jnp.exp(-jnp.abs(y)))) * (1.0 / GLA_TAU)
            hi = la.astype(BF16)
            lo = (la - hi.astype(F32)).astype(BF16)
            cums.append(_dot(tri_b[d], hi) + _dot(tri_b[d], lo))

        prepped = []
        for (d, r0), cum in zip(chains, cums):
            tot = cum[ck - 1:ck, :] if d == 0 else cum[0:1, :]
            qf = q_ref[0, pl.ds(r0, ck), :].astype(F32)
            kf = k_ref[0, pl.ds(r0, ck), :].astype(F32)
            qt = (qf * jnp.exp(cum)).astype(BF16)
            kt = (kf * jnp.exp(-cum)).astype(BF16)
            kw = (kf * jnp.exp(tot - cum)).astype(BF16)
            qs, a2 = [], []
            for p in range(2):
                sl = slice(p * LANES, (p + 1) * LANES)
                qs.append(_stack_halves(qt[:, sl]))
                a2.append(jnp.where(tri2[d], _dot_nt(qs[p], kt[:, sl]), 0.0).astype(BF16))
            prepped.append((qs, a2, kw, jnp.exp(tot)))

        intra, update = [], []
        for (d, r0), (qs, a2, kw, dec) in zip(chains, prepped):
            av, upd = [], []
            for p in range(2):
                sl = slice(p * LANES, (p + 1) * LANES)
                halves = []
                for hh in range(2):
                    vh = v_ref[0, pl.ds(r0, ck), (2 * p + hh) * GLA_DV:(2 * p + hh + 1) * GLA_DV]
                    av.append(_dot(a2[p][hh * ck:(hh + 1) * ck], vh))
                    halves.append(_dot_tn(vh, kw[:, sl]))
                upd.append(jnp.where(lane < GLA_DK, halves[0], halves[1]))
            intra.append(av)
            update.append(upd)

        states = [[st_s[2 * d + p] for p in range(2)] for d in range(2)]
        for (d, r0), (qs, a2, kw, dec), av, upd in zip(chains, prepped, intra, update):
            for p in range(2):
                inter = _dot_nt(qs[p], states[d][p].astype(BF16))
                for hh in range(2):
                    vs = slice((2 * p + hh) * GLA_DV, (2 * p + hh + 1) * GLA_DV)
                    acc_refs[d][pl.ds(r0, ck), vs] = av[2 * p + hh] + inter[hh * ck:(hh + 1) * ck]
                states[d][p] = states[d][p] * dec[:, p * LANES:(p + 1) * LANES] + upd[p]
        for d in range(2):
            for p in range(2):
                st_s[2 * d + p] = states[d][p]
        return 0

    lax.fori_loop(0, n_groups, body, 0)

    def finish(i, _):
        r0 = pl.multiple_of(i * TQ, TQ)
        for h in range(GLA_HEADS):
            vs = slice(h * GLA_DV, (h + 1) * GLA_DV)
            o = accf_s[pl.ds(r0, TQ), vs] + accb_s[pl.ds(r0, TQ), vs]
            y = o * lax.rsqrt(jnp.mean(o * o, axis=-1, keepdims=True) + EPS) * gn_ref[:, vs]
            o_ref[0, pl.ds(r0, TQ), vs] = y.astype(BF16)
        return 0

    lax.fori_loop(0, total // TQ, finish, 0)


def _gla_mixer(p_plain, wg, bg, gn, n_ctx):
    b, l, _ = p_plain.shape
    kern = functools.partial(_gla_kernel, total=l, n_ctx=n_ctx)
    const = lambda i: (0, 0)
    return pl.pallas_call(
        kern,
        grid=(b,),
        in_specs=[pl.BlockSpec((1, l, 256), lambda i: (i, 0, PL_AQ)),
                  pl.BlockSpec((1, l, 256), lambda i: (i, 0, PL_AK)),
                  pl.BlockSpec((1, l, 512), lambda i: (i, 0, PL_AV)),
                  pl.BlockSpec((1, l, LANES), lambda i: (i, 0, PL_AG)),
                  pl.BlockSpec((LANES, 512), const),
                  pl.BlockSpec((1, 512), const),
                  pl.BlockSpec((1, 512), const)],
        out_specs=pl.BlockSpec((1, l, BRANCH_W), lambda i: (i, 0, 0)),
        out_shape=jax.ShapeDtypeStruct((b, l, BRANCH_W), BF16),
        scratch_shapes=[pltpu.VMEM((l, BRANCH_W), F32), pltpu.VMEM((l, BRANCH_W), F32),
                        pltpu.VMEM((4, GLA_DV, LANES), F32)],
        compiler_params=_cparams(("parallel",)),
        name="gla_mixer",
    )(p_plain, p_plain, p_plain, p_plain, wg, bg, gn)


def _merge_kernel(h_ref, oa_ref, ob_ref, oc_ref, od_ref, za_ref, zb_ref, zc_ref, zd_ref, x_ref, mod_ref,
                  wm_ref, wup_ref, wout_ref, lng_ref, lnb_ref, out_ref, *, alpha):
    h = h_ref[0]
    acc = None
    for i, (o_ref, z_ref) in enumerate(((oa_ref, za_ref), (ob_ref, zb_ref), (oc_ref, zc_ref), (od_ref, zd_ref))):
        z = z_ref[0].astype(F32)
        br = (o_ref[0].astype(F32) * (z * _sigmoid(z))).astype(BF16)
        term = _sigmoid(_dot(h, wm_ref[i])) * _dot(br, wup_ref[i])
        acc = term if acc is None else acc + term
    y = _dot(acc.astype(BF16), wout_ref[...])
    r = alpha * x_ref[0] + mod_ref[0, 0, 2:3, :] * y
    mu = jnp.mean(r, axis=-1, keepdims=True)
    rc = r - mu
    var = jnp.mean(rc * rc, axis=-1, keepdims=True)
    out_ref[0] = rc * lax.rsqrt(var + EPS) * lng_ref[...] + lnb_ref[...]


def _merge(h, outs, p_plain, x_all, mod, wm, wup, wout, ln_g, ln_b, skip_ctx):
    b, l, d = h.shape
    off = 1 if skip_ctx else 0
    nt = l // TQ - off
    row = lambda i, j: (i, j + off, 0)
    zspec = lambda blk: pl.BlockSpec((1, TQ, BRANCH_W), lambda i, j: (i, j + off, blk))
    c2 = lambda i, j: (0, 0)
    c3 = lambda i, j: (0, 0, 0)
    in_specs = ([pl.BlockSpec((1, TQ, d), row)]
                + [pl.BlockSpec((1, TQ, BRANCH_W), row)] * 4
                + [zspec(PL_AZ), zspec(PL_BZ), zspec(PL_CZ), zspec(PL_DZ)]
                + [pl.BlockSpec((1, TQ, d), row),
                   pl.BlockSpec((1, 1, 3, d), lambda i, j: (i, jnp.minimum(j + off, 1), 0, 0)),
                   pl.BlockSpec((4, d, d), c3), pl.BlockSpec((4, BRANCH_W, d), c3), pl.BlockSpec((d, d), c2),
                   pl.BlockSpec((1, d), c2), pl.BlockSpec((1, d), c2)])
    return pl.pallas_call(
        functools.partial(_merge_kernel, alpha=(2 * DEPTH) ** 0.25),
        grid=(b, nt),
        in_specs=in_specs,
        out_specs=pl.BlockSpec((1, TQ, d), lambda i, j: (i, j, 0)),
        out_shape=jax.ShapeDtypeStruct((b, nt * TQ, d), F32),
        compiler_params=_cparams(("parallel", "parallel")),
        name="merge",
    )(h, *outs, p_plain, p_plain, p_plain, p_plain, x_all, mod, wm, wup, wout,
      ln_g.reshape(1, d), ln_b.reshape(1, d))


def _gather_cols(w, idx, scale=None):
    cols = jnp.take(w, jnp.asarray(np.maximum(idx, 0)), axis=1)
    cols = jnp.where(jnp.asarray(idx >= 0)[None, :], cols, 0.0)
    if scale is not None:
        cols = cols * jnp.asarray(scale)[None, :]
    return cols.astype(BF16)


def _rope_tables(n_ctx, n_lat):
    t = np.arange(n_lat)
    freqs = ROPE_BASE ** (-np.arange(ROPE_HALF, dtype=np.float32) / ROPE_HALF)
    pos = np.stack([(t // GRID_W).astype(np.float32), (t % GRID_W).astype(np.float32)], axis=1)
    ang = jnp.asarray(pos[:, :, None] * freqs[None, None, :], F32)
    cos, sin = jnp.cos(ang), jnp.sin(ang)
    cos_h = jnp.concatenate([cos, cos], axis=-1).reshape(n_lat, HEAD_DIM)
    sin_h = jnp.concatenate([-sin, sin], axis=-1).reshape(n_lat, HEAD_DIM)
    cos_t = jnp.concatenate([jnp.ones((n_ctx, HEAD_DIM), F32), cos_h], axis=0)
    sin_t = jnp.concatenate([jnp.zeros((n_ctx, HEAD_DIM), F32), sin_h], axis=0)
    return jnp.tile(cos_t, (1, 2)), jnp.tile(sin_t, (1, 2))


def kernel(x, c, ctx, c_ctx, w_ada, b_ada, w_in, gla_w_gate, gla_b_gate, gla_norm, win_sink, glb_q_norm,
           glb_k_norm, diff_lambda, diff_norm, w_merge, w_up, w_out, ln_g, ln_b):
    b, n_lat, d = x.shape
    n_ctx = ctx.shape[1]
    assert n_ctx == TQ and n_lat % TK == 0 and d == 1024
    x_all = jnp.concatenate([ctx, x], axis=1)
    cs = jnp.zeros((16, d), F32).at[0:b].set(c).at[b].set(c_ctx)
    cos_t, sin_t = _rope_tables(n_ctx, n_lat)
    seg = jnp.asarray(np.kron(np.eye(2, dtype=np.float32), np.ones((HEAD_DIM, HEAD_DIM), np.float32)), BF16)

    for layer in range(DEPTH):
        last = layer == DEPTH - 1
        lam_init = 0.8 - 0.6 * math.exp(-0.3 * layer)
        w_l = w_in[layer]
        w_plain = _gather_cols(w_l, _PLAIN_IDX, _PLAIN_SCALE)
        w_rope = _gather_cols(w_l, _ROPE_IDX, _ROPE_SCALE)
        w_nr = _gather_cols(w_l, _NR_IDX)
        gain_nr = jnp.concatenate([jnp.tile(glb_q_norm[layer] * (HEAD_DIM ** -0.5 * LOG2E), GLB_HEADS),
                                   jnp.tile(glb_k_norm[layer], GLB_KV)]).reshape(1, NR_W)
        wg = jnp.zeros((LANES, 2 * GLA_HEADS * GLA_DK), F32)
        wg = wg.at[0:GLA_RANK, 0:256].set(gla_w_gate[layer, 0]).at[GLA_RANK:2 * GLA_RANK, 256:512].set(gla_w_gate[layer, 1])
        bg = gla_b_gate[layer].reshape(1, 512)
        wup = jnp.stack([w_up[layer, 0], w_up[layer, 1][_PERM_B], w_up[layer, 2][_PERM_C], w_up[layer, 3]]).astype(BF16)

        ada = _ada(cs, w_ada[layer], b_ada[layer])
        mod_x = ada[0:b].reshape(b, 1, 3, d)
        mod_c = jnp.broadcast_to(ada[b].reshape(1, 1, 3, d), (b, 1, 3, d))
        mod = jnp.concatenate([mod_c, mod_x], axis=1)

        h = _ln_mod(x_all, mod)
        p_plain = _inproj(h, w_plain, "plain")
        p_rope = _inproj(h, w_rope, "rope", cos_t, sin_t)
        p_nr = _inproj(h, w_nr, "nr", cos_t, sin_t, gain_nr, seg)

        o_a = _gla_mixer(p_plain, wg.astype(BF16), bg, gla_norm[layer].reshape(1, BRANCH_W), n_ctx)
        o_b = _window_mixer(p_rope, p_plain, win_sink[layer], n_ctx)
        o_c = _global_mixer(p_nr, p_plain, n_ctx)
        o_d = _diff_mixer(p_rope, p_plain, diff_lambda[layer], diff_norm[layer], lam_init, n_ctx)

        x_all = _merge(h, (o_a, o_b, o_c, o_d), p_plain, x_all, mod, w_merge[layer].astype(BF16), wup,
                       w_out[layer].astype(BF16), ln_g[layer], ln_b[layer], skip_ctx=last)
    return x_all
```

```python
import functools
import math

import numpy as np
import jax
import jax.numpy as jnp
from jax import lax
from jax.experimental import pallas as pl
from jax.experimental.pallas import tpu as pltpu

F32 = jnp.float32
BF16 = jnp.bfloat16

DEPTH = 2
GRID_W = 64
HEAD_DIM = 64
ROPE_HALF = HEAD_DIM // 4
ROPE_BASE = 10000.0
EPS = 1e-6
NEG_INF = -1e30
GLA_HEADS, GLA_DK, GLA_DV, GLA_RANK, GLA_TAU, GLA_CHUNK = 4, 64, 128, 16, 16.0, 64
WIN_HEADS, WIN_KV, WINDOW = 8, 2, 128
GLB_HEADS, GLB_KV = 8, 4
DIF_HEADS, DIF_KV, DIF_DV = 4, 2, 128
BRANCH_W = 512

LANES = 128
TQ = 256
TK = 512
VMEM_LIMIT = 56 << 20

_A_Q, _A_K, _A_V, _A_GF, _A_Z = 0, 256, 512, 1024, 1056
_B_Q, _B_K, _B_V, _B_Z = 1568, 2080, 2208, 2336
_C_Q, _C_K, _C_V, _C_Z = 2848, 3360, 3616, 3872
_D_Q, _D_K, _D_V, _D_Z = 4384, 4896, 5152, 5408

PL_AV, PL_AZ, PL_BZ, PL_CZ, PL_DZ = 0, 1, 2, 3, 4
PL_AQ, PL_AK, PL_CV, PL_DV = 10, 11, 12, 13
PL_BV, PL_AG = 28, 29
PLAIN_W, ROPE_W, NR_W = 3840, 1408, 768
ROPE_QW = 1024
LOG2E = math.log2(math.e)


def _cparams(sem):
    return pltpu.CompilerParams(dimension_semantics=sem, vmem_limit_bytes=VMEM_LIMIT)


def _sigmoid(x):
    return 1.0 / (1.0 + jnp.exp(-x))


def _dot(a, b):
    return jnp.dot(a, b, preferred_element_type=F32)


def _dot_nt(a, b):
    return lax.dot_general(a, b, (((1,), (1,)), ((), ())), preferred_element_type=F32)


def _dot_tn(a, b):
    return lax.dot_general(a, b, (((0,), (0,)), ((), ())), preferred_element_type=F32)


def _ada_kernel(c_ref, w_ref, b_ref, o_ref):
    c = c_ref[...]
    s = c * _sigmoid(c)
    o_ref[...] = jnp.dot(s, w_ref[...], preferred_element_type=F32,
                         precision=lax.Precision.HIGHEST) + b_ref[...]


def _ada(cs, w_ada, b_ada):
    r, d = cs.shape
    n = w_ada.shape[1]
    bn = 1024
    return pl.pallas_call(
        _ada_kernel,
        grid=(n // bn,),
        in_specs=[pl.BlockSpec((r, d), lambda j: (0, 0)),
                  pl.BlockSpec((d, bn), lambda j: (0, j)),
                  pl.BlockSpec((1, bn), lambda j: (0, j))],
        out_specs=pl.BlockSpec((r, bn), lambda j: (0, j)),
        out_shape=jax.ShapeDtypeStruct((r, n), F32),
        compiler_params=_cparams(("arbitrary",)),
        name="ada",
    )(cs, w_ada, b_ada.reshape(1, n))


def _ln_kernel(x_ref, mod_ref, h_ref):
    x = x_ref[0]
    mu = jnp.mean(x, axis=-1, keepdims=True)
    xc = x - mu
    var = jnp.mean(xc * xc, axis=-1, keepdims=True)
    y = xc * lax.rsqrt(var + EPS)
    shift = mod_ref[0, 0, 0:1, :]
    scale = mod_ref[0, 0, 1:2, :]
    h_ref[0] = (y * (1.0 + scale) + shift).astype(BF16)


def _ln_mod(x_all, mod):
    b, l, d = x_all.shape
    return pl.pallas_call(
        _ln_kernel,
        grid=(b, l // TQ),
        in_specs=[pl.BlockSpec((1, TQ, d), lambda i, j: (i, j, 0)),
                  pl.BlockSpec((1, 1, 3, d), lambda i, j: (i, jnp.minimum(j, 1), 0, 0))],
        out_specs=pl.BlockSpec((1, TQ, d), lambda i, j: (i, j, 0)),
        out_shape=jax.ShapeDtypeStruct((b, l, d), BF16),
        compiler_params=_cparams(("parallel", "parallel")),
        name="ln_mod",
    )(x_all, mod)


def _rope(y, cos, sin, lo16):
    sw = jnp.where(lo16, pltpu.roll(y, LANES - ROPE_HALF, 1), pltpu.roll(y, ROPE_HALF, 1))
    return y * cos + sw * sin


def _inproj_plain_kernel(h_ref, w_ref, o_ref):
    o_ref[0] = _dot(h_ref[0], w_ref[...]).astype(BF16)


def _inproj_rope_kernel(h_ref, w_ref, cos_ref, sin_ref, o_ref, *, width, q_width):
    acc = _dot(h_ref[0], w_ref[...])
    cos, sin = cos_ref[...], sin_ref[...]
    lo16 = (lax.broadcasted_iota(jnp.int32, cos.shape, 1) % (2 * ROPE_HALF)) < ROPE_HALF
    for c in range(width // LANES):
        y = _rope(acc[:, c * LANES:(c + 1) * LANES], cos, sin, lo16)
        if c * LANES < q_width:
            y = y * LOG2E
        o_ref[0, :, c * LANES:(c + 1) * LANES] = y.astype(BF16)


def _inproj_nr_kernel(h_ref, w_ref, cos_ref, sin_ref, gain_ref, seg_ref, o_ref, *, width):
    acc = _dot(h_ref[0], w_ref[...])
    cos, sin = cos_ref[...], sin_ref[...]
    lo16 = (lax.broadcasted_iota(jnp.int32, cos.shape, 1) % (2 * ROPE_HALF)) < ROPE_HALF
    for c in range(width // LANES):
        y = acc[:, c * LANES:(c + 1) * LANES]
        ss = _dot((y * y).astype(BF16), seg_ref[...])
        y = y * lax.rsqrt(ss * (1.0 / HEAD_DIM) + EPS) * gain_ref[:, c * LANES:(c + 1) * LANES]
        o_ref[0, :, c * LANES:(c + 1) * LANES] = _rope(y, cos, sin, lo16).astype(BF16)


def _inproj(h, w, mode, cos=None, sin=None, gain=None, seg=None):
    b, l, d = h.shape
    n = w.shape[1]
    bm = l // 4
    if mode == "plain":
        bn = 768
        return pl.pallas_call(
            _inproj_plain_kernel,
            grid=(b, l // bm, n // bn),
            in_specs=[pl.BlockSpec((1, bm, d), lambda i, r, j: (i, r, 0)),
                      pl.BlockSpec((d, bn), lambda i, r, j: (0, j))],
            out_specs=pl.BlockSpec((1, bm, bn), lambda i, r, j: (i, r, j)),
            out_shape=jax.ShapeDtypeStruct((b, l, n), BF16),
            compiler_params=_cparams(("parallel", "parallel", "arbitrary")),
            name="inproj_plain",
        )(h, w)
    tab = pl.BlockSpec((bm, LANES), lambda i, r: (r, 0))
    in_specs = [pl.BlockSpec((1, bm, d), lambda i, r: (i, r, 0)),
                pl.BlockSpec((d, n), lambda i, r: (0, 0)), tab, tab]
    args = [h, w, cos, sin]
    if mode == "rope":
        body = functools.partial(_inproj_rope_kernel, width=n, q_width=ROPE_QW)
    else:
        body = functools.partial(_inproj_nr_kernel, width=n)
        in_specs += [pl.BlockSpec((1, n), lambda i, r: (0, 0)),
                     pl.BlockSpec((LANES, LANES), lambda i, r: (0, 0))]
        args += [gain, seg]
    return pl.pallas_call(
        body,
        grid=(b, l // bm),
        in_specs=in_specs,
        out_specs=pl.BlockSpec((1, bm, n), lambda i, r: (i, r, 0)),
        out_shape=jax.ShapeDtypeStruct((b, l, n), BF16),
        compiler_params=_cparams(("parallel", "parallel")),
        name="inproj_" + mode,
    )(*args)


def _stack_halves(q):
    lane = lax.broadcasted_iota(jnp.int32, q.shape, 1)
    zero = jnp.zeros_like(q)
    return jnp.concatenate([jnp.where(lane < HEAD_DIM, q, zero), jnp.where(lane >= HEAD_DIM, q, zero)], axis=0)


FLASH_ROWS = 2 * TQ


FLASH_REFS = 6


def _flash_scratch(n_streams):
    stat = pltpu.VMEM((FLASH_ROWS, LANES), F32)
    one = [pltpu.VMEM((FLASH_ROWS, LANES), BF16), stat, stat, stat,
           pltpu.VMEM((FLASH_ROWS, TK), F32), pltpu.VMEM((FLASH_ROWS, TK), BF16)]
    return one * n_streams


def _flash(q_blocks, kv_srcs, n_ctx, n_lat_chunks, with_latent, scratch):
    ns = len(q_blocks)
    streams = [scratch[FLASH_REFS * i:FLASH_REFS * (i + 1)] for i in range(ns)]
    kv_lanes = [slice(kb * LANES, (kb + 1) * LANES) for _, _, kb in kv_srcs]
    for q, (qs_s, _, _, _, _, _) in zip(q_blocks, streams):
        qs_s[...] = _stack_halves(q)

    def scores(i, r0, tk):
        qs_s, _, _, _, s_s, _ = streams[i]
        s_s[:, 0:tk] = _dot_nt(qs_s[...], kv_srcs[i][0][0, pl.ds(r0, tk), kv_lanes[i]])

    def accumulate(i, r0, tk, first=False):
        _, m_s, l_s, acc_s, s_s, p_s = streams[i]
        s = s_s[:, 0:tk]
        m_new = jnp.broadcast_to(jnp.max(s, axis=1, keepdims=True), (FLASH_ROWS, LANES))
        if not first:
            m_old = m_s[...]
            m_new = jnp.maximum(m_old, m_new)
            alpha = jnp.exp2(m_old - m_new)
        p = jnp.exp2(s - jnp.concatenate([m_new] * (tk // LANES), axis=1))
        p_s[:, 0:tk] = p.astype(BF16)
        psum = p[:, 0:LANES]
        for t in range(1, tk // LANES):
            psum = psum + p[:, t * LANES:(t + 1) * LANES]
        pv = _dot(p_s[:, 0:tk], kv_srcs[i][1][0, pl.ds(r0, tk), kv_lanes[i]])
        m_s[...] = m_new
        l_s[...] = psum if first else alpha * l_s[...] + psum
        acc_s[...] = pv if first else alpha * acc_s[...] + pv

    def stage(r0, tk, nxt, first=False):
        for i in range(ns):
            accumulate(i, r0, tk, first)
            if nxt is not None:
                scores(i, *nxt)

    for i in range(ns):
        scores(i, 0, n_ctx)
    stage(0, n_ctx, (n_ctx, TK), first=True)

    def body(c, _):
        r0 = pl.multiple_of(n_ctx + c * TK, LANES)
        stage(r0, TK, (r0 + TK, TK))
        return 0

    @pl.when(with_latent)
    def _():
        lax.fori_loop(0, n_lat_chunks - 1, body, 0)
        stage(n_ctx + (n_lat_chunks - 1) * TK, TK, None)

    return [acc_s[...] / jnp.sum(l_s[...], axis=1, keepdims=True) for _, _, l_s, acc_s, _, _ in streams]


N_GLB_BLOCKS = GLB_HEADS // 2
N_DIF_BLOCKS = DIF_HEADS


def _dense_kernel(qc_ref, kc_ref, vc_ref, qd_ref, kd_ref, vd_ref, lam_ref, g_ref, oc_ref, od_ref, *scratch,
                  n_ctx, n_lat, lam_init):
    q_blocks = ([qc_ref[0, :, c * LANES:(c + 1) * LANES] for c in range(N_GLB_BLOCKS)]
                + [qd_ref[0, :, c * LANES:(c + 1) * LANES] for c in range(N_DIF_BLOCKS)])
    kv_srcs = ([(kc_ref, vc_ref, c // 2) for c in range(N_GLB_BLOCKS)]
               + [(kd_ref, vd_ref, c // 2) for c in range(N_DIF_BLOCKS)])
    outs = _flash(q_blocks, kv_srcs, n_ctx, n_lat // TK, pl.program_id(1) > 0, scratch)

    lane = lax.broadcasted_iota(jnp.int32, (TQ, LANES), 1)
    for c, o in enumerate(outs[:N_GLB_BLOCKS]):
        oc_ref[0, :, c * LANES:(c + 1) * LANES] = jnp.where(lane < HEAD_DIM, o[0:TQ], o[TQ:2 * TQ]).astype(BF16)

    lp = lam_ref[...]
    lam = (jnp.exp(jnp.sum(lp[0:1] * lp[1:2], axis=1, keepdims=True))
           - jnp.exp(jnp.sum(lp[2:3] * lp[3:4], axis=1, keepdims=True)) + lam_init)
    for c, o in enumerate(outs[N_GLB_BLOCKS:]):
        o = o[0:TQ] - lam * o[TQ:2 * TQ]
        y = o * lax.rsqrt(jnp.mean(o * o, axis=-1, keepdims=True) + EPS) * g_ref[...]
        od_ref[0, :, c * LANES:(c + 1) * LANES] = (y * (1.0 - lam_init)).astype(BF16)


def _dense_mixers(p_nr, p_rope, p_plain, lam_p, sub_g, lam_init, n_ctx):
    b, l, _ = p_nr.shape
    kern = functools.partial(_dense_kernel, n_ctx=n_ctx, n_lat=l - n_ctx, lam_init=lam_init)
    tile = lambda blk: pl.BlockSpec((1, TQ, 512), lambda i, j: (i, j, blk))
    full = lambda blk: pl.BlockSpec((1, l, 256), lambda i, j: (i, 0, blk))
    out = jax.ShapeDtypeStruct((b, l, BRANCH_W), BF16)
    return pl.pallas_call(
        kern,
        grid=(b, l // TQ),
        in_specs=[tile(0), full(2), full(PL_CV),
                  tile(1), full(4), full(PL_DV),
                  pl.BlockSpec((4, HEAD_DIM), lambda i, j: (0, 0)),
                  pl.BlockSpec((1, DIF_DV), lambda i, j: (0, 0))],
        out_specs=[tile(0), tile(0)],
        out_shape=[out, out],
        scratch_shapes=_flash_scratch(N_GLB_BLOCKS + N_DIF_BLOCKS),
        compiler_params=_cparams(("parallel", "arbitrary")),
        name="dense_mixers",
    )(p_nr, p_nr, p_plain, p_rope, p_rope, p_plain, lam_p, sub_g.reshape(1, DIF_DV))


WIN_SPAN = TQ + 2 * WINDOW
WIN_BLOCKS = WIN_HEADS // WIN_KV


def _window_kernel(sink_ref, q_ref, k_ref, v_ref, o_ref, *scratch, n_ctx, total):
    j = pl.program_id(1)
    start = pl.multiple_of(jnp.clip(j * TQ - WINDOW, 0, total - WIN_SPAN), LANES)
    streams = [scratch[3 * g:3 * (g + 1)] for g in range(WIN_BLOCKS)]
    for g, (qs_s, s_s, _) in enumerate(streams):
        qs_s[...] = _stack_halves(q_ref[0, :, g * LANES:(g + 1) * LANES])
        s_s[:, 0:n_ctx] = _dot_nt(qs_s[...], k_ref[0, 0:n_ctx, :])
        s_s[:, n_ctx:] = _dot_nt(qs_s[...], k_ref[0, pl.ds(start, WIN_SPAN), :])

    row = lax.broadcasted_iota(jnp.int32, (2 * TQ, WIN_SPAN), 0)
    qpos = j * TQ + jnp.where(row >= TQ, row - TQ, row)
    kpos = start + lax.broadcasted_iota(jnp.int32, (2 * TQ, WIN_SPAN), 1)
    valid = (kpos >= n_ctx) & (qpos >= n_ctx) & (jnp.abs(kpos - qpos) <= WINDOW)
    srow = lax.broadcasted_iota(jnp.int32, (2 * TQ, LANES), 0)
    lane = lax.broadcasted_iota(jnp.int32, (TQ, LANES), 1)
    width = n_ctx + WIN_SPAN
    for g, (_, s_s, p_s) in enumerate(streams):
        sink = jnp.where(srow < TQ, sink_ref[g], sink_ref[WIN_BLOCKS + g]) * LOG2E
        s_c = s_s[:, 0:n_ctx]
        s_w = jnp.where(valid, s_s[:, n_ctx:], NEG_INF)
        m = jnp.maximum(jnp.maximum(jnp.max(s_c, axis=1, keepdims=True), jnp.max(s_w, axis=1, keepdims=True)), sink)
        p_c = jnp.exp2(s_c - jnp.concatenate([m] * (n_ctx // LANES), axis=1))
        p_w = jnp.exp2(s_w - jnp.concatenate([m] * (WIN_SPAN // LANES), axis=1))
        p_s[:, 0:n_ctx] = p_c.astype(BF16)
        p_s[:, n_ctx:] = p_w.astype(BF16)
        psum = p_c[:, 0:LANES]
        for t in range(1, n_ctx // LANES):
            psum = psum + p_c[:, t * LANES:(t + 1) * LANES]
        for t in range(WIN_SPAN // LANES):
            psum = psum + p_w[:, t * LANES:(t + 1) * LANES]
        den = jnp.sum(psum, axis=1, keepdims=True) + jnp.exp2(sink - m)
        o = (_dot(p_s[:, 0:n_ctx], v_ref[0, 0:n_ctx, :])
             + _dot(p_s[:, n_ctx:width], v_ref[0, pl.ds(start, WIN_SPAN), :])) / den
        o_ref[0, :, g * LANES:(g + 1) * LANES] = jnp.where(lane < HEAD_DIM, o[0:TQ], o[TQ:2 * TQ]).astype(BF16)


def _window_mixer(p_rope, p_plain, sink, n_ctx):
    b, l, _ = p_rope.shape
    kern = functools.partial(_window_kernel, n_ctx=n_ctx, total=l)
    grid_spec = pltpu.PrefetchScalarGridSpec(
        num_scalar_prefetch=1,
        grid=(b, l // TQ),
        in_specs=[pl.BlockSpec((1, TQ, 512), lambda i, j, s: (i, j, 0)),
                  pl.BlockSpec((1, l, LANES), lambda i, j, s: (i, 0, 10)),
                  pl.BlockSpec((1, l, LANES), lambda i, j, s: (i, 0, PL_BV))],
        out_specs=pl.BlockSpec((1, TQ, 512), lambda i, j, s: (i, j, 0)),
        scratch_shapes=[pltpu.VMEM((2 * TQ, LANES), BF16), pltpu.VMEM((2 * TQ, n_ctx + WIN_SPAN), F32),
                        pltpu.VMEM((2 * TQ, n_ctx + WIN_SPAN), BF16)] * WIN_BLOCKS,
    )
    return pl.pallas_call(
        kern,
        grid_spec=grid_spec,
        out_shape=jax.ShapeDtypeStruct((b, l, BRANCH_W), BF16),
        compiler_params=_cparams(("parallel", "arbitrary")),
        name="window_mixer",
    )(sink, p_rope, p_rope, p_plain)


GLA_GROUP = 4


def _gla_kernel(q_ref, k_ref, v_ref, g_ref, wg_ref, bg_ref, gn_ref, o_ref, accf_s, accb_s, st_s, *, total, n_ctx):
    ck = GLA_CHUNK
    n_groups = total // (ck * GLA_GROUP)
    ctx_groups = n_ctx // (ck * GLA_GROUP)
    hw = GLA_HEADS * GLA_DK
    acc_refs = (accf_s, accb_s)

    st_s[...] = jnp.zeros_like(st_s)

    rr = lax.broadcasted_iota(jnp.int32, (ck, ck), 0)
    cc = lax.broadcasted_iota(jnp.int32, (ck, ck), 1)
    tri = (rr >= cc, rr <= cc)
    tri_b = tuple(jnp.where(t, 1.0, 0.0).astype(BF16) for t in tri)
    tri2 = tuple(jnp.concatenate([t, t], axis=0) for t in tri)
    lane = lax.broadcasted_iota(jnp.int32, (LANES, LANES), 1)

    def body(i, _):
        gf = i
        gb = jnp.where(i < ctx_groups, ctx_groups - 1 - i, n_groups - 1 - (i - ctx_groups))
        chains = []
        for u in range(GLA_GROUP):
            for d, grp in ((0, gf), (1, gb)):
                c = grp * GLA_GROUP + (u if d == 0 else GLA_GROUP - 1 - u)
                chains.append((d, pl.multiple_of(c * ck, ck)))

        gate = [_dot(g_ref[0, pl.ds(r0, ck), :], wg_ref[:, d * hw:(d + 1) * hw]) + bg_ref[:, d * hw:(d + 1) * hw]
                for d, r0 in chains]
        cums = []
        for (d, r0), y in zip(chains, gate):
            la = (jnp.minimum(y, 0.0) - jnp.log(1.0 + jnp.exp(-jnp.abs(y)))) * (1.0 / GLA_TAU)
            hi = la.astype(BF16)
            lo = (la - hi.astype(F32)).astype(BF16)
            cums.append(_dot(tri_b[d], hi) + _dot(tri_b[d], lo))

        prepped = []
        for (d, r0), cum in zip(chains, cums):
            tot = cum[ck - 1:ck, :] if d == 0 else cum[0:1, :]
            qf = q_ref[0, pl.ds(r0, ck), :].astype(F32)
            kf = k_ref[0, pl.ds(r0, ck), :].astype(F32)
            qt = (qf * jnp.exp(cum)).astype(BF16)
            kt = (kf * jnp.exp(-cum)).astype(BF16)
            kw = (kf * jnp.exp(tot - cum)).astype(BF16)
            qs, a2 = [], []
            for p in range(2):
                sl = slice(p * LANES, (p + 1) * LANES)
                qs.append(_stack_halves(qt[:, sl]))
                a2.append(jnp.where(tri2[d], _dot_nt(qs[p], kt[:, sl]), 0.0).astype(BF16))
            prepped.append((qs, a2, kw, jnp.exp(tot)))

        intra, update = [], []
        for (d, r0), (qs, a2, kw, dec) in zip(chains, prepped):
            av, upd = [], []
            for p in range(2):
                sl = slice(p * LANES, (p + 1) * LANES)
                halves = []
                for hh in range(2):
                    vh = v_ref[0, pl.ds(r0, ck), (2 * p + hh) * GLA_DV:(2 * p + hh + 1) * GLA_DV]
                    av.append(_dot(a2[p][hh * ck:(hh + 1) * ck], vh))
                    halves.append(_dot_tn(vh, kw[:, sl]))
                upd.append(jnp.where(lane < GLA_DK, halves[0], halves[1]))
            intra.append(av)
            update.append(upd)

        states = [[st_s[2 * d + p] for p in range(2)] for d in range(2)]
        for (d, r0), (qs, a2, kw, dec), av, upd in zip(chains, prepped, intra, update):
            for p in range(2):
                inter = _dot_nt(qs[p], states[d][p].astype(BF16))
                for hh in range(2):
                    vs = slice((2 * p + hh) * GLA_DV, (2 * p + hh + 1) * GLA_DV)
                    acc_refs[d][pl.ds(r0, ck), vs] = av[2 * p + hh] + inter[hh * ck:(hh + 1) * ck]
                states[d][p] = states[d][p] * dec[:, p * LANES:(p + 1) * LANES] + upd[p]
        for d in range(2):
            for p in range(2):
                st_s[2 * d + p] = states[d][p]
        return 0

    lax.fori_loop(0, n_groups, body, 0)

    def finish(i, _):
        r0 = pl.multiple_of(i * TQ, TQ)
        for h in range(GLA_HEADS):
            vs = slice(h * GLA_DV, (h + 1) * GLA_DV)
            o = accf_s[pl.ds(r0, TQ), vs] + accb_s[pl.ds(r0, TQ), vs]
            y = o * lax.rsqrt(jnp.mean(o * o, axis=-1, keepdims=True) + EPS) * gn_ref[:, vs]
            o_ref[0, pl.ds(r0, TQ), vs] = y.astype(BF16)
        return 0

    lax.fori_loop(0, total // TQ, finish, 0)


def _gla_mixer(p_plain, wg, bg, gn, n_ctx):
    b, l, _ = p_plain.shape
    kern = functools.partial(_gla_kernel, total=l, n_ctx=n_ctx)
    const = lambda i: (0, 0)
    return pl.pallas_call(
        kern,
        grid=(b,),
        in_specs=[pl.BlockSpec((1, l, 256), lambda i: (i, 0, PL_AQ)),
                  pl.BlockSpec((1, l, 256), lambda i: (i, 0, PL_AK)),
                  pl.BlockSpec((1, l, 512), lambda i: (i, 0, PL_AV)),
                  pl.BlockSpec((1, l, LANES), lambda i: (i, 0, PL_AG)),
                  pl.BlockSpec((LANES, 512), const),
                  pl.BlockSpec((1, 512), const),
                  pl.BlockSpec((1, 512), const)],
        out_specs=pl.BlockSpec((1, l, BRANCH_W), lambda i: (i, 0, 0)),
        out_shape=jax.ShapeDtypeStruct((b, l, BRANCH_W), BF16),
        scratch_shapes=[pltpu.VMEM((l, BRANCH_W), F32), pltpu.VMEM((l, BRANCH_W), F32),
                        pltpu.VMEM((4, GLA_DV, LANES), F32)],
        compiler_params=_cparams(("parallel",)),
        name="gla_mixer",
    )(p_plain, p_plain, p_plain, p_plain, wg, bg, gn)


def _merge_kernel(h_ref, oa_ref, ob_ref, oc_ref, od_ref, za_ref, zb_ref, zc_ref, zd_ref, x_ref, mod_ref,
                  wm_ref, wup_ref, wout_ref, lng_ref, lnb_ref, out_ref, *, alpha):
    h = h_ref[0]
    acc = None
    for i, (o_ref, z_ref) in enumerate(((oa_ref, za_ref), (ob_ref, zb_ref), (oc_ref, zc_ref), (od_ref, zd_ref))):
        z = z_ref[0].astype(F32)
        br = (o_ref[0].astype(F32) * (z * _sigmoid(z))).astype(BF16)
        term = _sigmoid(_dot(h, wm_ref[i])) * _dot(br, wup_ref[i])
        acc = term if acc is None else acc + term
    y = _dot(acc.astype(BF16), wout_ref[...])
    r = alpha * x_ref[0] + mod_ref[0, 0, 2:3, :] * y
    mu = jnp.mean(r, axis=-1, keepdims=True)
    rc = r - mu
    var = jnp.mean(rc * rc, axis=-1, keepdims=True)
    out_ref[0] = rc * lax.rsqrt(var + EPS) * lng_ref[...] + lnb_ref[...]


def _merge(h, outs, p_plain, x_all, mod, wm, wup, wout, ln_g, ln_b, skip_ctx):
    b, l, d = h.shape
    off = 1 if skip_ctx else 0
    nt = l // TQ - off
    row = lambda i, j: (i, j + off, 0)
    zspec = lambda blk: pl.BlockSpec((1, TQ, BRANCH_W), lambda i, j: (i, j + off, blk))
    c2 = lambda i, j: (0, 0)
    c3 = lambda i, j: (0, 0, 0)
    in_specs = ([pl.BlockSpec((1, TQ, d), row)]
                + [pl.BlockSpec((1, TQ, BRANCH_W), row)] * 4
                + [zspec(PL_AZ), zspec(PL_BZ), zspec(PL_CZ), zspec(PL_DZ)]
                + [pl.BlockSpec((1, TQ, d), row),
                   pl.BlockSpec((1, 1, 3, d), lambda i, j: (i, jnp.minimum(j + off, 1), 0, 0)),
                   pl.BlockSpec((4, d, d), c3), pl.BlockSpec((4, BRANCH_W, d), c3), pl.BlockSpec((d, d), c2),
                   pl.BlockSpec((1, d), c2), pl.BlockSpec((1, d), c2)])
    return pl.pallas_call(
        functools.partial(_merge_kernel, alpha=(2 * DEPTH) ** 0.25),
        grid=(b, nt),
        in_specs=in_specs,
        out_specs=pl.BlockSpec((1, TQ, d), lambda i, j: (i, j, 0)),
        out_shape=jax.ShapeDtypeStruct((b, nt * TQ, d), F32),
        compiler_params=_cparams(("parallel", "parallel")),
        name="merge",
    )(h, *outs, p_plain, p_plain, p_plain, p_plain, x_all, mod, wm, wup, wout,
      ln_g.reshape(1, d), ln_b.reshape(1, d))


def _perm_window(a, axis):
    a = jnp.moveaxis(a, axis, -1)
    lead = a.shape[:-1]
    a = a.reshape(lead + (WIN_KV, WIN_HEADS // WIN_KV, HEAD_DIM)).swapaxes(-3, -2).reshape(lead + (BRANCH_W,))
    return jnp.moveaxis(a, -1, axis)


def _perm_global(a, axis):
    a = jnp.moveaxis(a, axis, -1)
    lead = a.shape[:-1]
    a = a.reshape(lead + (GLB_KV // 2, 2, GLB_HEADS // GLB_KV, HEAD_DIM)).swapaxes(-3, -2).reshape(lead + (BRANCH_W,))
    return jnp.moveaxis(a, -1, axis)


def _split_w_in(w):
    cols = lambda off, n: w[:, off:off + n]
    pad = jnp.zeros((w.shape[0], LANES - 2 * GLA_RANK), w.dtype)
    w_plain = jnp.concatenate([
        cols(_A_V, 512), cols(_A_Z, 512), _perm_window(cols(_B_Z, 512), 1), _perm_global(cols(_C_Z, 512), 1),
        cols(_D_Z, 512), cols(_A_Q, 256) * GLA_DK ** -0.5, cols(_A_K, 256), cols(_C_V, 256), cols(_D_V, 256),
        cols(_B_V, 128), cols(_A_GF, 2 * GLA_RANK), pad], axis=1)
    w_rope = jnp.concatenate([_perm_window(cols(_B_Q, 512), 1) * HEAD_DIM ** -0.5, cols(_D_Q, 512) * HEAD_DIM ** -0.5,
                              cols(_D_K, 256), cols(_B_K, 128)], axis=1)
    w_nr = jnp.concatenate([_perm_global(cols(_C_Q, 512), 1), cols(_C_K, 256)], axis=1)
    return w_plain.astype(BF16), w_rope.astype(BF16), w_nr.astype(BF16)


def _rope_tables(n_ctx, n_lat):
    t = np.arange(n_lat)
    freqs = ROPE_BASE ** (-np.arange(ROPE_HALF, dtype=np.float32) / ROPE_HALF)
    pos = np.stack([(t // GRID_W).astype(np.float32), (t % GRID_W).astype(np.float32)], axis=1)
    ang = jnp.asarray(pos[:, :, None] * freqs[None, None, :], F32)
    cos, sin = jnp.cos(ang), jnp.sin(ang)
    cos_h = jnp.concatenate([cos, cos], axis=-1).reshape(n_lat, HEAD_DIM)
    sin_h = jnp.concatenate([-sin, sin], axis=-1).reshape(n_lat, HEAD_DIM)
    cos_t = jnp.concatenate([jnp.ones((n_ctx, HEAD_DIM), F32), cos_h], axis=0)
    sin_t = jnp.concatenate([jnp.zeros((n_ctx, HEAD_DIM), F32), sin_h], axis=0)
    return jnp.tile(cos_t, (1, 2)), jnp.tile(sin_t, (1, 2))


def kernel(x, c, ctx, c_ctx, w_ada, b_ada, w_in, gla_w_gate, gla_b_gate, gla_norm, win_sink, glb_q_norm,
           glb_k_norm, diff_lambda, diff_norm, w_merge, w_up, w_out, ln_g, ln_b):
    b, n_lat, d = x.shape
    n_ctx = ctx.shape[1]
    assert n_ctx == TQ and n_lat % TK == 0 and d == 1024
    x_all = jnp.concatenate([ctx, x], axis=1)
    cs = jnp.zeros((16, d), F32).at[0:b].set(c).at[b].set(c_ctx)
    cos_t, sin_t = _rope_tables(n_ctx, n_lat)
    seg = jnp.asarray(np.kron(np.eye(2, dtype=np.float32), np.ones((HEAD_DIM, HEAD_DIM), np.float32)), BF16)

    for layer in range(DEPTH):
        last = layer == DEPTH - 1
        lam_init = 0.8 - 0.6 * math.exp(-0.3 * layer)
        w_plain, w_rope, w_nr = _split_w_in(w_in[layer])
        gain_nr = jnp.concatenate([jnp.tile(glb_q_norm[layer] * (HEAD_DIM ** -0.5 * LOG2E), GLB_HEADS),
                                   jnp.tile(glb_k_norm[layer], GLB_KV)]).reshape(1, NR_W)
        wg = jnp.zeros((LANES, 2 * GLA_HEADS * GLA_DK), F32)
        wg = wg.at[0:GLA_RANK, 0:256].set(gla_w_gate[layer, 0]).at[GLA_RANK:2 * GLA_RANK, 256:512].set(gla_w_gate[layer, 1])
        bg = gla_b_gate[layer].reshape(1, 512)
        wup = jnp.stack([w_up[layer, 0], _perm_window(w_up[layer, 1], 0), _perm_global(w_up[layer, 2], 0),
                         w_up[layer, 3]]).astype(BF16)

        ada = _ada(cs, w_ada[layer], b_ada[layer])
        mod_x = ada[0:b].reshape(b, 1, 3, d)
        mod_c = jnp.broadcast_to(ada[b].reshape(1, 1, 3, d), (b, 1, 3, d))
        mod = jnp.concatenate([mod_c, mod_x], axis=1)

        h = _ln_mod(x_all, mod)
        p_plain = _inproj(h, w_plain, "plain")
        p_rope = _inproj(h, w_rope, "rope", cos_t, sin_t)
        p_nr = _inproj(h, w_nr, "nr", cos_t, sin_t, gain_nr, seg)

        o_a = _gla_mixer(p_plain, wg.astype(BF16), bg, gla_norm[layer].reshape(1, BRANCH_W), n_ctx)
        o_b = _window_mixer(p_rope, p_plain, win_sink[layer], n_ctx)
        o_c, o_d = _dense_mixers(p_nr, p_rope, p_plain, diff_lambda[layer], diff_norm[layer], lam_init, n_ctx)

        x_all = _merge(h, (o_a, o_b, o_c, o_d), p_plain, x_all, mod, w_merge[layer].astype(BF16), wup,
                       w_out[layer].astype(BF16), ln_g[layer], ln_b[layer], skip_ctx=last)
    return x_all
```

```python
import functools
import math

import numpy as np
import jax
import jax.numpy as jnp
from jax import lax
from jax.experimental import pallas as pl
from jax.experimental.pallas import tpu as pltpu

F32 = jnp.float32
BF16 = jnp.bfloat16

DEPTH = 2
GRID_W = 64
HEAD_DIM = 64
ROPE_HALF = HEAD_DIM // 4
ROPE_BASE = 10000.0
EPS = 1e-6
NEG_INF = -1e30
GLA_HEADS, GLA_DK, GLA_DV, GLA_RANK, GLA_TAU, GLA_CHUNK = 4, 64, 128, 16, 16.0, 64
WIN_HEADS, WIN_KV, WINDOW = 8, 2, 128
GLB_HEADS, GLB_KV = 8, 4
DIF_HEADS, DIF_KV, DIF_DV = 4, 2, 128
BRANCH_W = 512

LANES = 128
TQ = 256
TK = 512
VMEM_LIMIT = 56 << 20

_A_Q, _A_K, _A_V, _A_GF, _A_Z = 0, 256, 512, 1024, 1056
_B_Q, _B_K, _B_V, _B_Z = 1568, 2080, 2208, 2336
_C_Q, _C_K, _C_V, _C_Z = 2848, 3360, 3616, 3872
_D_Q, _D_K, _D_V, _D_Z = 4384, 4896, 5152, 5408

PL_AV, PL_AZ, PL_BZ, PL_CZ, PL_DZ = 0, 1, 2, 3, 4
PL_AQ, PL_AK, PL_CV, PL_DV = 10, 11, 12, 13
PL_BV, PL_AG = 28, 29
PLAIN_W, ROPE_W, NR_W = 3840, 1408, 768
ROPE_QW = 1024
LOG2E = math.log2(math.e)


def _cparams(sem):
    return pltpu.CompilerParams(dimension_semantics=sem, vmem_limit_bytes=VMEM_LIMIT)


def _sigmoid(x):
    return 1.0 / (1.0 + jnp.exp(-x))


def _dot(a, b):
    return jnp.dot(a, b, preferred_element_type=F32)


def _dot_nt(a, b):
    return lax.dot_general(a, b, (((1,), (1,)), ((), ())), preferred_element_type=F32)


def _dot_tn(a, b):
    return lax.dot_general(a, b, (((0,), (0,)), ((), ())), preferred_element_type=F32)


def _ada_kernel(c_ref, w_ref, b_ref, o_ref):
    c = c_ref[...]
    s = c * _sigmoid(c)
    o_ref[...] = jnp.dot(s, w_ref[...], preferred_element_type=F32,
                         precision=lax.Precision.HIGHEST) + b_ref[...]


def _ada(cs, w_ada, b_ada):
    r, d = cs.shape
    n = w_ada.shape[1]
    bn = 1024
    return pl.pallas_call(
        _ada_kernel,
        grid=(n // bn,),
        in_specs=[pl.BlockSpec((r, d), lambda j: (0, 0)),
                  pl.BlockSpec((d, bn), lambda j: (0, j)),
                  pl.BlockSpec((1, bn), lambda j: (0, j))],
        out_specs=pl.BlockSpec((r, bn), lambda j: (0, j)),
        out_shape=jax.ShapeDtypeStruct((r, n), F32),
        compiler_params=_cparams(("arbitrary",)),
        name="ada",
    )(cs, w_ada, b_ada.reshape(1, n))


def _stream_specs(xs, d, off=0):
    if len(xs) == 1:
        return [pl.BlockSpec((1, TQ, d), lambda i, j: (i, j + off, 0))]
    return [pl.BlockSpec((1, TQ, d), lambda i, j: (i, 0, 0)),
            pl.BlockSpec((1, TQ, d), lambda i, j: (i, jnp.maximum(j + off - 1, 0), 0))]


def _stream_tile(refs, off=0):
    if len(refs) == 1:
        return refs[0][0]
    return jnp.where(pl.program_id(1) + off == 0, refs[0][0], refs[1][0])


def _ln_kernel(*refs):
    *x_refs, mod_ref, h_ref = refs
    x = _stream_tile(x_refs)
    mu = jnp.mean(x, axis=-1, keepdims=True)
    xc = x - mu
    var = jnp.mean(xc * xc, axis=-1, keepdims=True)
    y = xc * lax.rsqrt(var + EPS)
    shift = mod_ref[0, 0, 0:1, :]
    scale = mod_ref[0, 0, 1:2, :]
    h_ref[0] = (y * (1.0 + scale) + shift).astype(BF16)


def _ln_mod(xs, mod):
    b, _, d = xs[0].shape
    l = sum(a.shape[1] for a in xs)
    return pl.pallas_call(
        _ln_kernel,
        grid=(b, l // TQ),
        in_specs=_stream_specs(xs, d) + [pl.BlockSpec((1, 1, 3, d), lambda i, j: (i, jnp.minimum(j, 1), 0, 0))],
        out_specs=pl.BlockSpec((1, TQ, d), lambda i, j: (i, j, 0)),
        out_shape=jax.ShapeDtypeStruct((b, l, d), BF16),
        compiler_params=_cparams(("parallel", "parallel")),
        name="ln_mod",
    )(*xs, mod)


def _rope(y, cos, sin, lo16):
    sw = jnp.where(lo16, pltpu.roll(y, LANES - ROPE_HALF, 1), pltpu.roll(y, ROPE_HALF, 1))
    return y * cos + sw * sin


def _inproj_plain_kernel(h_ref, w_ref, o_ref):
    o_ref[0] = _dot(h_ref[0], w_ref[...]).astype(BF16)


def _inproj_rope_kernel(h_ref, w_ref, cos_ref, sin_ref, o_ref, *, width, q_width):
    acc = _dot(h_ref[0], w_ref[...])
    cos, sin = cos_ref[...], sin_ref[...]
    lo16 = (lax.broadcasted_iota(jnp.int32, cos.shape, 1) % (2 * ROPE_HALF)) < ROPE_HALF
    for c in range(width // LANES):
        y = _rope(acc[:, c * LANES:(c + 1) * LANES], cos, sin, lo16)
        if c * LANES < q_width:
            y = y * LOG2E
        o_ref[0, :, c * LANES:(c + 1) * LANES] = y.astype(BF16)


def _inproj_nr_kernel(h_ref, w_ref, cos_ref, sin_ref, gain_ref, seg_ref, o_ref, *, width):
    acc = _dot(h_ref[0], w_ref[...])
    cos, sin = cos_ref[...], sin_ref[...]
    lo16 = (lax.broadcasted_iota(jnp.int32, cos.shape, 1) % (2 * ROPE_HALF)) < ROPE_HALF
    for c in range(width // LANES):
        y = acc[:, c * LANES:(c + 1) * LANES]
        ss = _dot((y * y).astype(BF16), seg_ref[...])
        y = y * lax.rsqrt(ss * (1.0 / HEAD_DIM) + EPS) * gain_ref[:, c * LANES:(c + 1) * LANES]
        o_ref[0, :, c * LANES:(c + 1) * LANES] = _rope(y, cos, sin, lo16).astype(BF16)


def _inproj(h, w, mode, cos=None, sin=None, gain=None, seg=None):
    b, l, d = h.shape
    n = w.shape[1]
    bm = l // 4
    if mode == "plain":
        bm, bn = l // 2, 768
        return pl.pallas_call(
            _inproj_plain_kernel,
            grid=(b, l // bm, n // bn),
            in_specs=[pl.BlockSpec((1, bm, d), lambda i, r, j: (i, r, 0)),
                      pl.BlockSpec((d, bn), lambda i, r, j: (0, j))],
            out_specs=pl.BlockSpec((1, bm, bn), lambda i, r, j: (i, r, j)),
            out_shape=jax.ShapeDtypeStruct((b, l, n), BF16),
            compiler_params=_cparams(("parallel", "parallel", "arbitrary")),
            name="inproj_plain",
        )(h, w)
    tab = pl.BlockSpec((bm, LANES), lambda i, r: (r, 0))
    in_specs = [pl.BlockSpec((1, bm, d), lambda i, r: (i, r, 0)),
                pl.BlockSpec((d, n), lambda i, r: (0, 0)), tab, tab]
    args = [h, w, cos, sin]
    if mode == "rope":
        body = functools.partial(_inproj_rope_kernel, width=n, q_width=ROPE_QW)
    else:
        body = functools.partial(_inproj_nr_kernel, width=n)
        in_specs += [pl.BlockSpec((1, n), lambda i, r: (0, 0)),
                     pl.BlockSpec((LANES, LANES), lambda i, r: (0, 0))]
        args += [gain, seg]
    return pl.pallas_call(
        body,
        grid=(b, l // bm),
        in_specs=in_specs,
        out_specs=pl.BlockSpec((1, bm, n), lambda i, r: (i, r, 0)),
        out_shape=jax.ShapeDtypeStruct((b, l, n), BF16),
        compiler_params=_cparams(("parallel", "parallel")),
        name="inproj_" + mode,
    )(*args)


def _stack_halves(q):
    lane = lax.broadcasted_iota(jnp.int32, q.shape, 1)
    zero = jnp.zeros_like(q)
    return jnp.concatenate([jnp.where(lane < HEAD_DIM, q, zero), jnp.where(lane >= HEAD_DIM, q, zero)], axis=0)


FLASH_ROWS = 2 * TQ


FLASH_REFS = 6


def _flash_scratch(n_streams):
    stat = pltpu.VMEM((FLASH_ROWS, LANES), F32)
    one = [pltpu.VMEM((FLASH_ROWS, LANES), BF16), stat, stat, stat,
           pltpu.VMEM((FLASH_ROWS, TK), F32), pltpu.VMEM((FLASH_ROWS, TK), BF16)]
    return one * n_streams


def _flash(q_blocks, kv_srcs, n_ctx, n_lat_chunks, with_latent, scratch):
    ns = len(q_blocks)
    streams = [scratch[FLASH_REFS * i:FLASH_REFS * (i + 1)] for i in range(ns)]
    kv_lanes = [slice(kb * LANES, (kb + 1) * LANES) for _, _, kb in kv_srcs]
    for q, (qs_s, _, _, _, _, _) in zip(q_blocks, streams):
        qs_s[...] = _stack_halves(q)

    def scores(i, r0, tk):
        qs_s, _, _, _, s_s, _ = streams[i]
        s_s[:, 0:tk] = _dot_nt(qs_s[...], kv_srcs[i][0][0, pl.ds(r0, tk), kv_lanes[i]])

    def accumulate(i, r0, tk, first=False):
        _, m_s, l_s, acc_s, s_s, p_s = streams[i]
        s = s_s[:, 0:tk]
        m_new = jnp.broadcast_to(jnp.max(s, axis=1, keepdims=True), (FLASH_ROWS, LANES))
        if not first:
            m_old = m_s[...]
            m_new = jnp.maximum(m_old, m_new)
            alpha = jnp.exp2(m_old - m_new)
        p = jnp.exp2(s - jnp.concatenate([m_new] * (tk // LANES), axis=1))
        p_s[:, 0:tk] = p.astype(BF16)
        psum = p[:, 0:LANES]
        for t in range(1, tk // LANES):
            psum = psum + p[:, t * LANES:(t + 1) * LANES]
        pv = _dot(p_s[:, 0:tk], kv_srcs[i][1][0, pl.ds(r0, tk), kv_lanes[i]])
        m_s[...] = m_new
        l_s[...] = psum if first else alpha * l_s[...] + psum
        acc_s[...] = pv if first else alpha * acc_s[...] + pv

    def stage(r0, tk, nxt, first=False):
        for i in range(ns):
            accumulate(i, r0, tk, first)
            if nxt is not None:
                scores(i, *nxt)

    for i in range(ns):
        scores(i, 0, n_ctx)
    stage(0, n_ctx, (n_ctx, TK), first=True)

    def body(c, _):
        r0 = pl.multiple_of(n_ctx + c * TK, LANES)
        stage(r0, TK, (r0 + TK, TK))
        return 0

    @pl.when(with_latent)
    def _():
        lax.fori_loop(0, n_lat_chunks - 1, body, 0)
        stage(n_ctx + (n_lat_chunks - 1) * TK, TK, None)

    return [acc_s[...] / jnp.sum(l_s[...], axis=1, keepdims=True) for _, _, l_s, acc_s, _, _ in streams]


N_GLB_BLOCKS = GLB_HEADS // 2
N_DIF_BLOCKS = DIF_HEADS


def _dense_kernel(qc_ref, kc_ref, vc_ref, qd_ref, kd_ref, vd_ref, lam_ref, g_ref, oc_ref, od_ref, *scratch,
                  n_ctx, n_lat, lam_init):
    q_blocks = ([qc_ref[0, :, c * LANES:(c + 1) * LANES] for c in range(N_GLB_BLOCKS)]
                + [qd_ref[0, :, c * LANES:(c + 1) * LANES] for c in range(N_DIF_BLOCKS)])
    kv_srcs = ([(kc_ref, vc_ref, c // 2) for c in range(N_GLB_BLOCKS)]
               + [(kd_ref, vd_ref, c // 2) for c in range(N_DIF_BLOCKS)])
    outs = _flash(q_blocks, kv_srcs, n_ctx, n_lat // TK, pl.program_id(1) > 0, scratch)

    lane = lax.broadcasted_iota(jnp.int32, (TQ, LANES), 1)
    for c, o in enumerate(outs[:N_GLB_BLOCKS]):
        oc_ref[0, :, c * LANES:(c + 1) * LANES] = jnp.where(lane < HEAD_DIM, o[0:TQ], o[TQ:2 * TQ]).astype(BF16)

    lp = lam_ref[...]
    lam = (jnp.exp(jnp.sum(lp[0:1] * lp[1:2], axis=1, keepdims=True))
           - jnp.exp(jnp.sum(lp[2:3] * lp[3:4], axis=1, keepdims=True)) + lam_init)
    for c, o in enumerate(outs[N_GLB_BLOCKS:]):
        o = o[0:TQ] - lam * o[TQ:2 * TQ]
        y = o * lax.rsqrt(jnp.mean(o * o, axis=-1, keepdims=True) + EPS) * g_ref[...]
        od_ref[0, :, c * LANES:(c + 1) * LANES] = (y * (1.0 - lam_init)).astype(BF16)


def _dense_mixers(p_nr, p_rope, p_plain, lam_p, sub_g, lam_init, n_ctx):
    b, l, _ = p_nr.shape
    kern = functools.partial(_dense_kernel, n_ctx=n_ctx, n_lat=l - n_ctx, lam_init=lam_init)
    tile = lambda blk: pl.BlockSpec((1, TQ, 512), lambda i, j: (i, j, blk))
    full = lambda blk: pl.BlockSpec((1, l, 256), lambda i, j: (i, 0, blk))
    out = jax.ShapeDtypeStruct((b, l, BRANCH_W), BF16)
    return pl.pallas_call(
        kern,
        grid=(b, l // TQ),
        in_specs=[tile(0), full(2), full(PL_CV),
                  tile(1), full(4), full(PL_DV),
                  pl.BlockSpec((4, HEAD_DIM), lambda i, j: (0, 0)),
                  pl.BlockSpec((1, DIF_DV), lambda i, j: (0, 0))],
        out_specs=[tile(0), tile(0)],
        out_shape=[out, out],
        scratch_shapes=_flash_scratch(N_GLB_BLOCKS + N_DIF_BLOCKS),
        compiler_params=_cparams(("parallel", "arbitrary")),
        name="dense_mixers",
    )(p_nr, p_nr, p_plain, p_rope, p_rope, p_plain, lam_p, sub_g.reshape(1, DIF_DV))


WIN_SPAN = TQ + 2 * WINDOW
WIN_BLOCKS = WIN_HEADS // WIN_KV


def _window_kernel(sink_ref, q_ref, k_ref, v_ref, o_ref, *scratch, n_ctx, total):
    j = pl.program_id(1)
    start = pl.multiple_of(jnp.clip(j * TQ - WINDOW, 0, total - WIN_SPAN), LANES)
    streams = [scratch[3 * g:3 * (g + 1)] for g in range(WIN_BLOCKS)]
    for g, (qs_s, s_s, _) in enumerate(streams):
        qs_s[...] = _stack_halves(q_ref[0, :, g * LANES:(g + 1) * LANES])
        s_s[:, 0:n_ctx] = _dot_nt(qs_s[...], k_ref[0, 0:n_ctx, :])
        s_s[:, n_ctx:] = _dot_nt(qs_s[...], k_ref[0, pl.ds(start, WIN_SPAN), :])

    row = lax.broadcasted_iota(jnp.int32, (2 * TQ, WIN_SPAN), 0)
    qpos = j * TQ + jnp.where(row >= TQ, row - TQ, row)
    kpos = start + lax.broadcasted_iota(jnp.int32, (2 * TQ, WIN_SPAN), 1)
    valid = (kpos >= n_ctx) & (qpos >= n_ctx) & (jnp.abs(kpos - qpos) <= WINDOW)
    srow = lax.broadcasted_iota(jnp.int32, (2 * TQ, LANES), 0)
    lane = lax.broadcasted_iota(jnp.int32, (TQ, LANES), 1)
    width = n_ctx + WIN_SPAN
    for g, (_, s_s, p_s) in enumerate(streams):
        sink = jnp.where(srow < TQ, sink_ref[g], sink_ref[WIN_BLOCKS + g]) * LOG2E
        s_c = s_s[:, 0:n_ctx]
        s_w = jnp.where(valid, s_s[:, n_ctx:], NEG_INF)
        m = jnp.maximum(jnp.maximum(jnp.max(s_c, axis=1, keepdims=True), jnp.max(s_w, axis=1, keepdims=True)), sink)
        p_c = jnp.exp2(s_c - jnp.concatenate([m] * (n_ctx // LANES), axis=1))
        p_w = jnp.exp2(s_w - jnp.concatenate([m] * (WIN_SPAN // LANES), axis=1))
        p_s[:, 0:n_ctx] = p_c.astype(BF16)
        p_s[:, n_ctx:] = p_w.astype(BF16)
        psum = p_c[:, 0:LANES]
        for t in range(1, n_ctx // LANES):
            psum = psum + p_c[:, t * LANES:(t + 1) * LANES]
        for t in range(WIN_SPAN // LANES):
            psum = psum + p_w[:, t * LANES:(t + 1) * LANES]
        den = jnp.sum(psum, axis=1, keepdims=True) + jnp.exp2(sink - m)
        o = (_dot(p_s[:, 0:n_ctx], v_ref[0, 0:n_ctx, :])
             + _dot(p_s[:, n_ctx:width], v_ref[0, pl.ds(start, WIN_SPAN), :])) / den
        o_ref[0, :, g * LANES:(g + 1) * LANES] = jnp.where(lane < HEAD_DIM, o[0:TQ], o[TQ:2 * TQ]).astype(BF16)


def _window_mixer(p_rope, p_plain, sink, n_ctx):
    b, l, _ = p_rope.shape
    kern = functools.partial(_window_kernel, n_ctx=n_ctx, total=l)
    grid_spec = pltpu.PrefetchScalarGridSpec(
        num_scalar_prefetch=1,
        grid=(b, l // TQ),
        in_specs=[pl.BlockSpec((1, TQ, 512), lambda i, j, s: (i, j, 0)),
                  pl.BlockSpec((1, l, LANES), lambda i, j, s: (i, 0, 10)),
                  pl.BlockSpec((1, l, LANES), lambda i, j, s: (i, 0, PL_BV))],
        out_specs=pl.BlockSpec((1, TQ, 512), lambda i, j, s: (i, j, 0)),
        scratch_shapes=[pltpu.VMEM((2 * TQ, LANES), BF16), pltpu.VMEM((2 * TQ, n_ctx + WIN_SPAN), F32),
                        pltpu.VMEM((2 * TQ, n_ctx + WIN_SPAN), BF16)] * WIN_BLOCKS,
    )
    return pl.pallas_call(
        kern,
        grid_spec=grid_spec,
        out_shape=jax.ShapeDtypeStruct((b, l, BRANCH_W), BF16),
        compiler_params=_cparams(("parallel", "arbitrary")),
        name="window_mixer",
    )(sink, p_rope, p_rope, p_plain)


GLA_GROUP = 4


def _gla_kernel(q_ref, k_ref, v_ref, g_ref, wg_ref, bg_ref, gn_ref, o_ref, accf_s, accb_s, st_s, *, total, n_ctx):
    ck = GLA_CHUNK
    n_groups = total // (ck * GLA_GROUP)
    ctx_groups = n_ctx // (ck * GLA_GROUP)
    hw = GLA_HEADS * GLA_DK
    acc_refs = (accf_s, accb_s)

    st_s[...] = jnp.zeros_like(st_s)

    rr = lax.broadcasted_iota(jnp.int32, (ck, ck), 0)
    cc = lax.broadcasted_iota(jnp.int32, (ck, ck), 1)
    tri = (rr >= cc, rr <= cc)
    tri_b = tuple(jnp.where(t, 1.0, 0.0).astype(BF16) for t in tri)
    tri2 = tuple(jnp.concatenate([t, t], axis=0) for t in tri)
    lane = lax.broadcasted_iota(jnp.int32, (LANES, LANES), 1)

    def body(i, _):
        gf = i
        gb = jnp.where(i < ctx_groups, ctx_groups - 1 - i, n_groups - 1 - (i - ctx_groups))
        chains = []
        for u in range(GLA_GROUP):
            for d, grp in ((0, gf), (1, gb)):
                c = grp * GLA_GROUP + (u if d == 0 else GLA_GROUP - 1 - u)
                chains.append((d, pl.multiple_of(c * ck, ck)))

        gate = [_dot(g_ref[0, pl.ds(r0, ck), :], wg_ref[:, d * hw:(d + 1) * hw]) + bg_ref[:, d * hw:(d + 1) * hw]
                for d, r0 in chains]
        cums = []
        for (d, r0), y in zip(chains, gate):
            la = (jnp.minimum(y, 0.0) - jnp.log(1.0 + jnp.exp(-jnp.abs(y)))) * (1.0 / GLA_TAU)
            hi = la.astype(BF16)
            lo = (la - hi.astype(F32)).astype(BF16)
            cums.append(_dot(tri_b[d], hi) + _dot(tri_b[d], lo))

        prepped = []
        for (d, r0), cum in zip(chains, cums):
            tot = cum[ck - 1:ck, :] if d == 0 else cum[0:1, :]
            qf = q_ref[0, pl.ds(r0, ck), :].astype(F32)
            kf = k_ref[0, pl.ds(r0, ck), :].astype(F32)
            qt = (qf * jnp.exp(cum)).astype(BF16)
            kt = (kf * jnp.exp(-cum)).astype(BF16)
            kw = (kf * jnp.exp(tot - cum)).astype(BF16)
            qs, a2 = [], []
            for p in range(2):
                sl = slice(p * LANES, (p + 1) * LANES)
                qs.append(_stack_halves(qt[:, sl]))
                a2.append(jnp.where(tri2[d], _dot_nt(qs[p], kt[:, sl]), 0.0).astype(BF16))
            prepped.append((qs, a2, kw, jnp.exp(tot)))

        intra, update = [], []
        for (d, r0), (qs, a2, kw, dec) in zip(chains, prepped):
            av, upd = [], []
            for p in range(2):
                sl = slice(p * LANES, (p + 1) * LANES)
                halves = []
                for hh in range(2):
                    vh = v_ref[0, pl.ds(r0, ck), (2 * p + hh) * GLA_DV:(2 * p + hh + 1) * GLA_DV]
                    av.append(_dot(a2[p][hh * ck:(hh + 1) * ck], vh))
                    halves.append(_dot_tn(vh, kw[:, sl]))
                upd.append(jnp.where(lane < GLA_DK, halves[0], halves[1]))
            intra.append(av)
            update.append(upd)

        states = [[st_s[2 * d + p] for p in range(2)] for d in range(2)]
        for (d, r0), (qs, a2, kw, dec), av, upd in zip(chains, prepped, intra, update):
            for p in range(2):
                inter = _dot_nt(qs[p], states[d][p].astype(BF16))
                for hh in range(2):
                    vs = slice((2 * p + hh) * GLA_DV, (2 * p + hh + 1) * GLA_DV)
                    acc_refs[d][pl.ds(r0, ck), vs] = av[2 * p + hh] + inter[hh * ck:(hh + 1) * ck]
                states[d][p] = states[d][p] * dec[:, p * LANES:(p + 1) * LANES] + upd[p]
        for d in range(2):
            for p in range(2):
                st_s[2 * d + p] = states[d][p]
        return 0

    lax.fori_loop(0, n_groups, body, 0)

    def finish(i, _):
        r0 = pl.multiple_of(i * TQ, TQ)
        for h in range(GLA_HEADS):
            vs = slice(h * GLA_DV, (h + 1) * GLA_DV)
            o = accf_s[pl.ds(r0, TQ), vs] + accb_s[pl.ds(r0, TQ), vs]
            y = o * lax.rsqrt(jnp.mean(o * o, axis=-1, keepdims=True) + EPS) * gn_ref[:, vs]
            o_ref[0, pl.ds(r0, TQ), vs] = y.astype(BF16)
        return 0

    lax.fori_loop(0, total // TQ, finish, 0)


def _gla_mixer(p_plain, wg, bg, gn, n_ctx):
    b, l, _ = p_plain.shape
    kern = functools.partial(_gla_kernel, total=l, n_ctx=n_ctx)
    const = lambda i: (0, 0)
    return pl.pallas_call(
        kern,
        grid=(b,),
        in_specs=[pl.BlockSpec((1, l, 256), lambda i: (i, 0, PL_AQ)),
                  pl.BlockSpec((1, l, 256), lambda i: (i, 0, PL_AK)),
                  pl.BlockSpec((1, l, 512), lambda i: (i, 0, PL_AV)),
                  pl.BlockSpec((1, l, LANES), lambda i: (i, 0, PL_AG)),
                  pl.BlockSpec((LANES, 512), const),
                  pl.BlockSpec((1, 512), const),
                  pl.BlockSpec((1, 512), const)],
        out_specs=pl.BlockSpec((1, l, BRANCH_W), lambda i: (i, 0, 0)),
        out_shape=jax.ShapeDtypeStruct((b, l, BRANCH_W), BF16),
        scratch_shapes=[pltpu.VMEM((l, BRANCH_W), F32), pltpu.VMEM((l, BRANCH_W), F32),
                        pltpu.VMEM((4, GLA_DV, LANES), F32)],
        compiler_params=_cparams(("parallel",)),
        name="gla_mixer",
    )(p_plain, p_plain, p_plain, p_plain, wg, bg, gn)


def _merge_kernel(h_ref, oa_ref, ob_ref, oc_ref, od_ref, za_ref, zb_ref, zc_ref, zd_ref, *rest, alpha, off):
    *x_refs, mod_ref, wm_ref, wup_ref, wout_ref, lng_ref, lnb_ref, out_ref = rest
    h = h_ref[0]
    acc = None
    for i, (o_ref, z_ref) in enumerate(((oa_ref, za_ref), (ob_ref, zb_ref), (oc_ref, zc_ref), (od_ref, zd_ref))):
        z = z_ref[0].astype(F32)
        br = (o_ref[0].astype(F32) * (z * _sigmoid(z))).astype(BF16)
        term = _sigmoid(_dot(h, wm_ref[i])) * _dot(br, wup_ref[i])
        acc = term if acc is None else acc + term
    y = _dot(acc.astype(BF16), wout_ref[...])
    r = alpha * _stream_tile(x_refs, off) + mod_ref[0, 0, 2:3, :] * y
    mu = jnp.mean(r, axis=-1, keepdims=True)
    rc = r - mu
    var = jnp.mean(rc * rc, axis=-1, keepdims=True)
    out_ref[0] = rc * lax.rsqrt(var + EPS) * lng_ref[...] + lnb_ref[...]


def _merge(h, outs, p_plain, xs, mod, wm, wup, wout, ln_g, ln_b, skip_ctx):
    b, l, d = h.shape
    off = 1 if skip_ctx else 0
    nt = l // TQ - off
    row = lambda i, j: (i, j + off, 0)
    zspec = lambda blk: pl.BlockSpec((1, TQ, BRANCH_W), lambda i, j: (i, j + off, blk))
    c2 = lambda i, j: (0, 0)
    c3 = lambda i, j: (0, 0, 0)
    in_specs = ([pl.BlockSpec((1, TQ, d), row)]
                + [pl.BlockSpec((1, TQ, BRANCH_W), row)] * 4
                + [zspec(PL_AZ), zspec(PL_BZ), zspec(PL_CZ), zspec(PL_DZ)]
                + _stream_specs(xs, d, off)
                + [pl.BlockSpec((1, 1, 3, d), lambda i, j: (i, jnp.minimum(j + off, 1), 0, 0)),
                   pl.BlockSpec((4, d, d), c3), pl.BlockSpec((4, BRANCH_W, d), c3), pl.BlockSpec((d, d), c2),
                   pl.BlockSpec((1, d), c2), pl.BlockSpec((1, d), c2)])
    return pl.pallas_call(
        functools.partial(_merge_kernel, alpha=(2 * DEPTH) ** 0.25, off=off),
        grid=(b, nt),
        in_specs=in_specs,
        out_specs=pl.BlockSpec((1, TQ, d), lambda i, j: (i, j, 0)),
        out_shape=jax.ShapeDtypeStruct((b, nt * TQ, d), F32),
        compiler_params=_cparams(("parallel", "parallel")),
        name="merge",
    )(h, *outs, p_plain, p_plain, p_plain, p_plain, *xs, mod, wm, wup, wout,
      ln_g.reshape(1, d), ln_b.reshape(1, d))


def _perm_window(a, axis):
    a = jnp.moveaxis(a, axis, -1)
    lead = a.shape[:-1]
    a = a.reshape(lead + (WIN_KV, WIN_HEADS // WIN_KV, HEAD_DIM)).swapaxes(-3, -2).reshape(lead + (BRANCH_W,))
    return jnp.moveaxis(a, -1, axis)


def _perm_global(a, axis):
    a = jnp.moveaxis(a, axis, -1)
    lead = a.shape[:-1]
    a = a.reshape(lead + (GLB_KV // 2, 2, GLB_HEADS // GLB_KV, HEAD_DIM)).swapaxes(-3, -2).reshape(lead + (BRANCH_W,))
    return jnp.moveaxis(a, -1, axis)


def _split_w_in(w):
    cols = lambda off, n: w[:, off:off + n]
    pad = jnp.zeros((w.shape[0], LANES - 2 * GLA_RANK), w.dtype)
    w_plain = jnp.concatenate([
        cols(_A_V, 512), cols(_A_Z, 512), _perm_window(cols(_B_Z, 512), 1), _perm_global(cols(_C_Z, 512), 1),
        cols(_D_Z, 512), cols(_A_Q, 256) * GLA_DK ** -0.5, cols(_A_K, 256), cols(_C_V, 256), cols(_D_V, 256),
        cols(_B_V, 128), cols(_A_GF, 2 * GLA_RANK), pad], axis=1)
    w_rope = jnp.concatenate([_perm_window(cols(_B_Q, 512), 1) * HEAD_DIM ** -0.5, cols(_D_Q, 512) * HEAD_DIM ** -0.5,
                              cols(_D_K, 256), cols(_B_K, 128)], axis=1)
    w_nr = jnp.concatenate([_perm_global(cols(_C_Q, 512), 1), cols(_C_K, 256)], axis=1)
    return w_plain.astype(BF16), w_rope.astype(BF16), w_nr.astype(BF16)


def _rope_tables(n_ctx, n_lat):
    t = np.arange(n_lat)
    freqs = ROPE_BASE ** (-np.arange(ROPE_HALF, dtype=np.float32) / ROPE_HALF)
    pos = np.stack([(t // GRID_W).astype(np.float32), (t % GRID_W).astype(np.float32)], axis=1)
    ang = jnp.asarray(pos[:, :, None] * freqs[None, None, :], F32)
    cos, sin = jnp.cos(ang), jnp.sin(ang)
    cos_h = jnp.concatenate([cos, cos], axis=-1).reshape(n_lat, HEAD_DIM)
    sin_h = jnp.concatenate([-sin, sin], axis=-1).reshape(n_lat, HEAD_DIM)
    cos_t = jnp.concatenate([jnp.ones((n_ctx, HEAD_DIM), F32), cos_h], axis=0)
    sin_t = jnp.concatenate([jnp.zeros((n_ctx, HEAD_DIM), F32), sin_h], axis=0)
    return jnp.tile(cos_t, (1, 2)), jnp.tile(sin_t, (1, 2))


def kernel(x, c, ctx, c_ctx, w_ada, b_ada, w_in, gla_w_gate, gla_b_gate, gla_norm, win_sink, glb_q_norm,
           glb_k_norm, diff_lambda, diff_norm, w_merge, w_up, w_out, ln_g, ln_b):
    b, n_lat, d = x.shape
    n_ctx = ctx.shape[1]
    assert n_ctx == TQ and n_lat % TK == 0 and d == 1024
    xs = (ctx, x)
    cs = jnp.zeros((16, d), F32).at[0:b].set(c).at[b].set(c_ctx)
    cos_t, sin_t = _rope_tables(n_ctx, n_lat)
    seg = jnp.asarray(np.kron(np.eye(2, dtype=np.float32), np.ones((HEAD_DIM, HEAD_DIM), np.float32)), BF16)

    for layer in range(DEPTH):
        last = layer == DEPTH - 1
        lam_init = 0.8 - 0.6 * math.exp(-0.3 * layer)
        w_plain, w_rope, w_nr = _split_w_in(w_in[layer])
        gain_nr = jnp.concatenate([jnp.tile(glb_q_norm[layer] * (HEAD_DIM ** -0.5 * LOG2E), GLB_HEADS),
                                   jnp.tile(glb_k_norm[layer], GLB_KV)]).reshape(1, NR_W)
        wg = jnp.zeros((LANES, 2 * GLA_HEADS * GLA_DK), F32)
        wg = wg.at[0:GLA_RANK, 0:256].set(gla_w_gate[layer, 0]).at[GLA_RANK:2 * GLA_RANK, 256:512].set(gla_w_gate[layer, 1])
        bg = gla_b_gate[layer].reshape(1, 512)
        wup = jnp.stack([w_up[layer, 0], _perm_window(w_up[layer, 1], 0), _perm_global(w_up[layer, 2], 0),
                         w_up[layer, 3]]).astype(BF16)

        ada = _ada(cs, w_ada[layer], b_ada[layer])
        mod_x = ada[0:b].reshape(b, 1, 3, d)
        mod_c = jnp.broadcast_to(ada[b].reshape(1, 1, 3, d), (b, 1, 3, d))
        mod = jnp.concatenate([mod_c, mod_x], axis=1)

        h = _ln_mod(xs, mod)
        p_plain = _inproj(h, w_plain, "plain")
        p_rope = _inproj(h, w_rope, "rope", cos_t, sin_t)
        p_nr = _inproj(h, w_nr, "nr", cos_t, sin_t, gain_nr, seg)

        o_a = _gla_mixer(p_plain, wg.astype(BF16), bg, gla_norm[layer].reshape(1, BRANCH_W), n_ctx)
        o_b = _window_mixer(p_rope, p_plain, win_sink[layer], n_ctx)
        o_c, o_d = _dense_mixers(p_nr, p_rope, p_plain, diff_lambda[layer], diff_norm[layer], lam_init, n_ctx)

        xs = (_merge(h, (o_a, o_b, o_c, o_d), p_plain, xs, mod, w_merge[layer].astype(BF16), wup,
                     w_out[layer].astype(BF16), ln_g[layer], ln_b[layer], skip_ctx=last),)
    return xs[0]
```

```python
import functools
import math

import numpy as np
import jax
import jax.numpy as jnp
from jax import lax
from jax.experimental import pallas as pl
from jax.experimental.pallas import tpu as pltpu

F32 = jnp.float32
BF16 = jnp.bfloat16

DEPTH = 2
GRID_W = 64
HEAD_DIM = 64
ROPE_HALF = HEAD_DIM // 4
ROPE_BASE = 10000.0
EPS = 1e-6
NEG_INF = -1e30
GLA_HEADS, GLA_DK, GLA_DV, GLA_RANK, GLA_TAU, GLA_CHUNK = 4, 64, 128, 16, 16.0, 64
WIN_HEADS, WIN_KV, WINDOW = 8, 2, 128
GLB_HEADS, GLB_KV = 8, 4
DIF_HEADS, DIF_KV, DIF_DV = 4, 2, 128
BRANCH_W = 512

LANES = 128
TQ = 256
TK = 512
VMEM_LIMIT = 56 << 20

_A_Q, _A_K, _A_V, _A_GF, _A_Z = 0, 256, 512, 1024, 1056
_B_Q, _B_K, _B_V, _B_Z = 1568, 2080, 2208, 2336
_C_Q, _C_K, _C_V, _C_Z = 2848, 3360, 3616, 3872
_D_Q, _D_K, _D_V, _D_Z = 4384, 4896, 5152, 5408

PL_AV, PL_AZ, PL_BZ, PL_CZ, PL_DZ = 0, 1, 2, 3, 4
PL_AQ, PL_AK, PL_CV, PL_DV = 10, 11, 12, 13
PL_BV, PL_AG = 28, 29
PLAIN_W, ROPE_W, NR_W = 3840, 1408, 768
ROPE_QW = 1024
LOG2E = math.log2(math.e)


def _cparams(sem):
    return pltpu.CompilerParams(dimension_semantics=sem, vmem_limit_bytes=VMEM_LIMIT)


def _sigmoid(x):
    return 1.0 / (1.0 + jnp.exp(-x))


def _dot(a, b):
    return jnp.dot(a, b, preferred_element_type=F32)


def _dot_nt(a, b):
    return lax.dot_general(a, b, (((1,), (1,)), ((), ())), preferred_element_type=F32)


def _dot_tn(a, b):
    return lax.dot_general(a, b, (((0,), (0,)), ((), ())), preferred_element_type=F32)


def _ada_kernel(c_ref, w_ref, b_ref, o_ref):
    c = c_ref[...]
    s = c * _sigmoid(c)
    o_ref[...] = jnp.dot(s, w_ref[...], preferred_element_type=F32,
                         precision=lax.Precision.HIGHEST) + b_ref[...]


def _ada(cs, w_ada, b_ada):
    r, d = cs.shape
    depth, _, n = w_ada.shape
    bn = 1024
    return pl.pallas_call(
        _ada_kernel,
        grid=(depth, n // bn),
        in_specs=[pl.BlockSpec((r, d), lambda l, j: (0, 0)),
                  pl.BlockSpec((None, d, bn), lambda l, j: (l, 0, j)),
                  pl.BlockSpec((None, 1, bn), lambda l, j: (l, 0, j))],
        out_specs=pl.BlockSpec((None, r, bn), lambda l, j: (l, 0, j)),
        out_shape=jax.ShapeDtypeStruct((depth, r, n), F32),
        compiler_params=_cparams(("arbitrary", "arbitrary")),
        name="ada",
    )(cs, w_ada, b_ada.reshape(depth, 1, n))


def _stream_specs(xs, d, off=0):
    if len(xs) == 1:
        return [pl.BlockSpec((1, TQ, d), lambda i, j: (i, j + off, 0))]
    return [pl.BlockSpec((1, TQ, d), lambda i, j: (i, 0, 0)),
            pl.BlockSpec((1, TQ, d), lambda i, j: (i, jnp.maximum(j + off - 1, 0), 0))]


def _stream_tile(refs, off=0):
    if len(refs) == 1:
        return refs[0][0]
    return jnp.where(pl.program_id(1) + off == 0, refs[0][0], refs[1][0])


def _ln_kernel(*refs):
    *x_refs, mod_ref, h_ref = refs
    x = _stream_tile(x_refs)
    mu = jnp.mean(x, axis=-1, keepdims=True)
    xc = x - mu
    var = jnp.mean(xc * xc, axis=-1, keepdims=True)
    y = xc * lax.rsqrt(var + EPS)
    shift = mod_ref[0, 0, 0:1, :]
    scale = mod_ref[0, 0, 1:2, :]
    h_ref[0] = (y * (1.0 + scale) + shift).astype(BF16)


def _ln_mod(xs, mod):
    b, _, d = xs[0].shape
    l = sum(a.shape[1] for a in xs)
    return pl.pallas_call(
        _ln_kernel,
        grid=(b, l // TQ),
        in_specs=_stream_specs(xs, d) + [pl.BlockSpec((1, 1, 3, d), lambda i, j: (i, jnp.minimum(j, 1), 0, 0))],
        out_specs=pl.BlockSpec((1, TQ, d), lambda i, j: (i, j, 0)),
        out_shape=jax.ShapeDtypeStruct((b, l, d), BF16),
        compiler_params=_cparams(("parallel", "parallel")),
        name="ln_mod",
    )(*xs, mod)


def _rope(y, cos, sin, lo16):
    sw = jnp.where(lo16, pltpu.roll(y, LANES - ROPE_HALF, 1), pltpu.roll(y, ROPE_HALF, 1))
    return y * cos + sw * sin


def _inproj_plain_kernel(h_ref, w_ref, o_ref):
    o_ref[0] = _dot(h_ref[0], w_ref[...]).astype(BF16)


def _inproj_rope_kernel(h_ref, w_ref, cos_ref, sin_ref, o_ref, *, width, q_width):
    acc = _dot(h_ref[0], w_ref[...])
    cos, sin = cos_ref[...], sin_ref[...]
    lo16 = (lax.broadcasted_iota(jnp.int32, cos.shape, 1) % (2 * ROPE_HALF)) < ROPE_HALF
    for c in range(width // LANES):
        y = _rope(acc[:, c * LANES:(c + 1) * LANES], cos, sin, lo16)
        if c * LANES < q_width:
            y = y * LOG2E
        o_ref[0, :, c * LANES:(c + 1) * LANES] = y.astype(BF16)


def _inproj_nr_kernel(h_ref, w_ref, cos_ref, sin_ref, gain_ref, seg_ref, o_ref, *, width):
    acc = _dot(h_ref[0], w_ref[...])
    cos, sin = cos_ref[...], sin_ref[...]
    lo16 = (lax.broadcasted_iota(jnp.int32, cos.shape, 1) % (2 * ROPE_HALF)) < ROPE_HALF
    for c in range(width // LANES):
        y = acc[:, c * LANES:(c + 1) * LANES]
        ss = _dot((y * y).astype(BF16), seg_ref[...])
        y = y * lax.rsqrt(ss * (1.0 / HEAD_DIM) + EPS) * gain_ref[:, c * LANES:(c + 1) * LANES]
        o_ref[0, :, c * LANES:(c + 1) * LANES] = _rope(y, cos, sin, lo16).astype(BF16)


def _inproj(h, w, layer, mode, cos=None, sin=None, gain=None, seg=None):
    b, l, d = h.shape
    n = w.shape[2]
    bm = l // 4
    if mode == "plain":
        bm, bn = l // 2, 768
        return pl.pallas_call(
            _inproj_plain_kernel,
            grid=(b, l // bm, n // bn),
            in_specs=[pl.BlockSpec((1, bm, d), lambda i, r, j: (i, r, 0)),
                      pl.BlockSpec((None, d, bn), lambda i, r, j: (layer, 0, j))],
            out_specs=pl.BlockSpec((1, bm, bn), lambda i, r, j: (i, r, j)),
            out_shape=jax.ShapeDtypeStruct((b, l, n), BF16),
            compiler_params=_cparams(("parallel", "parallel", "arbitrary")),
            name="inproj_plain",
        )(h, w)
    tab = pl.BlockSpec((bm, LANES), lambda i, r: (r, 0))
    in_specs = [pl.BlockSpec((1, bm, d), lambda i, r: (i, r, 0)),
                pl.BlockSpec((None, d, n), lambda i, r: (layer, 0, 0)), tab, tab]
    args = [h, w, cos, sin]
    if mode == "rope":
        body = functools.partial(_inproj_rope_kernel, width=n, q_width=ROPE_QW)
    else:
        body = functools.partial(_inproj_nr_kernel, width=n)
        in_specs += [pl.BlockSpec((1, n), lambda i, r: (0, 0)),
                     pl.BlockSpec((LANES, LANES), lambda i, r: (0, 0))]
        args += [gain, seg]
    return pl.pallas_call(
        body,
        grid=(b, l // bm),
        in_specs=in_specs,
        out_specs=pl.BlockSpec((1, bm, n), lambda i, r: (i, r, 0)),
        out_shape=jax.ShapeDtypeStruct((b, l, n), BF16),
        compiler_params=_cparams(("parallel", "parallel")),
        name="inproj_" + mode,
    )(*args)


def _stack_halves(q):
    lane = lax.broadcasted_iota(jnp.int32, q.shape, 1)
    zero = jnp.zeros_like(q)
    return jnp.concatenate([jnp.where(lane < HEAD_DIM, q, zero), jnp.where(lane >= HEAD_DIM, q, zero)], axis=0)


FLASH_ROWS = 2 * TQ


FLASH_REFS = 6


def _flash_scratch(n_streams):
    stat = pltpu.VMEM((FLASH_ROWS, LANES), F32)
    one = [pltpu.VMEM((FLASH_ROWS, LANES), BF16), stat, stat, stat,
           pltpu.VMEM((FLASH_ROWS, TK), F32), pltpu.VMEM((FLASH_ROWS, TK), BF16)]
    return one * n_streams


def _flash(q_blocks, kv_srcs, n_ctx, n_lat_chunks, with_latent, scratch):
    ns = len(q_blocks)
    streams = [scratch[FLASH_REFS * i:FLASH_REFS * (i + 1)] for i in range(ns)]
    kv_lanes = [slice(kb * LANES, (kb + 1) * LANES) for _, _, kb in kv_srcs]
    for q, (qs_s, _, _, _, _, _) in zip(q_blocks, streams):
        qs_s[...] = _stack_halves(q)

    def scores(i, r0, tk):
        qs_s, _, _, _, s_s, _ = streams[i]
        s_s[:, 0:tk] = _dot_nt(qs_s[...], kv_srcs[i][0][0, pl.ds(r0, tk), kv_lanes[i]])

    def accumulate(i, r0, tk, first=False):
        _, m_s, l_s, acc_s, s_s, p_s = streams[i]
        s = s_s[:, 0:tk]
        m_new = jnp.broadcast_to(jnp.max(s, axis=1, keepdims=True), (FLASH_ROWS, LANES))
        if not first:
            m_old = m_s[...]
            m_new = jnp.maximum(m_old, m_new)
            alpha = jnp.exp2(m_old - m_new)
        p = jnp.exp2(s - jnp.concatenate([m_new] * (tk // LANES), axis=1))
        p_s[:, 0:tk] = p.astype(BF16)
        psum = p[:, 0:LANES]
        for t in range(1, tk // LANES):
            psum = psum + p[:, t * LANES:(t + 1) * LANES]
        pv = _dot(p_s[:, 0:tk], kv_srcs[i][1][0, pl.ds(r0, tk), kv_lanes[i]])
        m_s[...] = m_new
        l_s[...] = psum if first else alpha * l_s[...] + psum
        acc_s[...] = pv if first else alpha * acc_s[...] + pv

    def stage(r0, tk, nxt, first=False):
        for i in range(ns):
            accumulate(i, r0, tk, first)
            if nxt is not None:
                scores(i, *nxt)

    for i in range(ns):
        scores(i, 0, n_ctx)
    stage(0, n_ctx, (n_ctx, TK), first=True)

    def body(c, _):
        r0 = pl.multiple_of(n_ctx + c * TK, LANES)
        stage(r0, TK, (r0 + TK, TK))
        return 0

    @pl.when(with_latent)
    def _():
        lax.fori_loop(0, n_lat_chunks - 1, body, 0)
        stage(n_ctx + (n_lat_chunks - 1) * TK, TK, None)

    return [acc_s[...] / jnp.sum(l_s[...], axis=1, keepdims=True) for _, _, l_s, acc_s, _, _ in streams]


N_GLB_BLOCKS = GLB_HEADS // 2
N_DIF_BLOCKS = DIF_HEADS


def _dense_kernel(qc_ref, kc_ref, vc_ref, qd_ref, kd_ref, vd_ref, lam_ref, g_ref, oc_ref, od_ref, *scratch,
                  n_ctx, n_lat, lam_init):
    q_blocks = ([qc_ref[0, :, c * LANES:(c + 1) * LANES] for c in range(N_GLB_BLOCKS)]
                + [qd_ref[0, :, c * LANES:(c + 1) * LANES] for c in range(N_DIF_BLOCKS)])
    kv_srcs = ([(kc_ref, vc_ref, c // 2) for c in range(N_GLB_BLOCKS)]
               + [(kd_ref, vd_ref, c // 2) for c in range(N_DIF_BLOCKS)])
    outs = _flash(q_blocks, kv_srcs, n_ctx, n_lat // TK, pl.program_id(1) > 0, scratch)

    lane = lax.broadcasted_iota(jnp.int32, (TQ, LANES), 1)
    for c, o in enumerate(outs[:N_GLB_BLOCKS]):
        oc_ref[0, :, c * LANES:(c + 1) * LANES] = jnp.where(lane < HEAD_DIM, o[0:TQ], o[TQ:2 * TQ]).astype(BF16)

    lp = lam_ref[...]
    lam = (jnp.exp(jnp.sum(lp[0:1] * lp[1:2], axis=1, keepdims=True))
           - jnp.exp(jnp.sum(lp[2:3] * lp[3:4], axis=1, keepdims=True)) + lam_init)
    for c, o in enumerate(outs[N_GLB_BLOCKS:]):
        o = o[0:TQ] - lam * o[TQ:2 * TQ]
        y = o * lax.rsqrt(jnp.mean(o * o, axis=-1, keepdims=True) + EPS) * g_ref[...]
        od_ref[0, :, c * LANES:(c + 1) * LANES] = (y * (1.0 - lam_init)).astype(BF16)


def _dense_mixers(p_nr, p_rope, p_plain, lam_p, sub_g, lam_init, n_ctx):
    b, l, _ = p_nr.shape
    kern = functools.partial(_dense_kernel, n_ctx=n_ctx, n_lat=l - n_ctx, lam_init=lam_init)
    tile = lambda blk: pl.BlockSpec((1, TQ, 512), lambda i, j: (i, j, blk))
    full = lambda blk: pl.BlockSpec((1, l, 256), lambda i, j: (i, 0, blk))
    out = jax.ShapeDtypeStruct((b, l, BRANCH_W), BF16)
    return pl.pallas_call(
        kern,
        grid=(b, l // TQ),
        in_specs=[tile(0), full(2), full(PL_CV),
                  tile(1), full(4), full(PL_DV),
                  pl.BlockSpec((4, HEAD_DIM), lambda i, j: (0, 0)),
                  pl.BlockSpec((1, DIF_DV), lambda i, j: (0, 0))],
        out_specs=[tile(0), tile(0)],
        out_shape=[out, out],
        scratch_shapes=_flash_scratch(N_GLB_BLOCKS + N_DIF_BLOCKS),
        compiler_params=_cparams(("parallel", "arbitrary")),
        name="dense_mixers",
    )(p_nr, p_nr, p_plain, p_rope, p_rope, p_plain, lam_p, sub_g.reshape(1, DIF_DV))


WIN_SPAN = TQ + 2 * WINDOW
WIN_BLOCKS = WIN_HEADS // WIN_KV


def _window_kernel(sink_ref, q_ref, k_ref, v_ref, o_ref, *scratch, n_ctx, total):
    j = pl.program_id(1)
    start = pl.multiple_of(jnp.clip(j * TQ - WINDOW, 0, total - WIN_SPAN), LANES)
    streams = [scratch[3 * g:3 * (g + 1)] for g in range(WIN_BLOCKS)]
    for g, (qs_s, s_s, _) in enumerate(streams):
        qs_s[...] = _stack_halves(q_ref[0, :, g * LANES:(g + 1) * LANES])
        s_s[:, 0:n_ctx] = _dot_nt(qs_s[...], k_ref[0, 0:n_ctx, :])
        s_s[:, n_ctx:] = _dot_nt(qs_s[...], k_ref[0, pl.ds(start, WIN_SPAN), :])

    row = lax.broadcasted_iota(jnp.int32, (2 * TQ, WIN_SPAN), 0)
    qpos = j * TQ + jnp.where(row >= TQ, row - TQ, row)
    kpos = start + lax.broadcasted_iota(jnp.int32, (2 * TQ, WIN_SPAN), 1)
    valid = (kpos >= n_ctx) & (qpos >= n_ctx) & (jnp.abs(kpos - qpos) <= WINDOW)
    srow = lax.broadcasted_iota(jnp.int32, (2 * TQ, LANES), 0)
    lane = lax.broadcasted_iota(jnp.int32, (TQ, LANES), 1)
    width = n_ctx + WIN_SPAN
    for g, (_, s_s, p_s) in enumerate(streams):
        sink = jnp.where(srow < TQ, sink_ref[g], sink_ref[WIN_BLOCKS + g]) * LOG2E
        s_c = s_s[:, 0:n_ctx]
        s_w = jnp.where(valid, s_s[:, n_ctx:], NEG_INF)
        m = jnp.maximum(jnp.maximum(jnp.max(s_c, axis=1, keepdims=True), jnp.max(s_w, axis=1, keepdims=True)), sink)
        p_c = jnp.exp2(s_c - jnp.concatenate([m] * (n_ctx // LANES), axis=1))
        p_w = jnp.exp2(s_w - jnp.concatenate([m] * (WIN_SPAN // LANES), axis=1))
        p_s[:, 0:n_ctx] = p_c.astype(BF16)
        p_s[:, n_ctx:] = p_w.astype(BF16)
        psum = p_c[:, 0:LANES]
        for t in range(1, n_ctx // LANES):
            psum = psum + p_c[:, t * LANES:(t + 1) * LANES]
        for t in range(WIN_SPAN // LANES):
            psum = psum + p_w[:, t * LANES:(t + 1) * LANES]
        den = jnp.sum(psum, axis=1, keepdims=True) + jnp.exp2(sink - m)
        o = (_dot(p_s[:, 0:n_ctx], v_ref[0, 0:n_ctx, :])
             + _dot(p_s[:, n_ctx:width], v_ref[0, pl.ds(start, WIN_SPAN), :])) / den
        o_ref[0, :, g * LANES:(g + 1) * LANES] = jnp.where(lane < HEAD_DIM, o[0:TQ], o[TQ:2 * TQ]).astype(BF16)


def _window_mixer(p_rope, p_plain, sink, n_ctx):
    b, l, _ = p_rope.shape
    kern = functools.partial(_window_kernel, n_ctx=n_ctx, total=l)
    grid_spec = pltpu.PrefetchScalarGridSpec(
        num_scalar_prefetch=1,
        grid=(b, l // TQ),
        in_specs=[pl.BlockSpec((1, TQ, 512), lambda i, j, s: (i, j, 0)),
                  pl.BlockSpec((1, l, LANES), lambda i, j, s: (i, 0, 10)),
                  pl.BlockSpec((1, l, LANES), lambda i, j, s: (i, 0, PL_BV))],
        out_specs=pl.BlockSpec((1, TQ, 512), lambda i, j, s: (i, j, 0)),
        scratch_shapes=[pltpu.VMEM((2 * TQ, LANES), BF16), pltpu.VMEM((2 * TQ, n_ctx + WIN_SPAN), F32),
                        pltpu.VMEM((2 * TQ, n_ctx + WIN_SPAN), BF16)] * WIN_BLOCKS,
    )
    return pl.pallas_call(
        kern,
        grid_spec=grid_spec,
        out_shape=jax.ShapeDtypeStruct((b, l, BRANCH_W), BF16),
        compiler_params=_cparams(("parallel", "arbitrary")),
        name="window_mixer",
    )(sink, p_rope, p_rope, p_plain)


GLA_GROUP = 4


def _gla_kernel(q_ref, k_ref, v_ref, g_ref, wg_ref, bg_ref, gn_ref, o_ref, accf_s, accb_s, st_s, *, total, n_ctx):
    ck = GLA_CHUNK
    n_groups = total // (ck * GLA_GROUP)
    ctx_groups = n_ctx // (ck * GLA_GROUP)
    hw = GLA_HEADS * GLA_DK
    acc_refs = (accf_s, accb_s)

    st_s[...] = jnp.zeros_like(st_s)

    rr = lax.broadcasted_iota(jnp.int32, (ck, ck), 0)
    cc = lax.broadcasted_iota(jnp.int32, (ck, ck), 1)
    tri = (rr >= cc, rr <= cc)
    tri_b = tuple(jnp.where(t, 1.0, 0.0).astype(BF16) for t in tri)
    tri2 = tuple(jnp.concatenate([t, t], axis=0) for t in tri)
    lane = lax.broadcasted_iota(jnp.int32, (LANES, LANES), 1)

    def body(i, _):
        gf = i
        gb = jnp.where(i < ctx_groups, ctx_groups - 1 - i, n_groups - 1 - (i - ctx_groups))
        chains = []
        for u in range(GLA_GROUP):
            for d, grp in ((0, gf), (1, gb)):
                c = grp * GLA_GROUP + (u if d == 0 else GLA_GROUP - 1 - u)
                chains.append((d, pl.multiple_of(c * ck, ck)))

        gate = [_dot(g_ref[0, pl.ds(r0, ck), :], wg_ref[:, d * hw:(d + 1) * hw]) + bg_ref[:, d * hw:(d + 1) * hw]
                for d, r0 in chains]
        cums = []
        for (d, r0), y in zip(chains, gate):
            la = (jnp.minimum(y, 0.0) - jnp.log(1.0 + jnp.exp(-jnp.abs(y)))) * (1.0 / GLA_TAU)
            hi = la.astype(BF16)
            lo = (la - hi.astype(F32)).astype(BF16)
            cums.append(_dot(tri_b[d], hi) + _dot(tri_b[d], lo))

        prepped = []
        for (d, r0), cum in zip(chains, cums):
            tot = cum[ck - 1:ck, :] if d == 0 else cum[0:1, :]
            qf = q_ref[0, pl.ds(r0, ck), :].astype(F32)
            kf = k_ref[0, pl.ds(r0, ck), :].astype(F32)
            qt = (qf * jnp.exp(cum)).astype(BF16)
            kt = (kf * jnp.exp(-cum)).astype(BF16)
            kw = (kf * jnp.exp(tot - cum)).astype(BF16)
            qs, a2 = [], []
            for p in range(2):
                sl = slice(p * LANES, (p + 1) * LANES)
                qs.append(_stack_halves(qt[:, sl]))
                a2.append(jnp.where(tri2[d], _dot_nt(qs[p], kt[:, sl]), 0.0).astype(BF16))
            prepped.append((qs, a2, kw, jnp.exp(tot)))

        intra, update = [], []
        for (d, r0), (qs, a2, kw, dec) in zip(chains, prepped):
            av, upd = [], []
            for p in range(2):
                sl = slice(p * LANES, (p + 1) * LANES)
                halves = []
                for hh in range(2):
                    vh = v_ref[0, pl.ds(r0, ck), (2 * p + hh) * GLA_DV:(2 * p + hh + 1) * GLA_DV]
                    av.append(_dot(a2[p][hh * ck:(hh + 1) * ck], vh))
                    halves.append(_dot_tn(vh, kw[:, sl]))
                upd.append(jnp.where(lane < GLA_DK, halves[0], halves[1]))
            intra.append(av)
            update.append(upd)

        states = [[st_s[2 * d + p] for p in range(2)] for d in range(2)]
        for (d, r0), (qs, a2, kw, dec), av, upd in zip(chains, prepped, intra, update):
            for p in range(2):
                inter = _dot_nt(qs[p], states[d][p].astype(BF16))
                for hh in range(2):
                    vs = slice((2 * p + hh) * GLA_DV, (2 * p + hh + 1) * GLA_DV)
                    acc_refs[d][pl.ds(r0, ck), vs] = av[2 * p + hh] + inter[hh * ck:(hh + 1) * ck]
                states[d][p] = states[d][p] * dec[:, p * LANES:(p + 1) * LANES] + upd[p]
        for d in range(2):
            for p in range(2):
                st_s[2 * d + p] = states[d][p]
        return 0

    lax.fori_loop(0, n_groups, body, 0)

    def finish(i, _):
        r0 = pl.multiple_of(i * TQ, TQ)
        for h in range(GLA_HEADS):
            vs = slice(h * GLA_DV, (h + 1) * GLA_DV)
            o = accf_s[pl.ds(r0, TQ), vs] + accb_s[pl.ds(r0, TQ), vs]
            y = o * lax.rsqrt(jnp.mean(o * o, axis=-1, keepdims=True) + EPS) * gn_ref[:, vs]
            o_ref[0, pl.ds(r0, TQ), vs] = y.astype(BF16)
        return 0

    lax.fori_loop(0, total // TQ, finish, 0)


def _gla_mixer(p_plain, wg, bg, gn, n_ctx):
    b, l, _ = p_plain.shape
    kern = functools.partial(_gla_kernel, total=l, n_ctx=n_ctx)
    const = lambda i: (0, 0)
    return pl.pallas_call(
        kern,
        grid=(b,),
        in_specs=[pl.BlockSpec((1, l, 256), lambda i: (i, 0, PL_AQ)),
                  pl.BlockSpec((1, l, 256), lambda i: (i, 0, PL_AK)),
                  pl.BlockSpec((1, l, 512), lambda i: (i, 0, PL_AV)),
                  pl.BlockSpec((1, l, LANES), lambda i: (i, 0, PL_AG)),
                  pl.BlockSpec((LANES, 512), const),
                  pl.BlockSpec((1, 512), const),
                  pl.BlockSpec((1, 512), const)],
        out_specs=pl.BlockSpec((1, l, BRANCH_W), lambda i: (i, 0, 0)),
        out_shape=jax.ShapeDtypeStruct((b, l, BRANCH_W), BF16),
        scratch_shapes=[pltpu.VMEM((l, BRANCH_W), F32), pltpu.VMEM((l, BRANCH_W), F32),
                        pltpu.VMEM((4, GLA_DV, LANES), F32)],
        compiler_params=_cparams(("parallel",)),
        name="gla_mixer",
    )(p_plain, p_plain, p_plain, p_plain, wg, bg, gn)


def _merge_kernel(h_ref, oa_ref, ob_ref, oc_ref, od_ref, za_ref, zb_ref, zc_ref, zd_ref, *rest, alpha, off):
    *x_refs, mod_ref, wm_ref, wup_ref, wout_ref, lng_ref, lnb_ref, out_ref = rest
    h = h_ref[0]
    acc = None
    for i, (o_ref, z_ref) in enumerate(((oa_ref, za_ref), (ob_ref, zb_ref), (oc_ref, zc_ref), (od_ref, zd_ref))):
        z = z_ref[0].astype(F32)
        br = (o_ref[0].astype(F32) * (z * _sigmoid(z))).astype(BF16)
        term = _sigmoid(_dot(h, wm_ref[i])) * _dot(br, wup_ref[i])
        acc = term if acc is None else acc + term
    y = _dot(acc.astype(BF16), wout_ref[...])
    r = alpha * _stream_tile(x_refs, off) + mod_ref[0, 0, 2:3, :] * y
    mu = jnp.mean(r, axis=-1, keepdims=True)
    rc = r - mu
    var = jnp.mean(rc * rc, axis=-1, keepdims=True)
    out_ref[0] = rc * lax.rsqrt(var + EPS) * lng_ref[...] + lnb_ref[...]


def _merge(h, outs, p_plain, xs, mod, wm, wup, wout, ln_g, ln_b, layer, skip_ctx):
    b, l, d = h.shape
    off = 1 if skip_ctx else 0
    nt = l // TQ - off
    row = lambda i, j: (i, j + off, 0)
    zspec = lambda blk: pl.BlockSpec((1, TQ, BRANCH_W), lambda i, j: (i, j + off, blk))
    c2 = lambda i, j: (0, 0)
    in_specs = ([pl.BlockSpec((1, TQ, d), row)]
                + [pl.BlockSpec((1, TQ, BRANCH_W), row)] * 4
                + [zspec(PL_AZ), zspec(PL_BZ), zspec(PL_CZ), zspec(PL_DZ)]
                + _stream_specs(xs, d, off)
                + [pl.BlockSpec((1, 1, 3, d), lambda i, j: (i, jnp.minimum(j + off, 1), 0, 0)),
                   pl.BlockSpec((None, 4, d, d), lambda i, j: (layer, 0, 0, 0)),
                   pl.BlockSpec((None, 4, BRANCH_W, d), lambda i, j: (layer, 0, 0, 0)),
                   pl.BlockSpec((None, d, d), lambda i, j: (layer, 0, 0)),
                   pl.BlockSpec((1, d), c2), pl.BlockSpec((1, d), c2)])
    return pl.pallas_call(
        functools.partial(_merge_kernel, alpha=(2 * DEPTH) ** 0.25, off=off),
        grid=(b, nt),
        in_specs=in_specs,
        out_specs=pl.BlockSpec((1, TQ, d), lambda i, j: (i, j, 0)),
        out_shape=jax.ShapeDtypeStruct((b, nt * TQ, d), F32),
        compiler_params=_cparams(("parallel", "parallel")),
        name="merge",
    )(h, *outs, p_plain, p_plain, p_plain, p_plain, *xs, mod, wm, wup, wout,
      ln_g.reshape(1, d), ln_b.reshape(1, d))


def _perm_window(a, axis):
    a = jnp.moveaxis(a, axis, -1)
    lead = a.shape[:-1]
    a = a.reshape(lead + (WIN_KV, WIN_HEADS // WIN_KV, HEAD_DIM)).swapaxes(-3, -2).reshape(lead + (BRANCH_W,))
    return jnp.moveaxis(a, -1, axis)


def _perm_global(a, axis):
    a = jnp.moveaxis(a, axis, -1)
    lead = a.shape[:-1]
    a = a.reshape(lead + (GLB_KV // 2, 2, GLB_HEADS // GLB_KV, HEAD_DIM)).swapaxes(-3, -2).reshape(lead + (BRANCH_W,))
    return jnp.moveaxis(a, -1, axis)


def _split_w_in(w):
    cols = lambda off, n: w[..., off:off + n]
    pad = jnp.zeros(w.shape[:-1] + (LANES - 2 * GLA_RANK,), w.dtype)
    w_plain = jnp.concatenate([
        cols(_A_V, 512), cols(_A_Z, 512), _perm_window(cols(_B_Z, 512), -1), _perm_global(cols(_C_Z, 512), -1),
        cols(_D_Z, 512), cols(_A_Q, 256) * GLA_DK ** -0.5, cols(_A_K, 256), cols(_C_V, 256), cols(_D_V, 256),
        cols(_B_V, 128), cols(_A_GF, 2 * GLA_RANK), pad], axis=-1)
    w_rope = jnp.concatenate([_perm_window(cols(_B_Q, 512), -1) * HEAD_DIM ** -0.5, cols(_D_Q, 512) * HEAD_DIM ** -0.5,
                              cols(_D_K, 256), cols(_B_K, 128)], axis=-1)
    w_nr = jnp.concatenate([_perm_global(cols(_C_Q, 512), -1), cols(_C_K, 256)], axis=-1)
    return w_plain.astype(BF16), w_rope.astype(BF16), w_nr.astype(BF16)


def _rope_tables(n_ctx, n_lat):
    t = np.arange(n_lat)
    freqs = ROPE_BASE ** (-np.arange(ROPE_HALF, dtype=np.float32) / ROPE_HALF)
    pos = np.stack([(t // GRID_W).astype(np.float32), (t % GRID_W).astype(np.float32)], axis=1)
    ang = jnp.asarray(pos[:, :, None] * freqs[None, None, :], F32)
    cos, sin = jnp.cos(ang), jnp.sin(ang)
    cos_h = jnp.concatenate([cos, cos], axis=-1).reshape(n_lat, HEAD_DIM)
    sin_h = jnp.concatenate([-sin, sin], axis=-1).reshape(n_lat, HEAD_DIM)
    cos_t = jnp.concatenate([jnp.ones((n_ctx, HEAD_DIM), F32), cos_h], axis=0)
    sin_t = jnp.concatenate([jnp.zeros((n_ctx, HEAD_DIM), F32), sin_h], axis=0)
    return jnp.tile(cos_t, (1, 2)), jnp.tile(sin_t, (1, 2))


def kernel(x, c, ctx, c_ctx, w_ada, b_ada, w_in, gla_w_gate, gla_b_gate, gla_norm, win_sink, glb_q_norm,
           glb_k_norm, diff_lambda, diff_norm, w_merge, w_up, w_out, ln_g, ln_b):
    b, n_lat, d = x.shape
    n_ctx = ctx.shape[1]
    assert n_ctx == TQ and n_lat % TK == 0 and d == 1024
    xs = (ctx, x)
    cs = jnp.zeros((16, d), F32).at[0:b].set(c).at[b].set(c_ctx)
    cos_t, sin_t = _rope_tables(n_ctx, n_lat)
    seg = jnp.asarray(np.kron(np.eye(2, dtype=np.float32), np.ones((HEAD_DIM, HEAD_DIM), np.float32)), BF16)

    w_plain, w_rope, w_nr = _split_w_in(w_in)
    gain_nr = jnp.concatenate([jnp.tile(glb_q_norm * (HEAD_DIM ** -0.5 * LOG2E), (1, GLB_HEADS)),
                               jnp.tile(glb_k_norm, (1, GLB_KV))], axis=1)
    wg = jnp.zeros((DEPTH, LANES, 2 * GLA_HEADS * GLA_DK), F32)
    wg = wg.at[:, 0:GLA_RANK, 0:256].set(gla_w_gate[:, 0]).at[:, GLA_RANK:2 * GLA_RANK, 256:512].set(gla_w_gate[:, 1])
    wg = wg.astype(BF16)
    wup = jnp.stack([w_up[:, 0], _perm_window(w_up[:, 1], -2), _perm_global(w_up[:, 2], -2), w_up[:, 3]],
                    axis=1).astype(BF16)
    wm, wout = w_merge.astype(BF16), w_out.astype(BF16)

    ada = _ada(cs, w_ada, b_ada)
    mod_x = ada[:, 0:b].reshape(DEPTH, b, 1, 3, d)
    mod_c = jnp.broadcast_to(ada[:, b].reshape(DEPTH, 1, 1, 3, d), (DEPTH, b, 1, 3, d))
    mods = jnp.concatenate([mod_c, mod_x], axis=2)

    for layer in range(DEPTH):
        last = layer == DEPTH - 1
        lam_init = 0.8 - 0.6 * math.exp(-0.3 * layer)
        mod = mods[layer]
        h = _ln_mod(xs, mod)
        p_plain = _inproj(h, w_plain, layer, "plain")
        p_rope = _inproj(h, w_rope, layer, "rope", cos_t, sin_t)
        p_nr = _inproj(h, w_nr, layer, "nr", cos_t, sin_t, gain_nr[layer].reshape(1, NR_W), seg)

        o_a = _gla_mixer(p_plain, wg[layer], gla_b_gate[layer].reshape(1, 512), gla_norm[layer].reshape(1, BRANCH_W),
                         n_ctx)
        o_b = _window_mixer(p_rope, p_plain, win_sink[layer], n_ctx)
        o_c, o_d = _dense_mixers(p_nr, p_rope, p_plain, diff_lambda[layer], diff_norm[layer], lam_init, n_ctx)

        xs = (_merge(h, (o_a, o_b, o_c, o_d), p_plain, xs, mod, wm, wup, wout, ln_g[layer], ln_b[layer], layer,
                     skip_ctx=last),)
    return xs[0]
```

```python
import functools
import math

import numpy as np
import jax
import jax.numpy as jnp
from jax import lax
from jax.experimental import pallas as pl
from jax.experimental.pallas import tpu as pltpu

F32 = jnp.float32
BF16 = jnp.bfloat16

DEPTH = 2
GRID_W = 64
HEAD_DIM = 64
ROPE_HALF = HEAD_DIM // 4
ROPE_BASE = 10000.0
EPS = 1e-6
NEG_INF = -1e30
GLA_HEADS, GLA_DK, GLA_DV, GLA_RANK, GLA_TAU, GLA_CHUNK = 4, 64, 128, 16, 16.0, 64
WIN_HEADS, WIN_KV, WINDOW = 8, 2, 128
GLB_HEADS, GLB_KV = 8, 4
DIF_HEADS, DIF_KV, DIF_DV = 4, 2, 128
BRANCH_W = 512

LANES = 128
TQ = 256
TK = 512
VMEM_LIMIT = 56 << 20

_A_Q, _A_K, _A_V, _A_GF, _A_Z = 0, 256, 512, 1024, 1056
_B_Q, _B_K, _B_V, _B_Z = 1568, 2080, 2208, 2336
_C_Q, _C_K, _C_V, _C_Z = 2848, 3360, 3616, 3872
_D_Q, _D_K, _D_V, _D_Z = 4384, 4896, 5152, 5408

PL_AV, PL_AZ, PL_BZ, PL_CZ, PL_DZ = 0, 1, 2, 3, 4
PL_AQ, PL_AK, PL_CV, PL_DV = 10, 11, 12, 13
PL_BV, PL_AG = 28, 29
PLAIN_W, ROPE_W, NR_W = 3840, 1408, 768
ROPE_QW = 1024
LOG2E = math.log2(math.e)


def _cparams(sem):
    return pltpu.CompilerParams(dimension_semantics=sem, vmem_limit_bytes=VMEM_LIMIT)


def _sigmoid(x):
    return 1.0 / (1.0 + jnp.exp(-x))


def _dot(a, b):
    return jnp.dot(a, b, preferred_element_type=F32)


def _dot_nt(a, b):
    return lax.dot_general(a, b, (((1,), (1,)), ((), ())), preferred_element_type=F32)


def _dot_tn(a, b):
    return lax.dot_general(a, b, (((0,), (0,)), ((), ())), preferred_element_type=F32)


def _ada_kernel(c_ref, w_ref, b_ref, o_ref):
    c = c_ref[...]
    s = c * _sigmoid(c)
    o_ref[...] = jnp.dot(s, w_ref[...], preferred_element_type=F32,
                         precision=lax.Precision.HIGHEST) + b_ref[...]


def _ada(cs, w_ada, b_ada):
    r, d = cs.shape
    depth, _, n = w_ada.shape
    bn = 1024
    return pl.pallas_call(
        _ada_kernel,
        grid=(depth, n // bn),
        in_specs=[pl.BlockSpec((r, d), lambda l, j: (0, 0)),
                  pl.BlockSpec((None, d, bn), lambda l, j: (l, 0, j)),
                  pl.BlockSpec((None, 1, bn), lambda l, j: (l, 0, j))],
        out_specs=pl.BlockSpec((None, r, bn), lambda l, j: (l, 0, j)),
        out_shape=jax.ShapeDtypeStruct((depth, r, n), F32),
        compiler_params=_cparams(("arbitrary", "arbitrary")),
        name="ada",
    )(cs, w_ada, b_ada.reshape(depth, 1, n))


def _stream_specs(xs, d, off=0):
    if len(xs) == 1:
        return [pl.BlockSpec((1, TQ, d), lambda i, j: (i, j + off, 0))]
    return [pl.BlockSpec((1, TQ, d), lambda i, j: (i, 0, 0)),
            pl.BlockSpec((1, TQ, d), lambda i, j: (i, jnp.maximum(j + off - 1, 0), 0))]


def _stream_tile(refs, off=0):
    if len(refs) == 1:
        return refs[0][0]
    return jnp.where(pl.program_id(1) + off == 0, refs[0][0], refs[1][0])


def _ln_kernel(*refs):
    *x_refs, mod_ref, h_ref = refs
    x = _stream_tile(x_refs)
    mu = jnp.mean(x, axis=-1, keepdims=True)
    xc = x - mu
    var = jnp.mean(xc * xc, axis=-1, keepdims=True)
    y = xc * lax.rsqrt(var + EPS)
    shift = mod_ref[0, 0, 0:1, :]
    scale = mod_ref[0, 0, 1:2, :]
    h_ref[0] = (y * (1.0 + scale) + shift).astype(BF16)


def _ln_mod(xs, mod):
    b, _, d = xs[0].shape
    l = sum(a.shape[1] for a in xs)
    return pl.pallas_call(
        _ln_kernel,
        grid=(b, l // TQ),
        in_specs=_stream_specs(xs, d) + [pl.BlockSpec((1, 1, 3, d), lambda i, j: (i, jnp.minimum(j, 1), 0, 0))],
        out_specs=pl.BlockSpec((1, TQ, d), lambda i, j: (i, j, 0)),
        out_shape=jax.ShapeDtypeStruct((b, l, d), BF16),
        compiler_params=_cparams(("parallel", "parallel")),
        name="ln_mod",
    )(*xs, mod)


def _rope(y, cos, sin, lo16):
    sw = jnp.where(lo16, pltpu.roll(y, LANES - ROPE_HALF, 1), pltpu.roll(y, ROPE_HALF, 1))
    return y * cos + sw * sin


def _inproj_plain_kernel(h_ref, w_ref, o_ref):
    o_ref[0] = _dot(h_ref[0], w_ref[...]).astype(BF16)


def _inproj_rope_kernel(h_ref, w_ref, cos_ref, sin_ref, o_ref, *, width, q_width):
    acc = _dot(h_ref[0], w_ref[...])
    cos, sin = cos_ref[...], sin_ref[...]
    lo16 = (lax.broadcasted_iota(jnp.int32, cos.shape, 1) % (2 * ROPE_HALF)) < ROPE_HALF
    for c in range(width // LANES):
        y = _rope(acc[:, c * LANES:(c + 1) * LANES], cos, sin, lo16)
        if c * LANES < q_width:
            y = y * LOG2E
        o_ref[0, :, c * LANES:(c + 1) * LANES] = y.astype(BF16)


def _inproj_nr_kernel(h_ref, w_ref, cos_ref, sin_ref, gain_ref, seg_ref, o_ref, *, width):
    acc = _dot(h_ref[0], w_ref[...])
    cos, sin = cos_ref[...], sin_ref[...]
    lo16 = (lax.broadcasted_iota(jnp.int32, cos.shape, 1) % (2 * ROPE_HALF)) < ROPE_HALF
    for c in range(width // LANES):
        y = acc[:, c * LANES:(c + 1) * LANES]
        ss = _dot((y * y).astype(BF16), seg_ref[...])
        y = y * lax.rsqrt(ss * (1.0 / HEAD_DIM) + EPS) * gain_ref[:, c * LANES:(c + 1) * LANES]
        o_ref[0, :, c * LANES:(c + 1) * LANES] = _rope(y, cos, sin, lo16).astype(BF16)


def _inproj(h, w, layer, mode, cos=None, sin=None, gain=None, seg=None):
    b, l, d = h.shape
    n = w.shape[2]
    bm = l // 4
    if mode == "plain":
        bm, bn = l // 2, 768
        return pl.pallas_call(
            _inproj_plain_kernel,
            grid=(b, l // bm, n // bn),
            in_specs=[pl.BlockSpec((1, bm, d), lambda i, r, j: (i, r, 0)),
                      pl.BlockSpec((None, d, bn), lambda i, r, j: (layer, 0, j))],
            out_specs=pl.BlockSpec((1, bm, bn), lambda i, r, j: (i, r, j)),
            out_shape=jax.ShapeDtypeStruct((b, l, n), BF16),
            compiler_params=_cparams(("parallel", "parallel", "arbitrary")),
            name="inproj_plain",
        )(h, w)
    tab = pl.BlockSpec((bm, LANES), lambda i, r: (r, 0))
    in_specs = [pl.BlockSpec((1, bm, d), lambda i, r: (i, r, 0)),
                pl.BlockSpec((None, d, n), lambda i, r: (layer, 0, 0)), tab, tab]
    args = [h, w, cos, sin]
    if mode == "rope":
        body = functools.partial(_inproj_rope_kernel, width=n, q_width=ROPE_QW)
    else:
        body = functools.partial(_inproj_nr_kernel, width=n)
        in_specs += [pl.BlockSpec((1, n), lambda i, r: (0, 0)),
                     pl.BlockSpec((LANES, LANES), lambda i, r: (0, 0))]
        args += [gain, seg]
    return pl.pallas_call(
        body,
        grid=(b, l // bm),
        in_specs=in_specs,
        out_specs=pl.BlockSpec((1, bm, n), lambda i, r: (i, r, 0)),
        out_shape=jax.ShapeDtypeStruct((b, l, n), BF16),
        compiler_params=_cparams(("parallel", "parallel")),
        name="inproj_" + mode,
    )(*args)


def _stack_halves(q):
    lane = lax.broadcasted_iota(jnp.int32, q.shape, 1)
    zero = jnp.zeros_like(q)
    return jnp.concatenate([jnp.where(lane < HEAD_DIM, q, zero), jnp.where(lane >= HEAD_DIM, q, zero)], axis=0)


FLASH_ROWS = 2 * TQ


FLASH_REFS = 6


def _flash_scratch(n_streams):
    stat = pltpu.VMEM((FLASH_ROWS, LANES), F32)
    one = [pltpu.VMEM((FLASH_ROWS, LANES), BF16), stat, stat, stat,
           pltpu.VMEM((FLASH_ROWS, TK), F32), pltpu.VMEM((FLASH_ROWS, TK), BF16)]
    return one * n_streams


def _fold_chunk(stream, s, v, first):
    _, m_s, l_s, acc_s, _, p_s = stream
    tk = s.shape[1]
    m_new = jnp.broadcast_to(jnp.max(s, axis=1, keepdims=True), (FLASH_ROWS, LANES))
    if not first:
        m_old = m_s[...]
        m_new = jnp.maximum(m_old, m_new)
        alpha = jnp.exp2(m_old - m_new)
    p = jnp.exp2(s - jnp.concatenate([m_new] * (tk // LANES), axis=1))
    p_s[:, 0:tk] = p.astype(BF16)
    psum = p[:, 0:LANES]
    for t in range(1, tk // LANES):
        psum = psum + p[:, t * LANES:(t + 1) * LANES]
    pv = _dot(p_s[:, 0:tk], v)
    m_s[...] = m_new
    l_s[...] = psum if first else alpha * l_s[...] + psum
    acc_s[...] = pv if first else alpha * acc_s[...] + pv


def _flash(q_blocks, kv_srcs, n_ctx, n_lat_chunks, scratch):
    ns = len(q_blocks)
    streams = [scratch[FLASH_REFS * i:FLASH_REFS * (i + 1)] for i in range(ns)]
    kv_lanes = [slice(kb * LANES, (kb + 1) * LANES) for _, _, kb in kv_srcs]

    def scores(i, r0, tk):
        qs_s, _, _, _, s_s, _ = streams[i]
        s_s[:, 0:tk] = _dot_nt(qs_s[...], kv_srcs[i][0][0, pl.ds(r0, tk), kv_lanes[i]])

    def accumulate(i, r0, tk, first=False):
        s_s = streams[i][4]
        _fold_chunk(streams[i], s_s[:, 0:tk], kv_srcs[i][1][0, pl.ds(r0, tk), kv_lanes[i]], first)

    def stage(r0, tk, nxt, first=False):
        for i in range(ns):
            accumulate(i, r0, tk, first)
            if nxt is not None:
                scores(i, *nxt)

    def head():
        for i in range(ns):
            streams[i][0][...] = _stack_halves(q_blocks[i]())
            scores(i, 0, n_ctx)
        stage(0, n_ctx, (n_ctx, TK), first=True)

    def body(c, _):
        r0 = pl.multiple_of(n_ctx + c * TK, LANES)
        stage(r0, TK, (r0 + TK, TK))
        return 0

    def latent():
        lax.fori_loop(0, n_lat_chunks - 1, body, 0)
        stage(n_ctx + (n_lat_chunks - 1) * TK, TK, None)

    def results():
        return [acc_s[...] / jnp.sum(l_s[...], axis=1, keepdims=True) for _, _, l_s, acc_s, _, _ in streams]

    return head, latent, results


N_GLB_BLOCKS = GLB_HEADS // 2
N_DIF_BLOCKS = DIF_HEADS


def _dense_kernel(qc_ref, kc_ref, vc_ref, qd_ref, kd_ref, vd_ref, lam_ref, g_ref, oc_ref, od_ref, *scratch,
                  n_ctx, n_lat, n_tiles, lam_init):
    load = lambda ref, c: (lambda: ref[0, :, c * LANES:(c + 1) * LANES])
    q_blocks = [load(qc_ref, c) for c in range(N_GLB_BLOCKS)] + [load(qd_ref, c) for c in range(N_DIF_BLOCKS)]
    kv_srcs = ([(kc_ref, vc_ref, c // 2) for c in range(N_GLB_BLOCKS)]
               + [(kd_ref, vd_ref, c // 2) for c in range(N_DIF_BLOCKS)])
    head, latent, results = _flash(q_blocks, kv_srcs, n_ctx, n_lat // TK, scratch)

    def write_out():
        outs = results()
        lane = lax.broadcasted_iota(jnp.int32, (TQ, LANES), 1)
        for c, o in enumerate(outs[:N_GLB_BLOCKS]):
            oc_ref[0, :, c * LANES:(c + 1) * LANES] = jnp.where(lane < HEAD_DIM, o[0:TQ], o[TQ:2 * TQ]).astype(BF16)
        lp = lam_ref[...]
        lam = (jnp.exp(jnp.sum(lp[0:1] * lp[1:2], axis=1, keepdims=True))
               - jnp.exp(jnp.sum(lp[2:3] * lp[3:4], axis=1, keepdims=True)) + lam_init)
        for c, o in enumerate(outs[N_GLB_BLOCKS:]):
            o = o[0:TQ] - lam * o[TQ:2 * TQ]
            y = o * lax.rsqrt(jnp.mean(o * o, axis=-1, keepdims=True) + EPS) * g_ref[...]
            od_ref[0, :, c * LANES:(c + 1) * LANES] = (y * (1.0 - lam_init)).astype(BF16)

    j = pl.program_id(1)

    @pl.when(j == 0)
    def _():
        head()

    @pl.when((j > 0) & (j < n_tiles))
    def _():
        write_out()
        head()
        latent()

    @pl.when(j == n_tiles)
    def _():
        write_out()


def _dense_mixers(p_nr, p_rope, p_plain, lam_p, sub_g, lam_init, n_ctx):
    b, l, _ = p_nr.shape
    nt = l // TQ
    kern = functools.partial(_dense_kernel, n_ctx=n_ctx, n_lat=l - n_ctx, n_tiles=nt, lam_init=lam_init)
    tile = lambda blk: pl.BlockSpec((1, TQ, 512), lambda i, j: (i, jnp.minimum(j, nt - 1), blk))
    lagged = pl.BlockSpec((1, TQ, 512), lambda i, j: (i, jnp.maximum(j - 1, 0), 0))
    full = lambda blk: pl.BlockSpec((1, l, 256), lambda i, j: (i, 0, blk))
    out = jax.ShapeDtypeStruct((b, l, BRANCH_W), BF16)
    return pl.pallas_call(
        kern,
        grid=(b, nt + 1),
        in_specs=[tile(0), full(2), full(PL_CV),
                  tile(1), full(4), full(PL_DV),
                  pl.BlockSpec((4, HEAD_DIM), lambda i, j: (0, 0)),
                  pl.BlockSpec((1, DIF_DV), lambda i, j: (0, 0))],
        out_specs=[lagged, lagged],
        out_shape=[out, out],
        scratch_shapes=_flash_scratch(N_GLB_BLOCKS + N_DIF_BLOCKS),
        compiler_params=_cparams(("parallel", "arbitrary")),
        name="dense_mixers",
    )(p_nr, p_nr, p_plain, p_rope, p_rope, p_plain, lam_p, sub_g.reshape(1, DIF_DV))


WIN_SPAN = TQ + 2 * WINDOW
WIN_BLOCKS = WIN_HEADS // WIN_KV


def _window_kernel(sink_ref, q_ref, k_ref, v_ref, o_ref, *scratch, n_ctx, total):
    j = pl.program_id(1)
    start = pl.multiple_of(jnp.clip(j * TQ - WINDOW, 0, total - WIN_SPAN), LANES)
    streams = [scratch[3 * g:3 * (g + 1)] for g in range(WIN_BLOCKS)]
    for g, (qs_s, s_s, _) in enumerate(streams):
        qs_s[...] = _stack_halves(q_ref[0, :, g * LANES:(g + 1) * LANES])
        s_s[:, 0:n_ctx] = _dot_nt(qs_s[...], k_ref[0, 0:n_ctx, :])
        s_s[:, n_ctx:] = _dot_nt(qs_s[...], k_ref[0, pl.ds(start, WIN_SPAN), :])

    row = lax.broadcasted_iota(jnp.int32, (2 * TQ, WIN_SPAN), 0)
    qpos = j * TQ + jnp.where(row >= TQ, row - TQ, row)
    kpos = start + lax.broadcasted_iota(jnp.int32, (2 * TQ, WIN_SPAN), 1)
    valid = (kpos >= n_ctx) & (qpos >= n_ctx) & (jnp.abs(kpos - qpos) <= WINDOW)
    srow = lax.broadcasted_iota(jnp.int32, (2 * TQ, LANES), 0)
    lane = lax.broadcasted_iota(jnp.int32, (TQ, LANES), 1)
    width = n_ctx + WIN_SPAN
    for g, (_, s_s, p_s) in enumerate(streams):
        sink = jnp.where(srow < TQ, sink_ref[g], sink_ref[WIN_BLOCKS + g]) * LOG2E
        s_c = s_s[:, 0:n_ctx]
        s_w = jnp.where(valid, s_s[:, n_ctx:], NEG_INF)
        m = jnp.maximum(jnp.maximum(jnp.max(s_c, axis=1, keepdims=True), jnp.max(s_w, axis=1, keepdims=True)), sink)
        p_c = jnp.exp2(s_c - jnp.concatenate([m] * (n_ctx // LANES), axis=1))
        p_w = jnp.exp2(s_w - jnp.concatenate([m] * (WIN_SPAN // LANES), axis=1))
        p_s[:, 0:n_ctx] = p_c.astype(BF16)
        p_s[:, n_ctx:] = p_w.astype(BF16)
        psum = p_c[:, 0:LANES]
        for t in range(1, n_ctx // LANES):
            psum = psum + p_c[:, t * LANES:(t + 1) * LANES]
        for t in range(WIN_SPAN // LANES):
            psum = psum + p_w[:, t * LANES:(t + 1) * LANES]
        den = jnp.sum(psum, axis=1, keepdims=True) + jnp.exp2(sink - m)
        o = (_dot(p_s[:, 0:n_ctx], v_ref[0, 0:n_ctx, :])
             + _dot(p_s[:, n_ctx:width], v_ref[0, pl.ds(start, WIN_SPAN), :])) / den
        o_ref[0, :, g * LANES:(g + 1) * LANES] = jnp.where(lane < HEAD_DIM, o[0:TQ], o[TQ:2 * TQ]).astype(BF16)


def _window_mixer(p_rope, p_plain, sink, n_ctx):
    b, l, _ = p_rope.shape
    kern = functools.partial(_window_kernel, n_ctx=n_ctx, total=l)
    grid_spec = pltpu.PrefetchScalarGridSpec(
        num_scalar_prefetch=1,
        grid=(b, l // TQ),
        in_specs=[pl.BlockSpec((1, TQ, 512), lambda i, j, s: (i, j, 0)),
                  pl.BlockSpec((1, l, LANES), lambda i, j, s: (i, 0, 10)),
                  pl.BlockSpec((1, l, LANES), lambda i, j, s: (i, 0, PL_BV))],
        out_specs=pl.BlockSpec((1, TQ, 512), lambda i, j, s: (i, j, 0)),
        scratch_shapes=[pltpu.VMEM((2 * TQ, LANES), BF16), pltpu.VMEM((2 * TQ, n_ctx + WIN_SPAN), F32),
                        pltpu.VMEM((2 * TQ, n_ctx + WIN_SPAN), BF16)] * WIN_BLOCKS,
    )
    return pl.pallas_call(
        kern,
        grid_spec=grid_spec,
        out_shape=jax.ShapeDtypeStruct((b, l, BRANCH_W), BF16),
        compiler_params=_cparams(("parallel", "arbitrary")),
        name="window_mixer",
    )(sink, p_rope, p_rope, p_plain)


GLA_GROUP = 4


def _gla_kernel(q_ref, k_ref, v_ref, g_ref, wg_ref, bg_ref, gn_ref, o_ref, accf_s, accb_s, st_s, *, total, n_ctx):
    ck = GLA_CHUNK
    n_groups = total // (ck * GLA_GROUP)
    ctx_groups = n_ctx // (ck * GLA_GROUP)
    hw = GLA_HEADS * GLA_DK
    acc_refs = (accf_s, accb_s)

    st_s[...] = jnp.zeros_like(st_s)

    rr = lax.broadcasted_iota(jnp.int32, (ck, ck), 0)
    cc = lax.broadcasted_iota(jnp.int32, (ck, ck), 1)
    tri = (rr >= cc, rr <= cc)
    tri_b = tuple(jnp.where(t, 1.0, 0.0).astype(BF16) for t in tri)
    tri2 = tuple(jnp.concatenate([t, t], axis=0) for t in tri)
    lane = lax.broadcasted_iota(jnp.int32, (LANES, LANES), 1)

    def body(i, _):
        gf = i
        gb = jnp.where(i < ctx_groups, ctx_groups - 1 - i, n_groups - 1 - (i - ctx_groups))
        chains = []
        for u in range(GLA_GROUP):
            for d, grp in ((0, gf), (1, gb)):
                c = grp * GLA_GROUP + (u if d == 0 else GLA_GROUP - 1 - u)
                chains.append((d, pl.multiple_of(c * ck, ck)))

        gate = [_dot(g_ref[0, pl.ds(r0, ck), :], wg_ref[:, d * hw:(d + 1) * hw]) + bg_ref[:, d * hw:(d + 1) * hw]
                for d, r0 in chains]
        cums = []
        for (d, r0), y in zip(chains, gate):
            la = (jnp.minimum(y, 0.0) - jnp.log(1.0 + jnp.exp(-jnp.abs(y)))) * (1.0 / GLA_TAU)
            hi = la.astype(BF16)
            lo = (la - hi.astype(F32)).astype(BF16)
            cums.append(_dot(tri_b[d], hi) + _dot(tri_b[d], lo))

        prepped = []
        for (d, r0), cum in zip(chains, cums):
            tot = cum[ck - 1:ck, :] if d == 0 else cum[0:1, :]
            qf = q_ref[0, pl.ds(r0, ck), :].astype(F32)
            kf = k_ref[0, pl.ds(r0, ck), :].astype(F32)
            qt = (qf * jnp.exp(cum)).astype(BF16)
            kt = (kf * jnp.exp(-cum)).astype(BF16)
            kw = (kf * jnp.exp(tot - cum)).astype(BF16)
            qs, a2 = [], []
            for p in range(2):
                sl = slice(p * LANES, (p + 1) * LANES)
                qs.append(_stack_halves(qt[:, sl]))
                a2.append(jnp.where(tri2[d], _dot_nt(qs[p], kt[:, sl]), 0.0).astype(BF16))
            prepped.append((qs, a2, kw, jnp.exp(tot)))

        intra, update = [], []
        for (d, r0), (qs, a2, kw, dec) in zip(chains, prepped):
            av, upd = [], []
            for p in range(2):
                sl = slice(p * LANES, (p + 1) * LANES)
                halves = []
                for hh in range(2):
                    vh = v_ref[0, pl.ds(r0, ck), (2 * p + hh) * GLA_DV:(2 * p + hh + 1) * GLA_DV]
                    av.append(_dot(a2[p][hh * ck:(hh + 1) * ck], vh))
                    halves.append(_dot_tn(vh, kw[:, sl]))
                upd.append(jnp.where(lane < GLA_DK, halves[0], halves[1]))
            intra.append(av)
            update.append(upd)

        states = [[st_s[2 * d + p] for p in range(2)] for d in range(2)]
        for (d, r0), (qs, a2, kw, dec), av, upd in zip(chains, prepped, intra, update):
            for p in range(2):
                inter = _dot_nt(qs[p], states[d][p].astype(BF16))
                for hh in range(2):
                    vs = slice((2 * p + hh) * GLA_DV, (2 * p + hh + 1) * GLA_DV)
                    acc_refs[d][pl.ds(r0, ck), vs] = av[2 * p + hh] + inter[hh * ck:(hh + 1) * ck]
                states[d][p] = states[d][p] * dec[:, p * LANES:(p + 1) * LANES] + upd[p]
        for d in range(2):
            for p in range(2):
                st_s[2 * d + p] = states[d][p]
        return 0

    lax.fori_loop(0, n_groups, body, 0)

    def finish(i, _):
        r0 = pl.multiple_of(i * TQ, TQ)
        for h in range(GLA_HEADS):
            vs = slice(h * GLA_DV, (h + 1) * GLA_DV)
            o = accf_s[pl.ds(r0, TQ), vs] + accb_s[pl.ds(r0, TQ), vs]
            y = o * lax.rsqrt(jnp.mean(o * o, axis=-1, keepdims=True) + EPS) * gn_ref[:, vs]
            o_ref[0, pl.ds(r0, TQ), vs] = y.astype(BF16)
        return 0

    lax.fori_loop(0, total // TQ, finish, 0)


def _gla_mixer(p_plain, wg, bg, gn, n_ctx):
    b, l, _ = p_plain.shape
    kern = functools.partial(_gla_kernel, total=l, n_ctx=n_ctx)
    const = lambda i: (0, 0)
    return pl.pallas_call(
        kern,
        grid=(b,),
        in_specs=[pl.BlockSpec((1, l, 256), lambda i: (i, 0, PL_AQ)),
                  pl.BlockSpec((1, l, 256), lambda i: (i, 0, PL_AK)),
                  pl.BlockSpec((1, l, 512), lambda i: (i, 0, PL_AV)),
                  pl.BlockSpec((1, l, LANES), lambda i: (i, 0, PL_AG)),
                  pl.BlockSpec((LANES, 512), const),
                  pl.BlockSpec((1, 512), const),
                  pl.BlockSpec((1, 512), const)],
        out_specs=pl.BlockSpec((1, l, BRANCH_W), lambda i: (i, 0, 0)),
        out_shape=jax.ShapeDtypeStruct((b, l, BRANCH_W), BF16),
        scratch_shapes=[pltpu.VMEM((l, BRANCH_W), F32), pltpu.VMEM((l, BRANCH_W), F32),
                        pltpu.VMEM((4, GLA_DV, LANES), F32)],
        compiler_params=_cparams(("parallel",)),
        name="gla_mixer",
    )(p_plain, p_plain, p_plain, p_plain, wg, bg, gn)


def _merge_kernel(h_ref, oa_ref, ob_ref, oc_ref, od_ref, za_ref, zb_ref, zc_ref, zd_ref, *rest, alpha, off):
    *x_refs, mod_ref, wm_ref, wup_ref, wout_ref, lng_ref, lnb_ref, out_ref = rest
    h = h_ref[0]
    acc = None
    for i, (o_ref, z_ref) in enumerate(((oa_ref, za_ref), (ob_ref, zb_ref), (oc_ref, zc_ref), (od_ref, zd_ref))):
        z = z_ref[0].astype(F32)
        br = (o_ref[0].astype(F32) * (z * _sigmoid(z))).astype(BF16)
        term = _sigmoid(_dot(h, wm_ref[i])) * _dot(br, wup_ref[i])
        acc = term if acc is None else acc + term
    y = _dot(acc.astype(BF16), wout_ref[...])
    r = alpha * _stream_tile(x_refs, off) + mod_ref[0, 0, 2:3, :] * y
    mu = jnp.mean(r, axis=-1, keepdims=True)
    rc = r - mu
    var = jnp.mean(rc * rc, axis=-1, keepdims=True)
    out_ref[0] = rc * lax.rsqrt(var + EPS) * lng_ref[...] + lnb_ref[...]


def _merge(h, outs, p_plain, xs, mod, wm, wup, wout, ln_g, ln_b, layer, skip_ctx):
    b, l, d = h.shape
    off = 1 if skip_ctx else 0
    nt = l // TQ - off
    row = lambda i, j: (i, j + off, 0)
    zspec = lambda blk: pl.BlockSpec((1, TQ, BRANCH_W), lambda i, j: (i, j + off, blk))
    c2 = lambda i, j: (0, 0)
    in_specs = ([pl.BlockSpec((1, TQ, d), row)]
                + [pl.BlockSpec((1, TQ, BRANCH_W), row)] * 4
                + [zspec(PL_AZ), zspec(PL_BZ), zspec(PL_CZ), zspec(PL_DZ)]
                + _stream_specs(xs, d, off)
                + [pl.BlockSpec((1, 1, 3, d), lambda i, j: (i, jnp.minimum(j + off, 1), 0, 0)),
                   pl.BlockSpec((None, 4, d, d), lambda i, j: (layer, 0, 0, 0)),
                   pl.BlockSpec((None, 4, BRANCH_W, d), lambda i, j: (layer, 0, 0, 0)),
                   pl.BlockSpec((None, d, d), lambda i, j: (layer, 0, 0)),
                   pl.BlockSpec((1, d), c2), pl.BlockSpec((1, d), c2)])
    return pl.pallas_call(
        functools.partial(_merge_kernel, alpha=(2 * DEPTH) ** 0.25, off=off),
        grid=(b, nt),
        in_specs=in_specs,
        out_specs=pl.BlockSpec((1, TQ, d), lambda i, j: (i, j, 0)),
        out_shape=jax.ShapeDtypeStruct((b, nt * TQ, d), F32),
        compiler_params=_cparams(("parallel", "parallel")),
        name="merge",
    )(h, *outs, p_plain, p_plain, p_plain, p_plain, *xs, mod, wm, wup, wout,
      ln_g.reshape(1, d), ln_b.reshape(1, d))


def _perm_window(a, axis):
    a = jnp.moveaxis(a, axis, -1)
    lead = a.shape[:-1]
    a = a.reshape(lead + (WIN_KV, WIN_HEADS // WIN_KV, HEAD_DIM)).swapaxes(-3, -2).reshape(lead + (BRANCH_W,))
    return jnp.moveaxis(a, -1, axis)


def _perm_global(a, axis):
    a = jnp.moveaxis(a, axis, -1)
    lead = a.shape[:-1]
    a = a.reshape(lead + (GLB_KV // 2, 2, GLB_HEADS // GLB_KV, HEAD_DIM)).swapaxes(-3, -2).reshape(lead + (BRANCH_W,))
    return jnp.moveaxis(a, -1, axis)


def _split_w_in(w):
    cols = lambda off, n: w[..., off:off + n]
    pad = jnp.zeros(w.shape[:-1] + (LANES - 2 * GLA_RANK,), w.dtype)
    w_plain = jnp.concatenate([
        cols(_A_V, 512), cols(_A_Z, 512), _perm_window(cols(_B_Z, 512), -1), _perm_global(cols(_C_Z, 512), -1),
        cols(_D_Z, 512), cols(_A_Q, 256) * GLA_DK ** -0.5, cols(_A_K, 256), cols(_C_V, 256), cols(_D_V, 256),
        cols(_B_V, 128), cols(_A_GF, 2 * GLA_RANK), pad], axis=-1)
    w_rope = jnp.concatenate([_perm_window(cols(_B_Q, 512), -1) * HEAD_DIM ** -0.5, cols(_D_Q, 512) * HEAD_DIM ** -0.5,
                              cols(_D_K, 256), cols(_B_K, 128)], axis=-1)
    w_nr = jnp.concatenate([_perm_global(cols(_C_Q, 512), -1), cols(_C_K, 256)], axis=-1)
    return w_plain.astype(BF16), w_rope.astype(BF16), w_nr.astype(BF16)


def _rope_tables(n_ctx, n_lat):
    t = np.arange(n_lat)
    freqs = ROPE_BASE ** (-np.arange(ROPE_HALF, dtype=np.float32) / ROPE_HALF)
    pos = np.stack([(t // GRID_W).astype(np.float32), (t % GRID_W).astype(np.float32)], axis=1)
    ang = jnp.asarray(pos[:, :, None] * freqs[None, None, :], F32)
    cos, sin = jnp.cos(ang), jnp.sin(ang)
    cos_h = jnp.concatenate([cos, cos], axis=-1).reshape(n_lat, HEAD_DIM)
    sin_h = jnp.concatenate([-sin, sin], axis=-1).reshape(n_lat, HEAD_DIM)
    cos_t = jnp.concatenate([jnp.ones((n_ctx, HEAD_DIM), F32), cos_h], axis=0)
    sin_t = jnp.concatenate([jnp.zeros((n_ctx, HEAD_DIM), F32), sin_h], axis=0)
    return jnp.tile(cos_t, (1, 2)), jnp.tile(sin_t, (1, 2))


def kernel(x, c, ctx, c_ctx, w_ada, b_ada, w_in, gla_w_gate, gla_b_gate, gla_norm, win_sink, glb_q_norm,
           glb_k_norm, diff_lambda, diff_norm, w_merge, w_up, w_out, ln_g, ln_b):
    b, n_lat, d = x.shape
    n_ctx = ctx.shape[1]
    assert n_ctx == TQ and n_lat % TK == 0 and d == 1024
    xs = (ctx, x)
    cs = jnp.zeros((16, d), F32).at[0:b].set(c).at[b].set(c_ctx)
    cos_t, sin_t = _rope_tables(n_ctx, n_lat)
    seg = jnp.asarray(np.kron(np.eye(2, dtype=np.float32), np.ones((HEAD_DIM, HEAD_DIM), np.float32)), BF16)

    w_plain, w_rope, w_nr = _split_w_in(w_in)
    gain_nr = jnp.concatenate([jnp.tile(glb_q_norm * (HEAD_DIM ** -0.5 * LOG2E), (1, GLB_HEADS)),
                               jnp.tile(glb_k_norm, (1, GLB_KV))], axis=1)
    wg = jnp.zeros((DEPTH, LANES, 2 * GLA_HEADS * GLA_DK), F32)
    wg = wg.at[:, 0:GLA_RANK, 0:256].set(gla_w_gate[:, 0]).at[:, GLA_RANK:2 * GLA_RANK, 256:512].set(gla_w_gate[:, 1])
    wg = wg.astype(BF16)
    wup = jnp.stack([w_up[:, 0], _perm_window(w_up[:, 1], -2), _perm_global(w_up[:, 2], -2), w_up[:, 3]],
                    axis=1).astype(BF16)
    wm, wout = w_merge.astype(BF16), w_out.astype(BF16)

    ada = _ada(cs, w_ada, b_ada)
    mod_x = ada[:, 0:b].reshape(DEPTH, b, 1, 3, d)
    mod_c = jnp.broadcast_to(ada[:, b].reshape(DEPTH, 1, 1, 3, d), (DEPTH, b, 1, 3, d))
    mods = jnp.concatenate([mod_c, mod_x], axis=2)

    for layer in range(DEPTH):
        last = layer == DEPTH - 1
        lam_init = 0.8 - 0.6 * math.exp(-0.3 * layer)
        mod = mods[layer]
        h = _ln_mod(xs, mod)
        p_plain = _inproj(h, w_plain, layer, "plain")
        p_rope = _inproj(h, w_rope, layer, "rope", cos_t, sin_t)
        p_nr = _inproj(h, w_nr, layer, "nr", cos_t, sin_t, gain_nr[layer].reshape(1, NR_W), seg)

        o_a = _gla_mixer(p_plain, wg[layer], gla_b_gate[layer].reshape(1, 512), gla_norm[layer].reshape(1, BRANCH_W),
                         n_ctx)
        o_b = _window_mixer(p_rope, p_plain, win_sink[layer], n_ctx)
        o_c, o_d = _dense_mixers(p_nr, p_rope, p_plain, diff_lambda[layer], diff_norm[layer], lam_init, n_ctx)

        xs = (_merge(h, (o_a, o_b, o_c, o_d), p_plain, xs, mod, wm, wup, wout, ln_g[layer], ln_b[layer], layer,
                     skip_ctx=last),)
    return xs[0]
```

```python
import functools
import math

import numpy as np
import jax
import jax.numpy as jnp
from jax import lax
from jax.experimental import pallas as pl
from jax.experimental.pallas import tpu as pltpu

F32 = jnp.float32
BF16 = jnp.bfloat16

DEPTH = 2
GRID_W = 64
HEAD_DIM = 64
ROPE_HALF = HEAD_DIM // 4
ROPE_BASE = 10000.0
EPS = 1e-6
NEG_INF = -1e30
GLA_HEADS, GLA_DK, GLA_DV, GLA_RANK, GLA_TAU, GLA_CHUNK = 4, 64, 128, 16, 16.0, 64
WIN_HEADS, WIN_KV, WINDOW = 8, 2, 128
GLB_HEADS, GLB_KV = 8, 4
DIF_HEADS, DIF_KV, DIF_DV = 4, 2, 128
BRANCH_W = 512

LANES = 128
TQ = 256
TK = 512
VMEM_LIMIT = 56 << 20

_A_Q, _A_K, _A_V, _A_GF, _A_Z = 0, 256, 512, 1024, 1056
_B_Q, _B_K, _B_V, _B_Z = 1568, 2080, 2208, 2336
_C_Q, _C_K, _C_V, _C_Z = 2848, 3360, 3616, 3872
_D_Q, _D_K, _D_V, _D_Z = 4384, 4896, 5152, 5408

PL_AV, PL_AZ, PL_BZ, PL_CZ, PL_DZ = 0, 1, 2, 3, 4
PL_AQ, PL_AK, PL_CV, PL_DV = 10, 11, 12, 13
PL_BV, PL_AG = 28, 29
PLAIN_W, ROPE_W, NR_W = 3840, 1408, 768
ROPE_QW = 1024
LOG2E = math.log2(math.e)


def _cparams(sem):
    return pltpu.CompilerParams(dimension_semantics=sem, vmem_limit_bytes=VMEM_LIMIT)


def _sigmoid(x):
    return 1.0 / (1.0 + jnp.exp(-x))


def _dot(a, b):
    return jnp.dot(a, b, preferred_element_type=F32)


def _dot_nt(a, b):
    return lax.dot_general(a, b, (((1,), (1,)), ((), ())), preferred_element_type=F32)


def _dot_tn(a, b):
    return lax.dot_general(a, b, (((0,), (0,)), ((), ())), preferred_element_type=F32)


def _ada_kernel(c_ref, w_ref, b_ref, o_ref):
    c = c_ref[...]
    s = c * _sigmoid(c)
    o_ref[...] = jnp.dot(s, w_ref[...], preferred_element_type=F32,
                         precision=lax.Precision.HIGHEST) + b_ref[...]


def _ada(cs, w_ada, b_ada):
    r, d = cs.shape
    depth, _, n = w_ada.shape
    bn = 1024
    return pl.pallas_call(
        _ada_kernel,
        grid=(depth, n // bn),
        in_specs=[pl.BlockSpec((r, d), lambda l, j: (0, 0)),
                  pl.BlockSpec((None, d, bn), lambda l, j: (l, 0, j)),
                  pl.BlockSpec((None, 1, bn), lambda l, j: (l, 0, j))],
        out_specs=pl.BlockSpec((None, r, bn), lambda l, j: (l, 0, j)),
        out_shape=jax.ShapeDtypeStruct((depth, r, n), F32),
        compiler_params=_cparams(("arbitrary", "arbitrary")),
        name="ada",
    )(cs, w_ada, b_ada.reshape(depth, 1, n))


def _stream_specs(xs, d, tile_of=lambda j: j):
    if len(xs) == 1:
        return [pl.BlockSpec((1, TQ, d), lambda i, j: (i, tile_of(j), 0))]
    return [pl.BlockSpec((1, TQ, d), lambda i, j: (i, 0, 0)),
            pl.BlockSpec((1, TQ, d), lambda i, j: (i, jnp.maximum(tile_of(j) - 1, 0), 0))]


def _stream_tile(refs, tile):
    if len(refs) == 1:
        return refs[0][0]
    return jnp.where(tile == 0, refs[0][0], refs[1][0])


def _ln_kernel(*refs):
    *x_refs, mod_ref, h_ref = refs
    x = _stream_tile(x_refs, pl.program_id(1))
    mu = jnp.mean(x, axis=-1, keepdims=True)
    xc = x - mu
    var = jnp.mean(xc * xc, axis=-1, keepdims=True)
    y = xc * lax.rsqrt(var + EPS)
    shift = mod_ref[0, 0, 0:1, :]
    scale = mod_ref[0, 0, 1:2, :]
    h_ref[0] = (y * (1.0 + scale) + shift).astype(BF16)


def _ln_mod(xs, mod):
    b, _, d = xs[0].shape
    l = sum(a.shape[1] for a in xs)
    return pl.pallas_call(
        _ln_kernel,
        grid=(b, l // TQ),
        in_specs=_stream_specs(xs, d) + [pl.BlockSpec((1, 1, 3, d), lambda i, j: (i, jnp.minimum(j, 1), 0, 0))],
        out_specs=pl.BlockSpec((1, TQ, d), lambda i, j: (i, j, 0)),
        out_shape=jax.ShapeDtypeStruct((b, l, d), BF16),
        compiler_params=_cparams(("parallel", "parallel")),
        name="ln_mod",
    )(*xs, mod)


def _rope(y, cos, sin, lo16):
    sw = jnp.where(lo16, pltpu.roll(y, LANES - ROPE_HALF, 1), pltpu.roll(y, ROPE_HALF, 1))
    return y * cos + sw * sin


def _inproj_plain_kernel(h_ref, w_ref, o_ref):
    o_ref[0] = _dot(h_ref[0], w_ref[...]).astype(BF16)


def _inproj_rope_kernel(h_ref, w_ref, cos_ref, sin_ref, o_ref, *, width, q_width):
    acc = _dot(h_ref[0], w_ref[...])
    cos, sin = cos_ref[...], sin_ref[...]
    lo16 = (lax.broadcasted_iota(jnp.int32, cos.shape, 1) % (2 * ROPE_HALF)) < ROPE_HALF
    for c in range(width // LANES):
        y = _rope(acc[:, c * LANES:(c + 1) * LANES], cos, sin, lo16)
        if c * LANES < q_width:
            y = y * LOG2E
        o_ref[0, :, c * LANES:(c + 1) * LANES] = y.astype(BF16)


def _inproj_nr_kernel(h_ref, w_ref, cos_ref, sin_ref, gain_ref, seg_ref, o_ref, *, width):
    acc = _dot(h_ref[0], w_ref[...])
    cos, sin = cos_ref[...], sin_ref[...]
    lo16 = (lax.broadcasted_iota(jnp.int32, cos.shape, 1) % (2 * ROPE_HALF)) < ROPE_HALF
    for c in range(width // LANES):
        y = acc[:, c * LANES:(c + 1) * LANES]
        ss = _dot((y * y).astype(BF16), seg_ref[...])
        y = y * lax.rsqrt(ss * (1.0 / HEAD_DIM) + EPS) * gain_ref[:, c * LANES:(c + 1) * LANES]
        o_ref[0, :, c * LANES:(c + 1) * LANES] = _rope(y, cos, sin, lo16).astype(BF16)


def _inproj(h, w, layer, mode, cos=None, sin=None, gain=None, seg=None):
    b, l, d = h.shape
    n = w.shape[2]
    bm = l // 4
    if mode == "plain":
        bm, bn = l // 2, 768
        return pl.pallas_call(
            _inproj_plain_kernel,
            grid=(b, l // bm, n // bn),
            in_specs=[pl.BlockSpec((1, bm, d), lambda i, r, j: (i, r, 0)),
                      pl.BlockSpec((None, d, bn), lambda i, r, j: (layer, 0, j))],
            out_specs=pl.BlockSpec((1, bm, bn), lambda i, r, j: (i, r, j)),
            out_shape=jax.ShapeDtypeStruct((b, l, n), BF16),
            compiler_params=_cparams(("parallel", "parallel", "arbitrary")),
            name="inproj_plain",
        )(h, w)
    tab = pl.BlockSpec((bm, LANES), lambda i, r: (r, 0))
    in_specs = [pl.BlockSpec((1, bm, d), lambda i, r: (i, r, 0)),
                pl.BlockSpec((None, d, n), lambda i, r: (layer, 0, 0)), tab, tab]
    args = [h, w, cos, sin]
    if mode == "rope":
        body = functools.partial(_inproj_rope_kernel, width=n, q_width=ROPE_QW)
    else:
        body = functools.partial(_inproj_nr_kernel, width=n)
        in_specs += [pl.BlockSpec((1, n), lambda i, r: (0, 0)),
                     pl.BlockSpec((LANES, LANES), lambda i, r: (0, 0))]
        args += [gain, seg]
    return pl.pallas_call(
        body,
        grid=(b, l // bm),
        in_specs=in_specs,
        out_specs=pl.BlockSpec((1, bm, n), lambda i, r: (i, r, 0)),
        out_shape=jax.ShapeDtypeStruct((b, l, n), BF16),
        compiler_params=_cparams(("parallel", "parallel")),
        name="inproj_" + mode,
    )(*args)


def _stack_halves(q):
    lane = lax.broadcasted_iota(jnp.int32, q.shape, 1)
    zero = jnp.zeros_like(q)
    return jnp.concatenate([jnp.where(lane < HEAD_DIM, q, zero), jnp.where(lane >= HEAD_DIM, q, zero)], axis=0)


FLASH_ROWS = 2 * TQ


FLASH_REFS = 6


def _flash_scratch(n_streams):
    stat = pltpu.VMEM((FLASH_ROWS, LANES), F32)
    one = [pltpu.VMEM((FLASH_ROWS, LANES), BF16), stat, stat, stat,
           pltpu.VMEM((FLASH_ROWS, TK), F32), pltpu.VMEM((FLASH_ROWS, TK), BF16)]
    return one * n_streams


def _fold_chunk(stream, s, v, first):
    _, m_s, l_s, acc_s, _, p_s = stream
    tk = s.shape[1]
    m_new = jnp.broadcast_to(jnp.max(s, axis=1, keepdims=True), (FLASH_ROWS, LANES))
    if not first:
        m_old = m_s[...]
        m_new = jnp.maximum(m_old, m_new)
        alpha = jnp.exp2(m_old - m_new)
    p = jnp.exp2(s - jnp.concatenate([m_new] * (tk // LANES), axis=1))
    p_s[:, 0:tk] = p.astype(BF16)
    psum = p[:, 0:LANES]
    for t in range(1, tk // LANES):
        psum = psum + p[:, t * LANES:(t + 1) * LANES]
    pv = _dot(p_s[:, 0:tk], v)
    m_s[...] = m_new
    l_s[...] = psum if first else alpha * l_s[...] + psum
    acc_s[...] = pv if first else alpha * acc_s[...] + pv


def _flash(q_blocks, kv_srcs, n_ctx, n_lat_chunks, scratch):
    ns = len(q_blocks)
    streams = [scratch[FLASH_REFS * i:FLASH_REFS * (i + 1)] for i in range(ns)]
    kv_lanes = [slice(kb * LANES, (kb + 1) * LANES) for _, _, kb in kv_srcs]

    def scores(i, r0, tk):
        qs_s, _, _, _, s_s, _ = streams[i]
        s_s[:, 0:tk] = _dot_nt(qs_s[...], kv_srcs[i][0][0, pl.ds(r0, tk), kv_lanes[i]])

    def accumulate(i, r0, tk, first=False):
        s_s = streams[i][4]
        _fold_chunk(streams[i], s_s[:, 0:tk], kv_srcs[i][1][0, pl.ds(r0, tk), kv_lanes[i]], first)

    def stage(r0, tk, nxt, first=False):
        for i in range(ns):
            accumulate(i, r0, tk, first)
            if nxt is not None:
                scores(i, *nxt)

    def head():
        for i in range(ns):
            streams[i][0][...] = _stack_halves(q_blocks[i]())
            scores(i, 0, n_ctx)
        stage(0, n_ctx, (n_ctx, TK), first=True)

    def body(c, _):
        r0 = pl.multiple_of(n_ctx + c * TK, LANES)
        stage(r0, TK, (r0 + TK, TK))
        return 0

    def latent():
        lax.fori_loop(0, n_lat_chunks - 1, body, 0)
        stage(n_ctx + (n_lat_chunks - 1) * TK, TK, None)

    def results():
        return [acc_s[...] / jnp.sum(l_s[...], axis=1, keepdims=True) for _, _, l_s, acc_s, _, _ in streams]

    return head, latent, results


N_GLB_BLOCKS = GLB_HEADS // 2
N_DIF_BLOCKS = DIF_HEADS


def _dense_kernel(qc_ref, kc_ref, vc_ref, qd_ref, kd_ref, vd_ref, lam_ref, g_ref, oc_ref, od_ref, *scratch,
                  n_ctx, n_lat, n_tiles, lam_init):
    load = lambda ref, c: (lambda: ref[0, :, c * LANES:(c + 1) * LANES])
    q_blocks = [load(qc_ref, c) for c in range(N_GLB_BLOCKS)] + [load(qd_ref, c) for c in range(N_DIF_BLOCKS)]
    kv_srcs = ([(kc_ref, vc_ref, c // 2) for c in range(N_GLB_BLOCKS)]
               + [(kd_ref, vd_ref, c // 2) for c in range(N_DIF_BLOCKS)])
    head, latent, results = _flash(q_blocks, kv_srcs, n_ctx, n_lat // TK, scratch)

    def write_out():
        outs = results()
        lane = lax.broadcasted_iota(jnp.int32, (TQ, LANES), 1)
        for c, o in enumerate(outs[:N_GLB_BLOCKS]):
            oc_ref[0, :, c * LANES:(c + 1) * LANES] = jnp.where(lane < HEAD_DIM, o[0:TQ], o[TQ:2 * TQ]).astype(BF16)
        lp = lam_ref[...]
        lam = (jnp.exp(jnp.sum(lp[0:1] * lp[1:2], axis=1, keepdims=True))
               - jnp.exp(jnp.sum(lp[2:3] * lp[3:4], axis=1, keepdims=True)) + lam_init)
        for c, o in enumerate(outs[N_GLB_BLOCKS:]):
            o = o[0:TQ] - lam * o[TQ:2 * TQ]
            y = o * lax.rsqrt(jnp.mean(o * o, axis=-1, keepdims=True) + EPS) * g_ref[...]
            od_ref[0, :, c * LANES:(c + 1) * LANES] = (y * (1.0 - lam_init)).astype(BF16)

    j = pl.program_id(1)

    @pl.when(j == 0)
    def _():
        head()

    @pl.when((j > 0) & (j < n_tiles))
    def _():
        write_out()
        head()
        latent()

    @pl.when(j == n_tiles)
    def _():
        write_out()


def _dense_mixers(p_nr, p_rope, p_plain, lam_p, sub_g, lam_init, n_ctx):
    b, l, _ = p_nr.shape
    nt = l // TQ
    kern = functools.partial(_dense_kernel, n_ctx=n_ctx, n_lat=l - n_ctx, n_tiles=nt, lam_init=lam_init)
    tile = lambda blk: pl.BlockSpec((1, TQ, 512), lambda i, j: (i, jnp.minimum(j, nt - 1), blk))
    lagged = pl.BlockSpec((1, TQ, 512), lambda i, j: (i, jnp.maximum(j - 1, 0), 0))
    full = lambda blk: pl.BlockSpec((1, l, 256), lambda i, j: (i, 0, blk))
    out = jax.ShapeDtypeStruct((b, l, BRANCH_W), BF16)
    return pl.pallas_call(
        kern,
        grid=(b, nt + 1),
        in_specs=[tile(0), full(2), full(PL_CV),
                  tile(1), full(4), full(PL_DV),
                  pl.BlockSpec((4, HEAD_DIM), lambda i, j: (0, 0)),
                  pl.BlockSpec((1, DIF_DV), lambda i, j: (0, 0))],
        out_specs=[lagged, lagged],
        out_shape=[out, out],
        scratch_shapes=_flash_scratch(N_GLB_BLOCKS + N_DIF_BLOCKS),
        compiler_params=_cparams(("parallel", "arbitrary")),
        name="dense_mixers",
    )(p_nr, p_nr, p_plain, p_rope, p_rope, p_plain, lam_p, sub_g.reshape(1, DIF_DV))


WIN_SPAN = TQ + 2 * WINDOW
WIN_BLOCKS = WIN_HEADS // WIN_KV


def _window_kernel(sink_ref, q_ref, k_ref, v_ref, o_ref, *scratch, n_ctx, total):
    j = pl.program_id(1)
    start = pl.multiple_of(jnp.clip(j * TQ - WINDOW, 0, total - WIN_SPAN), LANES)
    streams = [scratch[3 * g:3 * (g + 1)] for g in range(WIN_BLOCKS)]
    for g, (qs_s, s_s, _) in enumerate(streams):
        qs_s[...] = _stack_halves(q_ref[0, :, g * LANES:(g + 1) * LANES])
        s_s[:, 0:n_ctx] = _dot_nt(qs_s[...], k_ref[0, 0:n_ctx, :])
        s_s[:, n_ctx:] = _dot_nt(qs_s[...], k_ref[0, pl.ds(start, WIN_SPAN), :])

    row = lax.broadcasted_iota(jnp.int32, (2 * TQ, WIN_SPAN), 0)
    qpos = j * TQ + jnp.where(row >= TQ, row - TQ, row)
    kpos = start + lax.broadcasted_iota(jnp.int32, (2 * TQ, WIN_SPAN), 1)
    valid = (kpos >= n_ctx) & (qpos >= n_ctx) & (jnp.abs(kpos - qpos) <= WINDOW)
    srow = lax.broadcasted_iota(jnp.int32, (2 * TQ, LANES), 0)
    lane = lax.broadcasted_iota(jnp.int32, (TQ, LANES), 1)
    width = n_ctx + WIN_SPAN
    for g, (_, s_s, p_s) in enumerate(streams):
        sink = jnp.where(srow < TQ, sink_ref[g], sink_ref[WIN_BLOCKS + g]) * LOG2E
        s_c = s_s[:, 0:n_ctx]
        s_w = jnp.where(valid, s_s[:, n_ctx:], NEG_INF)
        m = jnp.maximum(jnp.maximum(jnp.max(s_c, axis=1, keepdims=True), jnp.max(s_w, axis=1, keepdims=True)), sink)
        p_c = jnp.exp2(s_c - jnp.concatenate([m] * (n_ctx // LANES), axis=1))
        p_w = jnp.exp2(s_w - jnp.concatenate([m] * (WIN_SPAN // LANES), axis=1))
        p_s[:, 0:n_ctx] = p_c.astype(BF16)
        p_s[:, n_ctx:] = p_w.astype(BF16)
        psum = p_c[:, 0:LANES]
        for t in range(1, n_ctx // LANES):
            psum = psum + p_c[:, t * LANES:(t + 1) * LANES]
        for t in range(WIN_SPAN // LANES):
            psum = psum + p_w[:, t * LANES:(t + 1) * LANES]
        den = jnp.sum(psum, axis=1, keepdims=True) + jnp.exp2(sink - m)
        o = (_dot(p_s[:, 0:n_ctx], v_ref[0, 0:n_ctx, :])
             + _dot(p_s[:, n_ctx:width], v_ref[0, pl.ds(start, WIN_SPAN), :])) / den
        o_ref[0, :, g * LANES:(g + 1) * LANES] = jnp.where(lane < HEAD_DIM, o[0:TQ], o[TQ:2 * TQ]).astype(BF16)


def _window_mixer(p_rope, p_plain, sink, n_ctx):
    b, l, _ = p_rope.shape
    kern = functools.partial(_window_kernel, n_ctx=n_ctx, total=l)
    grid_spec = pltpu.PrefetchScalarGridSpec(
        num_scalar_prefetch=1,
        grid=(b, l // TQ),
        in_specs=[pl.BlockSpec((1, TQ, 512), lambda i, j, s: (i, j, 0)),
                  pl.BlockSpec((1, l, LANES), lambda i, j, s: (i, 0, 10)),
                  pl.BlockSpec((1, l, LANES), lambda i, j, s: (i, 0, PL_BV))],
        out_specs=pl.BlockSpec((1, TQ, 512), lambda i, j, s: (i, j, 0)),
        scratch_shapes=[pltpu.VMEM((2 * TQ, LANES), BF16), pltpu.VMEM((2 * TQ, n_ctx + WIN_SPAN), F32),
                        pltpu.VMEM((2 * TQ, n_ctx + WIN_SPAN), BF16)] * WIN_BLOCKS,
    )
    return pl.pallas_call(
        kern,
        grid_spec=grid_spec,
        out_shape=jax.ShapeDtypeStruct((b, l, BRANCH_W), BF16),
        compiler_params=_cparams(("parallel", "arbitrary")),
        name="window_mixer",
    )(sink, p_rope, p_rope, p_plain)


GLA_GROUP = 4


def _gla_kernel(q_ref, k_ref, v_ref, g_ref, wg_ref, bg_ref, gn_ref, o_ref, accf_s, accb_s, st_s, *, total, n_ctx):
    ck = GLA_CHUNK
    n_groups = total // (ck * GLA_GROUP)
    ctx_groups = n_ctx // (ck * GLA_GROUP)
    hw = GLA_HEADS * GLA_DK
    acc_refs = (accf_s, accb_s)

    st_s[...] = jnp.zeros_like(st_s)

    rr = lax.broadcasted_iota(jnp.int32, (ck, ck), 0)
    cc = lax.broadcasted_iota(jnp.int32, (ck, ck), 1)
    tri = (rr >= cc, rr <= cc)
    tri_b = tuple(jnp.where(t, 1.0, 0.0).astype(BF16) for t in tri)
    tri2 = tuple(jnp.concatenate([t, t], axis=0) for t in tri)
    lane = lax.broadcasted_iota(jnp.int32, (LANES, LANES), 1)

    def body(i, _):
        gf = i
        gb = jnp.where(i < ctx_groups, ctx_groups - 1 - i, n_groups - 1 - (i - ctx_groups))
        chains = []
        for u in range(GLA_GROUP):
            for d, grp in ((0, gf), (1, gb)):
                c = grp * GLA_GROUP + (u if d == 0 else GLA_GROUP - 1 - u)
                chains.append((d, pl.multiple_of(c * ck, ck)))

        gate = [_dot(g_ref[0, pl.ds(r0, ck), :], wg_ref[:, d * hw:(d + 1) * hw]) + bg_ref[:, d * hw:(d + 1) * hw]
                for d, r0 in chains]
        cums = []
        for (d, r0), y in zip(chains, gate):
            la = (jnp.minimum(y, 0.0) - jnp.log(1.0 + jnp.exp(-jnp.abs(y)))) * (1.0 / GLA_TAU)
            hi = la.astype(BF16)
            lo = (la - hi.astype(F32)).astype(BF16)
            cums.append(_dot(tri_b[d], hi) + _dot(tri_b[d], lo))

        prepped = []
        for (d, r0), cum in zip(chains, cums):
            tot = cum[ck - 1:ck, :] if d == 0 else cum[0:1, :]
            qf = q_ref[0, pl.ds(r0, ck), :].astype(F32)
            kf = k_ref[0, pl.ds(r0, ck), :].astype(F32)
            qt = (qf * jnp.exp(cum)).astype(BF16)
            kt = (kf * jnp.exp(-cum)).astype(BF16)
            kw = (kf * jnp.exp(tot - cum)).astype(BF16)
            qs, a2 = [], []
            for p in range(2):
                sl = slice(p * LANES, (p + 1) * LANES)
                qs.append(_stack_halves(qt[:, sl]))
                a2.append(jnp.where(tri2[d], _dot_nt(qs[p], kt[:, sl]), 0.0).astype(BF16))
            prepped.append((qs, a2, kw, jnp.exp(tot)))

        intra, update = [], []
        for (d, r0), (qs, a2, kw, dec) in zip(chains, prepped):
            av, upd = [], []
            for p in range(2):
                sl = slice(p * LANES, (p + 1) * LANES)
                halves = []
                for hh in range(2):
                    vh = v_ref[0, pl.ds(r0, ck), (2 * p + hh) * GLA_DV:(2 * p + hh + 1) * GLA_DV]
                    av.append(_dot(a2[p][hh * ck:(hh + 1) * ck], vh))
                    halves.append(_dot_tn(vh, kw[:, sl]))
                upd.append(jnp.where(lane < GLA_DK, halves[0], halves[1]))
            intra.append(av)
            update.append(upd)

        states = [[st_s[2 * d + p] for p in range(2)] for d in range(2)]
        for (d, r0), (qs, a2, kw, dec), av, upd in zip(chains, prepped, intra, update):
            for p in range(2):
                inter = _dot_nt(qs[p], states[d][p].astype(BF16))
                for hh in range(2):
                    vs = slice((2 * p + hh) * GLA_DV, (2 * p + hh + 1) * GLA_DV)
                    acc_refs[d][pl.ds(r0, ck), vs] = av[2 * p + hh] + inter[hh * ck:(hh + 1) * ck]
                states[d][p] = states[d][p] * dec[:, p * LANES:(p + 1) * LANES] + upd[p]
        for d in range(2):
            for p in range(2):
                st_s[2 * d + p] = states[d][p]
        return 0

    lax.fori_loop(0, n_groups, body, 0)

    def finish(i, _):
        r0 = pl.multiple_of(i * TQ, TQ)
        for h in range(GLA_HEADS):
            vs = slice(h * GLA_DV, (h + 1) * GLA_DV)
            o = accf_s[pl.ds(r0, TQ), vs] + accb_s[pl.ds(r0, TQ), vs]
            y = o * lax.rsqrt(jnp.mean(o * o, axis=-1, keepdims=True) + EPS) * gn_ref[:, vs]
            o_ref[0, pl.ds(r0, TQ), vs] = y.astype(BF16)
        return 0

    lax.fori_loop(0, total // TQ, finish, 0)


def _gla_mixer(p_plain, wg, bg, gn, n_ctx):
    b, l, _ = p_plain.shape
    kern = functools.partial(_gla_kernel, total=l, n_ctx=n_ctx)
    const = lambda i: (0, 0)
    return pl.pallas_call(
        kern,
        grid=(b,),
        in_specs=[pl.BlockSpec((1, l, 256), lambda i: (i, 0, PL_AQ)),
                  pl.BlockSpec((1, l, 256), lambda i: (i, 0, PL_AK)),
                  pl.BlockSpec((1, l, 512), lambda i: (i, 0, PL_AV)),
                  pl.BlockSpec((1, l, LANES), lambda i: (i, 0, PL_AG)),
                  pl.BlockSpec((LANES, 512), const),
                  pl.BlockSpec((1, 512), const),
                  pl.BlockSpec((1, 512), const)],
        out_specs=pl.BlockSpec((1, l, BRANCH_W), lambda i: (i, 0, 0)),
        out_shape=jax.ShapeDtypeStruct((b, l, BRANCH_W), BF16),
        scratch_shapes=[pltpu.VMEM((l, BRANCH_W), F32), pltpu.VMEM((l, BRANCH_W), F32),
                        pltpu.VMEM((4, GLA_DV, LANES), F32)],
        compiler_params=_cparams(("parallel",)),
        name="gla_mixer",
    )(p_plain, p_plain, p_plain, p_plain, wg, bg, gn)


def _merge_kernel(h_ref, oa_ref, ob_ref, oc_ref, od_ref, za_ref, zb_ref, zc_ref, zd_ref, *rest, alpha, off, n_tiles):
    *x_refs, mod_ref, wm_ref, wup_ref, wout_ref, lng_ref, lnb_ref, out_ref, acc_s = rest
    j = pl.program_id(1)

    def finish_previous():
        y = _dot(acc_s[...], wout_ref[...])
        r = alpha * _stream_tile(x_refs, j - 1 + off) + mod_ref[0, 0, 2:3, :] * y
        mu = jnp.mean(r, axis=-1, keepdims=True)
        rc = r - mu
        var = jnp.mean(rc * rc, axis=-1, keepdims=True)
        out_ref[0] = rc * lax.rsqrt(var + EPS) * lng_ref[...] + lnb_ref[...]

    def merge_current():
        h = h_ref[0]
        acc = None
        for i, (o_ref, z_ref) in enumerate(((oa_ref, za_ref), (ob_ref, zb_ref), (oc_ref, zc_ref), (od_ref, zd_ref))):
            z = z_ref[0].astype(F32)
            br = (o_ref[0].astype(F32) * (z * _sigmoid(z))).astype(BF16)
            term = _sigmoid(_dot(h, wm_ref[i])) * _dot(br, wup_ref[i])
            acc = term if acc is None else acc + term
        acc_s[...] = acc.astype(BF16)

    @pl.when(j == 0)
    def _():
        merge_current()

    @pl.when((j > 0) & (j < n_tiles))
    def _():
        finish_previous()
        merge_current()

    @pl.when(j == n_tiles)
    def _():
        finish_previous()


def _merge(h, outs, p_plain, xs, mod, wm, wup, wout, ln_g, ln_b, layer, skip_ctx):
    b, l, d = h.shape
    off = 1 if skip_ctx else 0
    nt = l // TQ - off
    cur = lambda j: jnp.minimum(j, nt - 1) + off
    prev = lambda j: jnp.maximum(j - 1, 0) + off
    row = lambda i, j: (i, cur(j), 0)
    zspec = lambda blk: pl.BlockSpec((1, TQ, BRANCH_W), lambda i, j: (i, cur(j), blk))
    c2 = lambda i, j: (0, 0)
    in_specs = ([pl.BlockSpec((1, TQ, d), row)]
                + [pl.BlockSpec((1, TQ, BRANCH_W), row)] * 4
                + [zspec(PL_AZ), zspec(PL_BZ), zspec(PL_CZ), zspec(PL_DZ)]
                + _stream_specs(xs, d, prev)
                + [pl.BlockSpec((1, 1, 3, d), lambda i, j: (i, jnp.minimum(prev(j), 1), 0, 0)),
                   pl.BlockSpec((None, 4, d, d), lambda i, j: (layer, 0, 0, 0)),
                   pl.BlockSpec((None, 4, BRANCH_W, d), lambda i, j: (layer, 0, 0, 0)),
                   pl.BlockSpec((None, d, d), lambda i, j: (layer, 0, 0)),
                   pl.BlockSpec((1, d), c2), pl.BlockSpec((1, d), c2)])
    return pl.pallas_call(
        functools.partial(_merge_kernel, alpha=(2 * DEPTH) ** 0.25, off=off, n_tiles=nt),
        grid=(b, nt + 1),
        in_specs=in_specs,
        out_specs=pl.BlockSpec((1, TQ, d), lambda i, j: (i, jnp.maximum(j - 1, 0), 0)),
        out_shape=jax.ShapeDtypeStruct((b, nt * TQ, d), F32),
        scratch_shapes=[pltpu.VMEM((TQ, d), BF16)],
        compiler_params=_cparams(("parallel", "arbitrary")),
        name="merge",
    )(h, *outs, p_plain, p_plain, p_plain, p_plain, *xs, mod, wm, wup, wout,
      ln_g.reshape(1, d), ln_b.reshape(1, d))


def _perm_window(a, axis):
    a = jnp.moveaxis(a, axis, -1)
    lead = a.shape[:-1]
    a = a.reshape(lead + (WIN_KV, WIN_HEADS // WIN_KV, HEAD_DIM)).swapaxes(-3, -2).reshape(lead + (BRANCH_W,))
    return jnp.moveaxis(a, -1, axis)


def _perm_global(a, axis):
    a = jnp.moveaxis(a, axis, -1)
    lead = a.shape[:-1]
    a = a.reshape(lead + (GLB_KV // 2, 2, GLB_HEADS // GLB_KV, HEAD_DIM)).swapaxes(-3, -2).reshape(lead + (BRANCH_W,))
    return jnp.moveaxis(a, -1, axis)


def _split_w_in(w):
    cols = lambda off, n: w[..., off:off + n]
    pad = jnp.zeros(w.shape[:-1] + (LANES - 2 * GLA_RANK,), w.dtype)
    w_plain = jnp.concatenate([
        cols(_A_V, 512), cols(_A_Z, 512), _perm_window(cols(_B_Z, 512), -1), _perm_global(cols(_C_Z, 512), -1),
        cols(_D_Z, 512), cols(_A_Q, 256) * GLA_DK ** -0.5, cols(_A_K, 256), cols(_C_V, 256), cols(_D_V, 256),
        cols(_B_V, 128), cols(_A_GF, 2 * GLA_RANK), pad], axis=-1)
    w_rope = jnp.concatenate([_perm_window(cols(_B_Q, 512), -1) * HEAD_DIM ** -0.5, cols(_D_Q, 512) * HEAD_DIM ** -0.5,
                              cols(_D_K, 256), cols(_B_K, 128)], axis=-1)
    w_nr = jnp.concatenate([_perm_global(cols(_C_Q, 512), -1), cols(_C_K, 256)], axis=-1)
    return w_plain.astype(BF16), w_rope.astype(BF16), w_nr.astype(BF16)


def _rope_tables(n_ctx, n_lat):
    t = np.arange(n_lat)
    freqs = ROPE_BASE ** (-np.arange(ROPE_HALF, dtype=np.float32) / ROPE_HALF)
    pos = np.stack([(t // GRID_W).astype(np.float32), (t % GRID_W).astype(np.float32)], axis=1)
    ang = jnp.asarray(pos[:, :, None] * freqs[None, None, :], F32)
    cos, sin = jnp.cos(ang), jnp.sin(ang)
    cos_h = jnp.concatenate([cos, cos], axis=-1).reshape(n_lat, HEAD_DIM)
    sin_h = jnp.concatenate([-sin, sin], axis=-1).reshape(n_lat, HEAD_DIM)
    cos_t = jnp.concatenate([jnp.ones((n_ctx, HEAD_DIM), F32), cos_h], axis=0)
    sin_t = jnp.concatenate([jnp.zeros((n_ctx, HEAD_DIM), F32), sin_h], axis=0)
    return jnp.tile(cos_t, (1, 2)), jnp.tile(sin_t, (1, 2))


def kernel(x, c, ctx, c_ctx, w_ada, b_ada, w_in, gla_w_gate, gla_b_gate, gla_norm, win_sink, glb_q_norm,
           glb_k_norm, diff_lambda, diff_norm, w_merge, w_up, w_out, ln_g, ln_b):
    b, n_lat, d = x.shape
    n_ctx = ctx.shape[1]
    assert n_ctx == TQ and n_lat % TK == 0 and d == 1024
    xs = (ctx, x)
    cs = jnp.zeros((16, d), F32).at[0:b].set(c).at[b].set(c_ctx)
    cos_t, sin_t = _rope_tables(n_ctx, n_lat)
    seg = jnp.asarray(np.kron(np.eye(2, dtype=np.float32), np.ones((HEAD_DIM, HEAD_DIM), np.float32)), BF16)

    w_plain, w_rope, w_nr = _split_w_in(w_in)
    gain_nr = jnp.concatenate([jnp.tile(glb_q_norm * (HEAD_DIM ** -0.5 * LOG2E), (1, GLB_HEADS)),
                               jnp.tile(glb_k_norm, (1, GLB_KV))], axis=1)
    wg = jnp.zeros((DEPTH, LANES, 2 * GLA_HEADS * GLA_DK), F32)
    wg = wg.at[:, 0:GLA_RANK, 0:256].set(gla_w_gate[:, 0]).at[:, GLA_RANK:2 * GLA_RANK, 256:512].set(gla_w_gate[:, 1])
    wg = wg.astype(BF16)
    wup = jnp.stack([w_up[:, 0], _perm_window(w_up[:, 1], -2), _perm_global(w_up[:, 2], -2), w_up[:, 3]],
                    axis=1).astype(BF16)
    wm, wout = w_merge.astype(BF16), w_out.astype(BF16)

    ada = _ada(cs, w_ada, b_ada)
    mod_x = ada[:, 0:b].reshape(DEPTH, b, 1, 3, d)
    mod_c = jnp.broadcast_to(ada[:, b].reshape(DEPTH, 1, 1, 3, d), (DEPTH, b, 1, 3, d))
    mods = jnp.concatenate([mod_c, mod_x], axis=2)

    for layer in range(DEPTH):
        last = layer == DEPTH - 1
        lam_init = 0.8 - 0.6 * math.exp(-0.3 * layer)
        mod = mods[layer]
        h = _ln_mod(xs, mod)
        p_plain = _inproj(h, w_plain, layer, "plain")
        p_rope = _inproj(h, w_rope, layer, "rope", cos_t, sin_t)
        p_nr = _inproj(h, w_nr, layer, "nr", cos_t, sin_t, gain_nr[layer].reshape(1, NR_W), seg)

        o_a = _gla_mixer(p_plain, wg[layer], gla_b_gate[layer].reshape(1, 512), gla_norm[layer].reshape(1, BRANCH_W),
                         n_ctx)
        o_b = _window_mixer(p_rope, p_plain, win_sink[layer], n_ctx)
        o_c, o_d = _dense_mixers(p_nr, p_rope, p_plain, diff_lambda[layer], diff_norm[layer], lam_init, n_ctx)

        xs = (_merge(h, (o_a, o_b, o_c, o_d), p_plain, xs, mod, wm, wup, wout, ln_g[layer], ln_b[layer], layer,
                     skip_ctx=last),)
    return xs[0]
```

```python
import functools
import math

import numpy as np
import jax
import jax.numpy as jnp
from jax import lax
from jax.experimental import pallas as pl
from jax.experimental.pallas import tpu as pltpu

F32 = jnp.float32
BF16 = jnp.bfloat16

DEPTH = 2
GRID_W = 64
HEAD_DIM = 64
ROPE_HALF = HEAD_DIM // 4
ROPE_BASE = 10000.0
EPS = 1e-6
NEG_INF = -1e30
GLA_HEADS, GLA_DK, GLA_DV, GLA_RANK, GLA_TAU, GLA_CHUNK = 4, 64, 128, 16, 16.0, 64
WIN_HEADS, WIN_KV, WINDOW = 8, 2, 128
GLB_HEADS, GLB_KV = 8, 4
DIF_HEADS, DIF_KV, DIF_DV = 4, 2, 128
BRANCH_W = 512

LANES = 128
TQ = 256
TK = 512
VMEM_LIMIT = 56 << 20

_A_Q, _A_K, _A_V, _A_GF, _A_Z = 0, 256, 512, 1024, 1056
_B_Q, _B_K, _B_V, _B_Z = 1568, 2080, 2208, 2336
_C_Q, _C_K, _C_V, _C_Z = 2848, 3360, 3616, 3872
_D_Q, _D_K, _D_V, _D_Z = 4384, 4896, 5152, 5408

PL_AV, PL_AZ, PL_BZ, PL_CZ, PL_DZ = 0, 1, 2, 3, 4
PL_AQ, PL_AK, PL_CV, PL_DV = 10, 11, 12, 13
PL_BV, PL_AG = 28, 29
PLAIN_W, ROPE_W, NR_W = 3840, 1408, 768
ROPE_QW = 1024
LOG2E = math.log2(math.e)


def _cparams(sem):
    return pltpu.CompilerParams(dimension_semantics=sem, vmem_limit_bytes=VMEM_LIMIT)


def _sigmoid(x):
    return 1.0 / (1.0 + jnp.exp(-x))


def _dot(a, b):
    return jnp.dot(a, b, preferred_element_type=F32)


def _dot_nt(a, b):
    return lax.dot_general(a, b, (((1,), (1,)), ((), ())), preferred_element_type=F32)


def _dot_tn(a, b):
    return lax.dot_general(a, b, (((0,), (0,)), ((), ())), preferred_element_type=F32)


def _ada_kernel(c_ref, w_ref, b_ref, o_ref):
    c = c_ref[...]
    s = c * _sigmoid(c)
    o_ref[...] = jnp.dot(s, w_ref[...], preferred_element_type=F32,
                         precision=lax.Precision.HIGHEST) + b_ref[...]


def _ada(cs, w_ada, b_ada):
    r, d = cs.shape
    depth, _, n = w_ada.shape
    bn = 1024
    return pl.pallas_call(
        _ada_kernel,
        grid=(depth, n // bn),
        in_specs=[pl.BlockSpec((r, d), lambda l, j: (0, 0)),
                  pl.BlockSpec((None, d, bn), lambda l, j: (l, 0, j)),
                  pl.BlockSpec((None, 1, bn), lambda l, j: (l, 0, j))],
        out_specs=pl.BlockSpec((None, r, bn), lambda l, j: (l, 0, j)),
        out_shape=jax.ShapeDtypeStruct((depth, r, n), F32),
        compiler_params=_cparams(("arbitrary", "arbitrary")),
        name="ada",
    )(cs, w_ada, b_ada.reshape(depth, 1, n))


def _stream_specs(xs, d, off=0):
    if len(xs) == 1:
        return [pl.BlockSpec((1, TQ, d), lambda i, j: (i, j + off, 0))]
    return [pl.BlockSpec((1, TQ, d), lambda i, j: (i, 0, 0)),
            pl.BlockSpec((1, TQ, d), lambda i, j: (i, jnp.maximum(j + off - 1, 0), 0))]


def _stream_tile(refs, off=0):
    if len(refs) == 1:
        return refs[0][0]
    return jnp.where(pl.program_id(1) + off == 0, refs[0][0], refs[1][0])


def _ln_kernel(*refs):
    *x_refs, mod_ref, h_ref = refs
    x = _stream_tile(x_refs)
    mu = jnp.mean(x, axis=-1, keepdims=True)
    xc = x - mu
    var = jnp.mean(xc * xc, axis=-1, keepdims=True)
    y = xc * lax.rsqrt(var + EPS)
    shift = mod_ref[0, 0, 0:1, :]
    scale = mod_ref[0, 0, 1:2, :]
    h_ref[0] = (y * (1.0 + scale) + shift).astype(BF16)


def _ln_mod(xs, mod):
    b, _, d = xs[0].shape
    l = sum(a.shape[1] for a in xs)
    return pl.pallas_call(
        _ln_kernel,
        grid=(b, l // TQ),
        in_specs=_stream_specs(xs, d) + [pl.BlockSpec((1, 1, 3, d), lambda i, j: (i, jnp.minimum(j, 1), 0, 0))],
        out_specs=pl.BlockSpec((1, TQ, d), lambda i, j: (i, j, 0)),
        out_shape=jax.ShapeDtypeStruct((b, l, d), BF16),
        compiler_params=_cparams(("parallel", "parallel")),
        name="ln_mod",
    )(*xs, mod)


def _rope(y, cos, sin):
    return y * cos + pltpu.roll(y, LANES // 2, 1) * sin


def _inproj_plain_kernel(h_ref, w_ref, o_ref):
    o_ref[0] = _dot(h_ref[0], w_ref[...]).astype(BF16)


def _inproj_rope_kernel(h_ref, w_ref, cos_ref, sin_ref, o_ref, *, width, q_width):
    acc = _dot(h_ref[0], w_ref[...])
    cos, sin = cos_ref[...], sin_ref[...]
    for c in range(width // LANES):
        y = _rope(acc[:, c * LANES:(c + 1) * LANES], cos, sin)
        if c * LANES < q_width:
            y = y * LOG2E
        o_ref[0, :, c * LANES:(c + 1) * LANES] = y.astype(BF16)


def _inproj_nr_kernel(h_ref, w_ref, cos_ref, sin_ref, gain_ref, seg_ref, o_ref, *, width):
    acc = _dot(h_ref[0], w_ref[...])
    cos, sin = cos_ref[...], sin_ref[...]
    for c in range(width // LANES):
        y = acc[:, c * LANES:(c + 1) * LANES]
        ss = _dot((y * y).astype(BF16), seg_ref[...])
        y = y * lax.rsqrt(ss * (1.0 / HEAD_DIM) + EPS) * gain_ref[:, c * LANES:(c + 1) * LANES]
        o_ref[0, :, c * LANES:(c + 1) * LANES] = _rope(y, cos, sin).astype(BF16)


def _inproj(h, w, layer, mode, cos=None, sin=None, gain=None, seg=None):
    b, l, d = h.shape
    n = w.shape[2]
    bm = l // 2
    if mode == "plain":
        bn = 768
        return pl.pallas_call(
            _inproj_plain_kernel,
            grid=(b, l // bm, n // bn),
            in_specs=[pl.BlockSpec((1, bm, d), lambda i, r, j: (i, r, 0)),
                      pl.BlockSpec((None, d, bn), lambda i, r, j: (layer, 0, j))],
            out_specs=pl.BlockSpec((1, bm, bn), lambda i, r, j: (i, r, j)),
            out_shape=jax.ShapeDtypeStruct((b, l, n), BF16),
            compiler_params=_cparams(("parallel", "parallel", "arbitrary")),
            name="inproj_plain",
        )(h, w)
    tab = pl.BlockSpec((bm, LANES), lambda i, r: (r, 0))
    in_specs = [pl.BlockSpec((1, bm, d), lambda i, r: (i, r, 0)),
                pl.BlockSpec((None, d, n), lambda i, r: (layer, 0, 0)), tab, tab]
    args = [h, w, cos, sin]
    if mode == "rope":
        body = functools.partial(_inproj_rope_kernel, width=n, q_width=ROPE_QW)
    else:
        body = functools.partial(_inproj_nr_kernel, width=n)
        in_specs += [pl.BlockSpec((1, n), lambda i, r: (0, 0)),
                     pl.BlockSpec((LANES, LANES), lambda i, r: (0, 0))]
        args += [gain, seg]
    return pl.pallas_call(
        body,
        grid=(b, l // bm),
        in_specs=in_specs,
        out_specs=pl.BlockSpec((1, bm, n), lambda i, r: (i, r, 0)),
        out_shape=jax.ShapeDtypeStruct((b, l, n), BF16),
        compiler_params=_cparams(("parallel", "parallel")),
        name="inproj_" + mode,
    )(*args)


def _stack_halves(q, rotary=False):
    lane = lax.broadcasted_iota(jnp.int32, q.shape, 1)
    first = ((lane // (HEAD_DIM // 2)) % 2 == 0) if rotary else (lane < HEAD_DIM)
    zero = jnp.zeros_like(q)
    return jnp.concatenate([jnp.where(first, q, zero), jnp.where(first, zero, q)], axis=0)


FLASH_ROWS = 2 * TQ


FLASH_REFS = 6


def _flash_scratch(n_streams):
    stat = pltpu.VMEM((FLASH_ROWS, LANES), F32)
    one = [pltpu.VMEM((FLASH_ROWS, LANES), BF16), stat, stat, stat,
           pltpu.VMEM((FLASH_ROWS, TK), F32), pltpu.VMEM((FLASH_ROWS, TK), BF16)]
    return one * n_streams


def _fold_chunk(stream, s, v, first):
    _, m_s, l_s, acc_s, _, p_s = stream
    tk = s.shape[1]
    m_new = jnp.broadcast_to(jnp.max(s, axis=1, keepdims=True), (FLASH_ROWS, LANES))
    if not first:
        m_old = m_s[...]
        m_new = jnp.maximum(m_old, m_new)
        alpha = jnp.exp2(m_old - m_new)
    p = jnp.exp2(s - jnp.concatenate([m_new] * (tk // LANES), axis=1))
    p_s[:, 0:tk] = p.astype(BF16)
    psum = p[:, 0:LANES]
    for t in range(1, tk // LANES):
        psum = psum + p[:, t * LANES:(t + 1) * LANES]
    pv = _dot(p_s[:, 0:tk], v)
    m_s[...] = m_new
    l_s[...] = psum if first else alpha * l_s[...] + psum
    acc_s[...] = pv if first else alpha * acc_s[...] + pv


def _flash(q_blocks, kv_srcs, n_ctx, n_lat_chunks, scratch):
    ns = len(q_blocks)
    streams = [scratch[FLASH_REFS * i:FLASH_REFS * (i + 1)] for i in range(ns)]
    kv_lanes = [slice(kb * LANES, (kb + 1) * LANES) for _, _, kb in kv_srcs]

    def scores(i, r0, tk):
        qs_s, _, _, _, s_s, _ = streams[i]
        s_s[:, 0:tk] = _dot_nt(qs_s[...], kv_srcs[i][0][0, pl.ds(r0, tk), kv_lanes[i]])

    def accumulate(i, r0, tk, first=False):
        s_s = streams[i][4]
        _fold_chunk(streams[i], s_s[:, 0:tk], kv_srcs[i][1][0, pl.ds(r0, tk), kv_lanes[i]], first)

    def stage(r0, tk, nxt, first=False):
        for i in range(ns):
            accumulate(i, r0, tk, first)
            if nxt is not None:
                scores(i, *nxt)

    def head():
        for i in range(ns):
            streams[i][0][...] = _stack_halves(q_blocks[i](), rotary=True)
            scores(i, 0, n_ctx)
        stage(0, n_ctx, (n_ctx, TK), first=True)

    def body(c, _):
        r0 = pl.multiple_of(n_ctx + c * TK, LANES)
        stage(r0, TK, (r0 + TK, TK))
        return 0

    def latent():
        lax.fori_loop(0, n_lat_chunks - 1, body, 0)
        stage(n_ctx + (n_lat_chunks - 1) * TK, TK, None)

    def results():
        return [acc_s[...] / jnp.sum(l_s[...], axis=1, keepdims=True) for _, _, l_s, acc_s, _, _ in streams]

    return head, latent, results


N_GLB_BLOCKS = GLB_HEADS // 2
N_DIF_BLOCKS = DIF_HEADS


def _dense_kernel(qc_ref, kc_ref, vc_ref, qd_ref, kd_ref, vd_ref, lam_ref, g_ref, oc_ref, od_ref, *scratch,
                  n_ctx, n_lat, n_tiles, lam_init):
    load = lambda ref, c: (lambda: ref[0, :, c * LANES:(c + 1) * LANES])
    q_blocks = [load(qc_ref, c) for c in range(N_GLB_BLOCKS)] + [load(qd_ref, c) for c in range(N_DIF_BLOCKS)]
    kv_srcs = ([(kc_ref, vc_ref, c // 2) for c in range(N_GLB_BLOCKS)]
               + [(kd_ref, vd_ref, c // 2) for c in range(N_DIF_BLOCKS)])
    head, latent, results = _flash(q_blocks, kv_srcs, n_ctx, n_lat // TK, scratch)

    def write_out():
        outs = results()
        lane = lax.broadcasted_iota(jnp.int32, (TQ, LANES), 1)
        for c, o in enumerate(outs[:N_GLB_BLOCKS]):
            oc_ref[0, :, c * LANES:(c + 1) * LANES] = jnp.where(lane < HEAD_DIM, o[0:TQ], o[TQ:2 * TQ]).astype(BF16)
        lp = lam_ref[...]
        lam = (jnp.exp(jnp.sum(lp[0:1] * lp[1:2], axis=1, keepdims=True))
               - jnp.exp(jnp.sum(lp[2:3] * lp[3:4], axis=1, keepdims=True)) + lam_init)
        for c, o in enumerate(outs[N_GLB_BLOCKS:]):
            o = o[0:TQ] - lam * o[TQ:2 * TQ]
            y = o * lax.rsqrt(jnp.mean(o * o, axis=-1, keepdims=True) + EPS) * g_ref[...]
            od_ref[0, :, c * LANES:(c + 1) * LANES] = (y * (1.0 - lam_init)).astype(BF16)

    j = pl.program_id(1)

    @pl.when(j == 0)
    def _():
        head()

    @pl.when((j > 0) & (j < n_tiles))
    def _():
        write_out()
        head()
        latent()

    @pl.when(j == n_tiles)
    def _():
        write_out()


def _dense_mixers(p_nr, p_rope, p_plain, lam_p, sub_g, lam_init, n_ctx):
    b, l, _ = p_nr.shape
    nt = l // TQ
    kern = functools.partial(_dense_kernel, n_ctx=n_ctx, n_lat=l - n_ctx, n_tiles=nt, lam_init=lam_init)
    tile = lambda blk: pl.BlockSpec((1, TQ, 512), lambda i, j: (i, jnp.minimum(j, nt - 1), blk))
    lagged = pl.BlockSpec((1, TQ, 512), lambda i, j: (i, jnp.maximum(j - 1, 0), 0))
    full = lambda blk: pl.BlockSpec((1, l, 256), lambda i, j: (i, 0, blk))
    out = jax.ShapeDtypeStruct((b, l, BRANCH_W), BF16)
    return pl.pallas_call(
        kern,
        grid=(b, nt + 1),
        in_specs=[tile(0), full(2), full(PL_CV),
                  tile(1), full(4), full(PL_DV),
                  pl.BlockSpec((4, HEAD_DIM), lambda i, j: (0, 0)),
                  pl.BlockSpec((1, DIF_DV), lambda i, j: (0, 0))],
        out_specs=[lagged, lagged],
        out_shape=[out, out],
        scratch_shapes=_flash_scratch(N_GLB_BLOCKS + N_DIF_BLOCKS),
        compiler_params=_cparams(("parallel", "arbitrary")),
        name="dense_mixers",
    )(p_nr, p_nr, p_plain, p_rope, p_rope, p_plain, lam_p, sub_g.reshape(1, DIF_DV))


WIN_SPAN = TQ + 2 * WINDOW
WIN_BLOCKS = WIN_HEADS // WIN_KV


def _window_kernel(sink_ref, q_ref, k_ref, v_ref, o_ref, *scratch, n_ctx, total):
    j = pl.program_id(1)
    start = pl.multiple_of(jnp.clip(j * TQ - WINDOW, 0, total - WIN_SPAN), LANES)
    streams = [scratch[3 * g:3 * (g + 1)] for g in range(WIN_BLOCKS)]
    for g, (qs_s, s_s, _) in enumerate(streams):
        qs_s[...] = _stack_halves(q_ref[0, :, g * LANES:(g + 1) * LANES], rotary=True)
        s_s[:, 0:n_ctx] = _dot_nt(qs_s[...], k_ref[0, 0:n_ctx, :])
        s_s[:, n_ctx:] = _dot_nt(qs_s[...], k_ref[0, pl.ds(start, WIN_SPAN), :])

    row = lax.broadcasted_iota(jnp.int32, (2 * TQ, WIN_SPAN), 0)
    qpos = j * TQ + jnp.where(row >= TQ, row - TQ, row)
    kpos = start + lax.broadcasted_iota(jnp.int32, (2 * TQ, WIN_SPAN), 1)
    valid = (kpos >= n_ctx) & (qpos >= n_ctx) & (jnp.abs(kpos - qpos) <= WINDOW)
    srow = lax.broadcasted_iota(jnp.int32, (2 * TQ, LANES), 0)
    lane = lax.broadcasted_iota(jnp.int32, (TQ, LANES), 1)
    width = n_ctx + WIN_SPAN
    for g, (_, s_s, p_s) in enumerate(streams):
        sink = jnp.where(srow < TQ, sink_ref[g], sink_ref[WIN_BLOCKS + g]) * LOG2E
        s_c = s_s[:, 0:n_ctx]
        s_w = jnp.where(valid, s_s[:, n_ctx:], NEG_INF)
        m = jnp.maximum(jnp.maximum(jnp.max(s_c, axis=1, keepdims=True), jnp.max(s_w, axis=1, keepdims=True)), sink)
        p_c = jnp.exp2(s_c - jnp.concatenate([m] * (n_ctx // LANES), axis=1))
        p_w = jnp.exp2(s_w - jnp.concatenate([m] * (WIN_SPAN // LANES), axis=1))
        p_s[:, 0:n_ctx] = p_c.astype(BF16)
        p_s[:, n_ctx:] = p_w.astype(BF16)
        psum = p_c[:, 0:LANES]
        for t in range(1, n_ctx // LANES):
            psum = psum + p_c[:, t * LANES:(t + 1) * LANES]
        for t in range(WIN_SPAN // LANES):
            psum = psum + p_w[:, t * LANES:(t + 1) * LANES]
        den = jnp.sum(psum, axis=1, keepdims=True) + jnp.exp2(sink - m)
        o = (_dot(p_s[:, 0:n_ctx], v_ref[0, 0:n_ctx, :])
             + _dot(p_s[:, n_ctx:width], v_ref[0, pl.ds(start, WIN_SPAN), :])) / den
        o_ref[0, :, g * LANES:(g + 1) * LANES] = jnp.where(lane < HEAD_DIM, o[0:TQ], o[TQ:2 * TQ]).astype(BF16)


def _window_mixer(p_rope, p_plain, sink, n_ctx):
    b, l, _ = p_rope.shape
    kern = functools.partial(_window_kernel, n_ctx=n_ctx, total=l)
    grid_spec = pltpu.PrefetchScalarGridSpec(
        num_scalar_prefetch=1,
        grid=(b, l // TQ),
        in_specs=[pl.BlockSpec((1, TQ, 512), lambda i, j, s: (i, j, 0)),
                  pl.BlockSpec((1, l, LANES), lambda i, j, s: (i, 0, 10)),
                  pl.BlockSpec((1, l, LANES), lambda i, j, s: (i, 0, PL_BV))],
        out_specs=pl.BlockSpec((1, TQ, 512), lambda i, j, s: (i, j, 0)),
        scratch_shapes=[pltpu.VMEM((2 * TQ, LANES), BF16), pltpu.VMEM((2 * TQ, n_ctx + WIN_SPAN), F32),
                        pltpu.VMEM((2 * TQ, n_ctx + WIN_SPAN), BF16)] * WIN_BLOCKS,
    )
    return pl.pallas_call(
        kern,
        grid_spec=grid_spec,
        out_shape=jax.ShapeDtypeStruct((b, l, BRANCH_W), BF16),
        compiler_params=_cparams(("parallel", "arbitrary")),
        name="window_mixer",
    )(sink, p_rope, p_rope, p_plain)


GLA_GROUP = 4


def _gla_kernel(q_ref, k_ref, v_ref, g_ref, wg_ref, bg_ref, gn_ref, o_ref, accf_s, accb_s, st_s, *, total, n_ctx):
    ck = GLA_CHUNK
    n_groups = total // (ck * GLA_GROUP)
    ctx_groups = n_ctx // (ck * GLA_GROUP)
    hw = GLA_HEADS * GLA_DK
    acc_refs = (accf_s, accb_s)

    st_s[...] = jnp.zeros_like(st_s)

    rr = lax.broadcasted_iota(jnp.int32, (ck, ck), 0)
    cc = lax.broadcasted_iota(jnp.int32, (ck, ck), 1)
    tri = (rr >= cc, rr <= cc)
    tri_b = tuple(jnp.where(t, 1.0, 0.0).astype(BF16) for t in tri)
    tri2 = tuple(jnp.concatenate([t, t], axis=0) for t in tri)
    lane = lax.broadcasted_iota(jnp.int32, (LANES, LANES), 1)

    def body(i, _):
        gf = i
        gb = jnp.where(i < ctx_groups, ctx_groups - 1 - i, n_groups - 1 - (i - ctx_groups))
        chains = []
        for u in range(GLA_GROUP):
            for d, grp in ((0, gf), (1, gb)):
                c = grp * GLA_GROUP + (u if d == 0 else GLA_GROUP - 1 - u)
                chains.append((d, pl.multiple_of(c * ck, ck)))

        gate = [_dot(g_ref[0, pl.ds(r0, ck), :], wg_ref[:, d * hw:(d + 1) * hw]) + bg_ref[:, d * hw:(d + 1) * hw]
                for d, r0 in chains]
        cums = []
        for (d, r0), y in zip(chains, gate):
            la = (jnp.minimum(y, 0.0) - jnp.log(1.0 + jnp.exp(-jnp.abs(y)))) * (1.0 / GLA_TAU)
            hi = la.astype(BF16)
            lo = (la - hi.astype(F32)).astype(BF16)
            cums.append(_dot(tri_b[d], hi) + _dot(tri_b[d], lo))

        prepped = []
        for (d, r0), cum in zip(chains, cums):
            tot = cum[ck - 1:ck, :] if d == 0 else cum[0:1, :]
            qf = q_ref[0, pl.ds(r0, ck), :].astype(F32)
            kf = k_ref[0, pl.ds(r0, ck), :].astype(F32)
            qt = (qf * jnp.exp(cum)).astype(BF16)
            kt = (kf * jnp.exp(-cum)).astype(BF16)
            kw = (kf * jnp.exp(tot - cum)).astype(BF16)
            qs, a2 = [], []
            for p in range(2):
                sl = slice(p * LANES, (p + 1) * LANES)
                qs.append(_stack_halves(qt[:, sl]))
                a2.append(jnp.where(tri2[d], _dot_nt(qs[p], kt[:, sl]), 0.0).astype(BF16))
            prepped.append((qs, a2, kw, jnp.exp(tot)))

        intra, update = [], []
        for (d, r0), (qs, a2, kw, dec) in zip(chains, prepped):
            av, upd = [], []
            for p in range(2):
                sl = slice(p * LANES, (p + 1) * LANES)
                halves = []
                for hh in range(2):
                    vh = v_ref[0, pl.ds(r0, ck), (2 * p + hh) * GLA_DV:(2 * p + hh + 1) * GLA_DV]
                    av.append(_dot(a2[p][hh * ck:(hh + 1) * ck], vh))
                    halves.append(_dot_tn(vh, kw[:, sl]))
                upd.append(jnp.where(lane < GLA_DK, halves[0], halves[1]))
            intra.append(av)
            update.append(upd)

        states = [[st_s[2 * d + p] for p in range(2)] for d in range(2)]
        for (d, r0), (qs, a2, kw, dec), av, upd in zip(chains, prepped, intra, update):
            for p in range(2):
                inter = _dot_nt(qs[p], states[d][p].astype(BF16))
                for hh in range(2):
                    vs = slice((2 * p + hh) * GLA_DV, (2 * p + hh + 1) * GLA_DV)
                    acc_refs[d][pl.ds(r0, ck), vs] = av[2 * p + hh] + inter[hh * ck:(hh + 1) * ck]
                states[d][p] = states[d][p] * dec[:, p * LANES:(p + 1) * LANES] + upd[p]
        for d in range(2):
            for p in range(2):
                st_s[2 * d + p] = states[d][p]
        return 0

    lax.fori_loop(0, n_groups, body, 0)

    def finish(i, _):
        r0 = pl.multiple_of(i * TQ, TQ)
        for h in range(GLA_HEADS):
            vs = slice(h * GLA_DV, (h + 1) * GLA_DV)
            o = accf_s[pl.ds(r0, TQ), vs] + accb_s[pl.ds(r0, TQ), vs]
            y = o * lax.rsqrt(jnp.mean(o * o, axis=-1, keepdims=True) + EPS) * gn_ref[:, vs]
            o_ref[0, pl.ds(r0, TQ), vs] = y.astype(BF16)
        return 0

    lax.fori_loop(0, total // TQ, finish, 0)


def _gla_mixer(p_plain, wg, bg, gn, n_ctx):
    b, l, _ = p_plain.shape
    kern = functools.partial(_gla_kernel, total=l, n_ctx=n_ctx)
    const = lambda i: (0, 0)
    return pl.pallas_call(
        kern,
        grid=(b,),
        in_specs=[pl.BlockSpec((1, l, 256), lambda i: (i, 0, PL_AQ)),
                  pl.BlockSpec((1, l, 256), lambda i: (i, 0, PL_AK)),
                  pl.BlockSpec((1, l, 512), lambda i: (i, 0, PL_AV)),
                  pl.BlockSpec((1, l, LANES), lambda i: (i, 0, PL_AG)),
                  pl.BlockSpec((LANES, 512), const),
                  pl.BlockSpec((1, 512), const),
                  pl.BlockSpec((1, 512), const)],
        out_specs=pl.BlockSpec((1, l, BRANCH_W), lambda i: (i, 0, 0)),
        out_shape=jax.ShapeDtypeStruct((b, l, BRANCH_W), BF16),
        scratch_shapes=[pltpu.VMEM((l, BRANCH_W), F32), pltpu.VMEM((l, BRANCH_W), F32),
                        pltpu.VMEM((4, GLA_DV, LANES), F32)],
        compiler_params=_cparams(("parallel",)),
        name="gla_mixer",
    )(p_plain, p_plain, p_plain, p_plain, wg, bg, gn)


def _merge_kernel(h_ref, oa_ref, ob_ref, oc_ref, od_ref, za_ref, zb_ref, zc_ref, zd_ref, *rest, alpha, off):
    *x_refs, mod_ref, wm_ref, wup_ref, wout_ref, lng_ref, lnb_ref, out_ref = rest
    h = h_ref[0]
    acc = None
    for i, (o_ref, z_ref) in enumerate(((oa_ref, za_ref), (ob_ref, zb_ref), (oc_ref, zc_ref), (od_ref, zd_ref))):
        z = z_ref[0].astype(F32)
        br = (o_ref[0].astype(F32) * (z * _sigmoid(z))).astype(BF16)
        term = _sigmoid(_dot(h, wm_ref[i])) * _dot(br, wup_ref[i])
        acc = term if acc is None else acc + term
    y = _dot(acc.astype(BF16), wout_ref[...])
    r = alpha * _stream_tile(x_refs, off) + mod_ref[0, 0, 2:3, :] * y
    mu = jnp.mean(r, axis=-1, keepdims=True)
    rc = r - mu
    var = jnp.mean(rc * rc, axis=-1, keepdims=True)
    out_ref[0] = rc * lax.rsqrt(var + EPS) * lng_ref[...] + lnb_ref[...]


def _merge(h, outs, p_plain, xs, mod, wm, wup, wout, ln_g, ln_b, layer, skip_ctx):
    b, l, d = h.shape
    off = 1 if skip_ctx else 0
    nt = l // TQ - off
    row = lambda i, j: (i, j + off, 0)
    zspec = lambda blk: pl.BlockSpec((1, TQ, BRANCH_W), lambda i, j: (i, j + off, blk))
    c2 = lambda i, j: (0, 0)
    in_specs = ([pl.BlockSpec((1, TQ, d), row)]
                + [pl.BlockSpec((1, TQ, BRANCH_W), row)] * 4
                + [zspec(PL_AZ), zspec(PL_BZ), zspec(PL_CZ), zspec(PL_DZ)]
                + _stream_specs(xs, d, off)
                + [pl.BlockSpec((1, 1, 3, d), lambda i, j: (i, jnp.minimum(j + off, 1), 0, 0)),
                   pl.BlockSpec((None, 4, d, d), lambda i, j: (layer, 0, 0, 0)),
                   pl.BlockSpec((None, 4, BRANCH_W, d), lambda i, j: (layer, 0, 0, 0)),
                   pl.BlockSpec((None, d, d), lambda i, j: (layer, 0, 0)),
                   pl.BlockSpec((1, d), c2), pl.BlockSpec((1, d), c2)])
    return pl.pallas_call(
        functools.partial(_merge_kernel, alpha=(2 * DEPTH) ** 0.25, off=off),
        grid=(b, nt),
        in_specs=in_specs,
        out_specs=pl.BlockSpec((1, TQ, d), lambda i, j: (i, j, 0)),
        out_shape=jax.ShapeDtypeStruct((b, nt * TQ, d), F32),
        compiler_params=_cparams(("parallel", "parallel")),
        name="merge",
    )(h, *outs, p_plain, p_plain, p_plain, p_plain, *xs, mod, wm, wup, wout,
      ln_g.reshape(1, d), ln_b.reshape(1, d))


def _perm_window(a, axis):
    a = jnp.moveaxis(a, axis, -1)
    lead = a.shape[:-1]
    a = a.reshape(lead + (WIN_KV, WIN_HEADS // WIN_KV, HEAD_DIM)).swapaxes(-3, -2).reshape(lead + (BRANCH_W,))
    return jnp.moveaxis(a, -1, axis)


def _perm_global(a, axis):
    a = jnp.moveaxis(a, axis, -1)
    lead = a.shape[:-1]
    a = a.reshape(lead + (GLB_KV // 2, 2, GLB_HEADS // GLB_KV, HEAD_DIM)).swapaxes(-3, -2).reshape(lead + (BRANCH_W,))
    return jnp.moveaxis(a, -1, axis)


def _rotary_lanes(a):
    lead = a.shape[:-1]
    a = a.reshape(lead + (a.shape[-1] // LANES, 2, 2, 2, ROPE_HALF))
    return jnp.moveaxis(a, -2, -4).reshape(lead + (-1,))


def _split_w_in(w):
    cols = lambda off, n: w[..., off:off + n]
    pad = jnp.zeros(w.shape[:-1] + (LANES - 2 * GLA_RANK,), w.dtype)
    w_plain = jnp.concatenate([
        cols(_A_V, 512), cols(_A_Z, 512), _perm_window(cols(_B_Z, 512), -1), _perm_global(cols(_C_Z, 512), -1),
        cols(_D_Z, 512), cols(_A_Q, 256) * GLA_DK ** -0.5, cols(_A_K, 256), cols(_C_V, 256), cols(_D_V, 256),
        cols(_B_V, 128), cols(_A_GF, 2 * GLA_RANK), pad], axis=-1)
    w_rope = _rotary_lanes(jnp.concatenate([
        _perm_window(cols(_B_Q, 512), -1) * HEAD_DIM ** -0.5, cols(_D_Q, 512) * HEAD_DIM ** -0.5,
        cols(_D_K, 256), cols(_B_K, 128)], axis=-1))
    w_nr = _rotary_lanes(jnp.concatenate([_perm_global(cols(_C_Q, 512), -1), cols(_C_K, 256)], axis=-1))
    return w_plain.astype(BF16), w_rope.astype(BF16), w_nr.astype(BF16)


def _rope_tables(n_ctx, n_lat):
    t = np.arange(n_lat)
    freqs = ROPE_BASE ** (-np.arange(ROPE_HALF, dtype=np.float32) / ROPE_HALF)
    pos = np.stack([(t // GRID_W).astype(np.float32), (t % GRID_W).astype(np.float32)], axis=1)
    ang = jnp.asarray(pos[:, :, None] * freqs[None, None, :], F32).reshape(n_lat, 2 * ROPE_HALF)
    cos = jnp.tile(jnp.cos(ang), (1, LANES // (2 * ROPE_HALF)))
    sign = np.where(np.arange(LANES) < LANES // 2, -1.0, 1.0).astype(np.float32)
    sin = jnp.tile(jnp.sin(ang), (1, LANES // (2 * ROPE_HALF))) * sign
    cos_t = jnp.concatenate([jnp.ones((n_ctx, LANES), F32), cos], axis=0)
    sin_t = jnp.concatenate([jnp.zeros((n_ctx, LANES), F32), sin], axis=0)
    return cos_t, sin_t


def kernel(x, c, ctx, c_ctx, w_ada, b_ada, w_in, gla_w_gate, gla_b_gate, gla_norm, win_sink, glb_q_norm,
           glb_k_norm, diff_lambda, diff_norm, w_merge, w_up, w_out, ln_g, ln_b):
    b, n_lat, d = x.shape
    n_ctx = ctx.shape[1]
    assert n_ctx == TQ and n_lat % TK == 0 and d == 1024
    xs = (ctx, x)
    cs = jnp.zeros((16, d), F32).at[0:b].set(c).at[b].set(c_ctx)
    cos_t, sin_t = _rope_tables(n_ctx, n_lat)
    head_of_lane = (np.arange(LANES) // (HEAD_DIM // 2)) % 2
    seg = jnp.asarray(head_of_lane[:, None] == head_of_lane[None, :], BF16)

    w_plain, w_rope, w_nr = _split_w_in(w_in)
    gain_nr = _rotary_lanes(jnp.concatenate([jnp.tile(glb_q_norm * (HEAD_DIM ** -0.5 * LOG2E), (1, GLB_HEADS)),
                                             jnp.tile(glb_k_norm, (1, GLB_KV))], axis=1))
    wg = jnp.zeros((DEPTH, LANES, 2 * GLA_HEADS * GLA_DK), F32)
    wg = wg.at[:, 0:GLA_RANK, 0:256].set(gla_w_gate[:, 0]).at[:, GLA_RANK:2 * GLA_RANK, 256:512].set(gla_w_gate[:, 1])
    wg = wg.astype(BF16)
    wup = jnp.stack([w_up[:, 0], _perm_window(w_up[:, 1], -2), _perm_global(w_up[:, 2], -2), w_up[:, 3]],
                    axis=1).astype(BF16)
    wm, wout = w_merge.astype(BF16), w_out.astype(BF16)

    ada = _ada(cs, w_ada, b_ada)
    mod_x = ada[:, 0:b].reshape(DEPTH, b, 1, 3, d)
    mod_c = jnp.broadcast_to(ada[:, b].reshape(DEPTH, 1, 1, 3, d), (DEPTH, b, 1, 3, d))
    mods = jnp.concatenate([mod_c, mod_x], axis=2)

    for layer in range(DEPTH):
        last = layer == DEPTH - 1
        lam_init = 0.8 - 0.6 * math.exp(-0.3 * layer)
        mod = mods[layer]
        h = _ln_mod(xs, mod)
        p_plain = _inproj(h, w_plain, layer, "plain")
        p_rope = _inproj(h, w_rope, layer, "rope", cos_t, sin_t)
        p_nr = _inproj(h, w_nr, layer, "nr", cos_t, sin_t, gain_nr[layer].reshape(1, NR_W), seg)

        o_a = _gla_mixer(p_plain, wg[layer], gla_b_gate[layer].reshape(1, 512), gla_norm[layer].reshape(1, BRANCH_W),
                         n_ctx)
        o_b = _window_mixer(p_rope, p_plain, win_sink[layer], n_ctx)
        o_c, o_d = _dense_mixers(p_nr, p_rope, p_plain, diff_lambda[layer], diff_norm[layer], lam_init, n_ctx)

        xs = (_merge(h, (o_a, o_b, o_c, o_d), p_plain, xs, mod, wm, wup, wout, ln_g[layer], ln_b[layer], layer,
                     skip_ctx=last),)
    return xs[0]
```

```python
import functools
import math

import numpy as np
import jax
import jax.numpy as jnp
from jax import lax
from jax.experimental import pallas as pl
from jax.experimental.pallas import tpu as pltpu

F32 = jnp.float32
BF16 = jnp.bfloat16

DEPTH = 2
GRID_W = 64
HEAD_DIM = 64
ROPE_HALF = HEAD_DIM // 4
ROPE_BASE = 10000.0
EPS = 1e-6
NEG_INF = -1e30
GLA_HEADS, GLA_DK, GLA_DV, GLA_RANK, GLA_TAU, GLA_CHUNK = 4, 64, 128, 16, 16.0, 64
WIN_HEADS, WIN_KV, WINDOW = 8, 2, 128
GLB_HEADS, GLB_KV = 8, 4
DIF_HEADS, DIF_KV, DIF_DV = 4, 2, 128
BRANCH_W = 512

LANES = 128
TQ = 256
TK = 512
VMEM_LIMIT = 56 << 20

_A_Q, _A_K, _A_V, _A_GF, _A_Z = 0, 256, 512, 1024, 1056
_B_Q, _B_K, _B_V, _B_Z = 1568, 2080, 2208, 2336
_C_Q, _C_K, _C_V, _C_Z = 2848, 3360, 3616, 3872
_D_Q, _D_K, _D_V, _D_Z = 4384, 4896, 5152, 5408

PL_AV, PL_AZ, PL_BZ, PL_CZ, PL_DZ = 0, 1, 2, 3, 4
PL_AQ, PL_AK, PL_CV, PL_DV = 10, 11, 12, 13
PL_BV, PL_AG = 28, 29
PLAIN_W, ROPE_W, NR_W = 3840, 1408, 768
ROPE_QW = 1024
LOG2E = math.log2(math.e)


def _cparams(sem):
    return pltpu.CompilerParams(dimension_semantics=sem, vmem_limit_bytes=VMEM_LIMIT)


def _sigmoid(x):
    return 1.0 / (1.0 + jnp.exp(-x))


def _dot(a, b):
    return jnp.dot(a, b, preferred_element_type=F32)


def _dot_nt(a, b):
    return lax.dot_general(a, b, (((1,), (1,)), ((), ())), preferred_element_type=F32)


def _dot_tn(a, b):
    return lax.dot_general(a, b, (((0,), (0,)), ((), ())), preferred_element_type=F32)


def _ada_kernel(c_ref, w_ref, b_ref, o_ref):
    c = c_ref[...]
    s = c * _sigmoid(c)
    o_ref[...] = jnp.dot(s, w_ref[...], preferred_element_type=F32,
                         precision=lax.Precision.HIGHEST) + b_ref[...]


def _ada(cs, w_ada, b_ada):
    r, d = cs.shape
    depth, _, n = w_ada.shape
    bn = 1024
    return pl.pallas_call(
        _ada_kernel,
        grid=(depth, n // bn),
        in_specs=[pl.BlockSpec((r, d), lambda l, j: (0, 0)),
                  pl.BlockSpec((None, d, bn), lambda l, j: (l, 0, j)),
                  pl.BlockSpec((None, 1, bn), lambda l, j: (l, 0, j))],
        out_specs=pl.BlockSpec((None, r, bn), lambda l, j: (l, 0, j)),
        out_shape=jax.ShapeDtypeStruct((depth, r, n), F32),
        compiler_params=_cparams(("arbitrary", "arbitrary")),
        name="ada",
    )(cs, w_ada, b_ada.reshape(depth, 1, n))


def _stream_specs(xs, d, off=0):
    if len(xs) == 1:
        return [pl.BlockSpec((1, TQ, d), lambda i, j: (i, j + off, 0))]
    return [pl.BlockSpec((1, TQ, d), lambda i, j: (i, 0, 0)),
            pl.BlockSpec((1, TQ, d), lambda i, j: (i, jnp.maximum(j + off - 1, 0), 0))]


def _stream_tile(refs, off=0):
    if len(refs) == 1:
        return refs[0][0]
    return jnp.where(pl.program_id(1) + off == 0, refs[0][0], refs[1][0])


def _ln_kernel(*refs):
    *x_refs, mod_ref, h_ref = refs
    x = _stream_tile(x_refs)
    mu = jnp.mean(x, axis=-1, keepdims=True)
    xc = x - mu
    var = jnp.mean(xc * xc, axis=-1, keepdims=True)
    y = xc * lax.rsqrt(var + EPS)
    shift = mod_ref[0, 0, 0:1, :]
    scale = mod_ref[0, 0, 1:2, :]
    h_ref[0] = (y * (1.0 + scale) + shift).astype(BF16)


def _ln_mod(xs, mod):
    b, _, d = xs[0].shape
    l = sum(a.shape[1] for a in xs)
    return pl.pallas_call(
        _ln_kernel,
        grid=(b, l // TQ),
        in_specs=_stream_specs(xs, d) + [pl.BlockSpec((1, 1, 3, d), lambda i, j: (i, jnp.minimum(j, 1), 0, 0))],
        out_specs=pl.BlockSpec((1, TQ, d), lambda i, j: (i, j, 0)),
        out_shape=jax.ShapeDtypeStruct((b, l, d), BF16),
        compiler_params=_cparams(("parallel", "parallel")),
        name="ln_mod",
    )(*xs, mod)


def _rope(y, cos, sin):
    return y * cos + pltpu.roll(y, LANES // 2, 1) * sin


def _inproj_plain_kernel(h_ref, w_ref, o_ref):
    o_ref[0] = _dot(h_ref[0], w_ref[...]).astype(BF16)


def _inproj_rope_kernel(h_ref, w_ref, cos_ref, sin_ref, o_ref, *, width, q_width):
    acc = _dot(h_ref[0], w_ref[...])
    cos, sin = cos_ref[...], sin_ref[...]
    for c in range(width // LANES):
        y = _rope(acc[:, c * LANES:(c + 1) * LANES], cos, sin)
        if c * LANES < q_width:
            y = y * LOG2E
        o_ref[0, :, c * LANES:(c + 1) * LANES] = y.astype(BF16)


def _inproj_nr_kernel(h_ref, w_ref, cos_ref, sin_ref, gain_ref, seg_ref, o_ref, *, width):
    acc = _dot(h_ref[0], w_ref[...])
    cos, sin = cos_ref[...], sin_ref[...]
    for c in range(width // LANES):
        y = acc[:, c * LANES:(c + 1) * LANES]
        ss = _dot((y * y).astype(BF16), seg_ref[...])
        y = y * lax.rsqrt(ss * (1.0 / HEAD_DIM) + EPS) * gain_ref[:, c * LANES:(c + 1) * LANES]
        o_ref[0, :, c * LANES:(c + 1) * LANES] = _rope(y, cos, sin).astype(BF16)


def _inproj(h, w, layer, mode, cos=None, sin=None, gain=None, seg=None):
    b, l, d = h.shape
    n = w.shape[2]
    bm = l // 2
    if mode == "plain":
        bn = 768
        return pl.pallas_call(
            _inproj_plain_kernel,
            grid=(b, l // bm, n // bn),
            in_specs=[pl.BlockSpec((1, bm, d), lambda i, r, j: (i, r, 0)),
                      pl.BlockSpec((None, d, bn), lambda i, r, j: (layer, 0, j))],
            out_specs=pl.BlockSpec((1, bm, bn), lambda i, r, j: (i, r, j)),
            out_shape=jax.ShapeDtypeStruct((b, l, n), BF16),
            compiler_params=_cparams(("parallel", "parallel", "arbitrary")),
            name="inproj_plain",
        )(h, w)
    tab = pl.BlockSpec((bm, LANES), lambda i, r: (r, 0))
    in_specs = [pl.BlockSpec((1, bm, d), lambda i, r: (i, r, 0)),
                pl.BlockSpec((None, d, n), lambda i, r: (layer, 0, 0)), tab, tab]
    args = [h, w, cos, sin]
    if mode == "rope":
        body = functools.partial(_inproj_rope_kernel, width=n, q_width=ROPE_QW)
    else:
        body = functools.partial(_inproj_nr_kernel, width=n)
        in_specs += [pl.BlockSpec((1, n), lambda i, r: (0, 0)),
                     pl.BlockSpec((LANES, LANES), lambda i, r: (0, 0))]
        args += [gain, seg]
    return pl.pallas_call(
        body,
        grid=(b, l // bm),
        in_specs=in_specs,
        out_specs=pl.BlockSpec((1, bm, n), lambda i, r: (i, r, 0)),
        out_shape=jax.ShapeDtypeStruct((b, l, n), BF16),
        compiler_params=_cparams(("parallel", "parallel")),
        name="inproj_" + mode,
    )(*args)


def _stack_halves(q, rotary=False):
    lane = lax.broadcasted_iota(jnp.int32, q.shape, 1)
    first = ((lane // (HEAD_DIM // 2)) % 2 == 0) if rotary else (lane < HEAD_DIM)
    zero = jnp.zeros_like(q)
    return jnp.concatenate([jnp.where(first, q, zero), jnp.where(first, zero, q)], axis=0)


FLASH_ROWS = 2 * TQ


FLASH_REFS = 6


def _flash_scratch(n_streams):
    stat = pltpu.VMEM((FLASH_ROWS, LANES), F32)
    one = [pltpu.VMEM((FLASH_ROWS, LANES), BF16), stat, stat, stat,
           pltpu.VMEM((FLASH_ROWS, TK), F32), pltpu.VMEM((FLASH_ROWS, TK), BF16)]
    return one * n_streams


def _fold_chunk(stream, s, v, first):
    _, m_s, l_s, acc_s, _, p_s = stream
    tk = s.shape[1]
    m_new = jnp.broadcast_to(jnp.max(s, axis=1, keepdims=True), (FLASH_ROWS, LANES))
    if not first:
        m_old = m_s[...]
        m_new = jnp.maximum(m_old, m_new)
        alpha = jnp.exp2(m_old - m_new)
    p = jnp.exp2(s - jnp.concatenate([m_new] * (tk // LANES), axis=1))
    p_s[:, 0:tk] = p.astype(BF16)
    psum = p[:, 0:LANES]
    for t in range(1, tk // LANES):
        psum = psum + p[:, t * LANES:(t + 1) * LANES]
    pv = _dot(p_s[:, 0:tk], v)
    m_s[...] = m_new
    l_s[...] = psum if first else alpha * l_s[...] + psum
    acc_s[...] = pv if first else alpha * acc_s[...] + pv


def _flash(q_blocks, kv_srcs, n_ctx, n_lat_chunks, scratch):
    ns = len(q_blocks)
    streams = [scratch[FLASH_REFS * i:FLASH_REFS * (i + 1)] for i in range(ns)]
    kv_lanes = [slice(kb * LANES, (kb + 1) * LANES) for _, _, kb in kv_srcs]

    def scores(i, r0, tk):
        qs_s, _, _, _, s_s, _ = streams[i]
        s_s[:, 0:tk] = _dot_nt(qs_s[...], kv_srcs[i][0][0, pl.ds(r0, tk), kv_lanes[i]])

    def accumulate(i, r0, tk, first=False):
        s_s = streams[i][4]
        _fold_chunk(streams[i], s_s[:, 0:tk], kv_srcs[i][1][0, pl.ds(r0, tk), kv_lanes[i]], first)

    def stage(r0, tk, nxt, first=False):
        for i in range(ns):
            accumulate(i, r0, tk, first)
            if nxt is not None:
                scores(i, *nxt)

    def head():
        for i in range(ns):
            streams[i][0][...] = _stack_halves(q_blocks[i](), rotary=True)
            scores(i, 0, n_ctx)
        stage(0, n_ctx, (n_ctx, TK), first=True)

    def body(c, _):
        r0 = pl.multiple_of(n_ctx + c * TK, LANES)
        stage(r0, TK, (r0 + TK, TK))
        return 0

    def latent():
        lax.fori_loop(0, n_lat_chunks - 1, body, 0)
        stage(n_ctx + (n_lat_chunks - 1) * TK, TK, None)

    def results():
        return [acc_s[...] / jnp.sum(l_s[...], axis=1, keepdims=True) for _, _, l_s, acc_s, _, _ in streams]

    return head, latent, results


N_GLB_BLOCKS = GLB_HEADS // 2
N_DIF_BLOCKS = DIF_HEADS


def _dense_kernel(qc_ref, kc_ref, vc_ref, qd_ref, kd_ref, vd_ref, lam_ref, g_ref, oc_ref, od_ref, *scratch,
                  n_ctx, n_lat, n_tiles, lam_init):
    load = lambda ref, c: (lambda: ref[0, :, c * LANES:(c + 1) * LANES])
    q_blocks = [load(qc_ref, c) for c in range(N_GLB_BLOCKS)] + [load(qd_ref, c) for c in range(N_DIF_BLOCKS)]
    kv_srcs = ([(kc_ref, vc_ref, c // 2) for c in range(N_GLB_BLOCKS)]
               + [(kd_ref, vd_ref, c // 2) for c in range(N_DIF_BLOCKS)])
    head, latent, results = _flash(q_blocks, kv_srcs, n_ctx, n_lat // TK, scratch)

    def write_out():
        outs = results()
        lane = lax.broadcasted_iota(jnp.int32, (TQ, LANES), 1)
        for c, o in enumerate(outs[:N_GLB_BLOCKS]):
            oc_ref[0, :, c * LANES:(c + 1) * LANES] = jnp.where(lane < HEAD_DIM, o[0:TQ], o[TQ:2 * TQ]).astype(BF16)
        lp = lam_ref[...]
        lam = (jnp.exp(jnp.sum(lp[0:1] * lp[1:2], axis=1, keepdims=True))
               - jnp.exp(jnp.sum(lp[2:3] * lp[3:4], axis=1, keepdims=True)) + lam_init)
        for c, o in enumerate(outs[N_GLB_BLOCKS:]):
            o = o[0:TQ] - lam * o[TQ:2 * TQ]
            y = o * lax.rsqrt(jnp.mean(o * o, axis=-1, keepdims=True) + EPS) * g_ref[...]
            od_ref[0, :, c * LANES:(c + 1) * LANES] = (y * (1.0 - lam_init)).astype(BF16)

    j = pl.program_id(1)

    @pl.when(j == 0)
    def _():
        head()

    @pl.when((j > 0) & (j < n_tiles))
    def _():
        write_out()
        head()
        latent()

    @pl.when(j == n_tiles)
    def _():
        write_out()


def _dense_mixers(p_nr, p_rope, p_plain, lam_p, sub_g, lam_init, n_ctx):
    b, l, _ = p_nr.shape
    nt = l // TQ
    kern = functools.partial(_dense_kernel, n_ctx=n_ctx, n_lat=l - n_ctx, n_tiles=nt, lam_init=lam_init)
    tile = lambda blk: pl.BlockSpec((1, TQ, 512), lambda i, j: (i, jnp.minimum(j, nt - 1), blk))
    lagged = pl.BlockSpec((1, TQ, 512), lambda i, j: (i, jnp.maximum(j - 1, 0), 0))
    full = lambda blk: pl.BlockSpec((1, l, 256), lambda i, j: (i, 0, blk))
    out = jax.ShapeDtypeStruct((b, l, BRANCH_W), BF16)
    return pl.pallas_call(
        kern,
        grid=(b, nt + 1),
        in_specs=[tile(0), full(2), full(PL_CV),
                  tile(1), full(4), full(PL_DV),
                  pl.BlockSpec((4, HEAD_DIM), lambda i, j: (0, 0)),
                  pl.BlockSpec((1, DIF_DV), lambda i, j: (0, 0))],
        out_specs=[lagged, lagged],
        out_shape=[out, out],
        scratch_shapes=_flash_scratch(N_GLB_BLOCKS + N_DIF_BLOCKS),
        compiler_params=_cparams(("parallel", "arbitrary")),
        name="dense_mixers",
    )(p_nr, p_nr, p_plain, p_rope, p_rope, p_plain, lam_p, sub_g.reshape(1, DIF_DV))


WIN_SPAN = TQ + 2 * WINDOW
WIN_BLOCKS = WIN_HEADS // WIN_KV


def _window_kernel(sink_ref, q_ref, k_ref, v_ref, o_ref, *scratch, n_ctx, total):
    j = pl.program_id(1)
    start = pl.multiple_of(jnp.clip(j * TQ - WINDOW, 0, total - WIN_SPAN), LANES)
    streams = [scratch[3 * g:3 * (g + 1)] for g in range(WIN_BLOCKS)]
    for g, (qs_s, s_s, _) in enumerate(streams):
        qs_s[...] = _stack_halves(q_ref[0, :, g * LANES:(g + 1) * LANES], rotary=True)
        s_s[:, 0:n_ctx] = _dot_nt(qs_s[...], k_ref[0, 0:n_ctx, :])
        s_s[:, n_ctx:] = _dot_nt(qs_s[...], k_ref[0, pl.ds(start, WIN_SPAN), :])

    row = lax.broadcasted_iota(jnp.int32, (2 * TQ, WIN_SPAN), 0)
    qpos = j * TQ + jnp.where(row >= TQ, row - TQ, row)
    kpos = start + lax.broadcasted_iota(jnp.int32, (2 * TQ, WIN_SPAN), 1)
    valid = (kpos >= n_ctx) & (qpos >= n_ctx) & (jnp.abs(kpos - qpos) <= WINDOW)
    srow = lax.broadcasted_iota(jnp.int32, (2 * TQ, LANES), 0)
    lane = lax.broadcasted_iota(jnp.int32, (TQ, LANES), 1)
    width = n_ctx + WIN_SPAN
    for g, (_, s_s, p_s) in enumerate(streams):
        sink = jnp.where(srow < TQ, sink_ref[g], sink_ref[WIN_BLOCKS + g]) * LOG2E
        s_c = s_s[:, 0:n_ctx]
        s_w = jnp.where(valid, s_s[:, n_ctx:], NEG_INF)
        m = jnp.maximum(jnp.maximum(jnp.max(s_c, axis=1, keepdims=True), jnp.max(s_w, axis=1, keepdims=True)), sink)
        p_c = jnp.exp2(s_c - jnp.concatenate([m] * (n_ctx // LANES), axis=1))
        p_w = jnp.exp2(s_w - jnp.concatenate([m] * (WIN_SPAN // LANES), axis=1))
        p_s[:, 0:n_ctx] = p_c.astype(BF16)
        p_s[:, n_ctx:] = p_w.astype(BF16)
        psum = p_c[:, 0:LANES]
        for t in range(1, n_ctx // LANES):
            psum = psum + p_c[:, t * LANES:(t + 1) * LANES]
        for t in range(WIN_SPAN // LANES):
            psum = psum + p_w[:, t * LANES:(t + 1) * LANES]
        den = jnp.sum(psum, axis=1, keepdims=True) + jnp.exp2(sink - m)
        o = (_dot(p_s[:, 0:n_ctx], v_ref[0, 0:n_ctx, :])
             + _dot(p_s[:, n_ctx:width], v_ref[0, pl.ds(start, WIN_SPAN), :])) / den
        o_ref[0, :, g * LANES:(g + 1) * LANES] = jnp.where(lane < HEAD_DIM, o[0:TQ], o[TQ:2 * TQ]).astype(BF16)


def _window_mixer(p_rope, p_plain, sink, n_ctx):
    b, l, _ = p_rope.shape
    kern = functools.partial(_window_kernel, n_ctx=n_ctx, total=l)
    grid_spec = pltpu.PrefetchScalarGridSpec(
        num_scalar_prefetch=1,
        grid=(b, l // TQ),
        in_specs=[pl.BlockSpec((1, TQ, 512), lambda i, j, s: (i, j, 0)),
                  pl.BlockSpec((1, l, LANES), lambda i, j, s: (i, 0, 10)),
                  pl.BlockSpec((1, l, LANES), lambda i, j, s: (i, 0, PL_BV))],
        out_specs=pl.BlockSpec((1, TQ, 512), lambda i, j, s: (i, j, 0)),
        scratch_shapes=[pltpu.VMEM((2 * TQ, LANES), BF16), pltpu.VMEM((2 * TQ, n_ctx + WIN_SPAN), F32),
                        pltpu.VMEM((2 * TQ, n_ctx + WIN_SPAN), BF16)] * WIN_BLOCKS,
    )
    return pl.pallas_call(
        kern,
        grid_spec=grid_spec,
        out_shape=jax.ShapeDtypeStruct((b, l, BRANCH_W), BF16),
        compiler_params=_cparams(("parallel", "arbitrary")),
        name="window_mixer",
    )(sink, p_rope, p_rope, p_plain)


GLA_GROUP = 4


def _gla_kernel(q_ref, k_ref, v_ref, g_ref, wg_ref, bg_ref, gn_ref, o_ref, accf_s, accb_s, st_s, *, total, n_ctx):
    ck = GLA_CHUNK
    n_groups = total // (ck * GLA_GROUP)
    ctx_groups = n_ctx // (ck * GLA_GROUP)
    hw = GLA_HEADS * GLA_DK
    acc_refs = (accf_s, accb_s)

    st_s[...] = jnp.zeros_like(st_s)

    rr = lax.broadcasted_iota(jnp.int32, (ck, ck), 0)
    cc = lax.broadcasted_iota(jnp.int32, (ck, ck), 1)
    tri = (rr >= cc, rr <= cc)
    tri_b = tuple(jnp.where(t, 1.0, 0.0).astype(BF16) for t in tri)
    tri2 = tuple(jnp.concatenate([t, t], axis=0) for t in tri)
    lane = lax.broadcasted_iota(jnp.int32, (LANES, LANES), 1)

    def body(i, _):
        gf = i
        gb = jnp.where(i < ctx_groups, ctx_groups - 1 - i, n_groups - 1 - (i - ctx_groups))
        chains = []
        for u in range(GLA_GROUP):
            for d, grp in ((0, gf), (1, gb)):
                c = grp * GLA_GROUP + (u if d == 0 else GLA_GROUP - 1 - u)
                chains.append((d, pl.multiple_of(c * ck, ck)))

        gate = [_dot(g_ref[0, pl.ds(r0, ck), :], wg_ref[:, d * hw:(d + 1) * hw]) + bg_ref[:, d * hw:(d + 1) * hw]
                for d, r0 in chains]
        cums = []
        for (d, r0), y in zip(chains, gate):
            la = (jnp.minimum(y, 0.0) - jnp.log(1.0 + jnp.exp(-jnp.abs(y)))) * (1.0 / GLA_TAU)
            hi = la.astype(BF16)
            lo = (la - hi.astype(F32)).astype(BF16)
            cums.append(_dot(tri_b[d], hi) + _dot(tri_b[d], lo))

        prepped = []
        for (d, r0), cum in zip(chains, cums):
            tot = cum[ck - 1:ck, :] if d == 0 else cum[0:1, :]
            qf = q_ref[0, pl.ds(r0, ck), :].astype(F32)
            kf = k_ref[0, pl.ds(r0, ck), :].astype(F32)
            qt = (qf * jnp.exp(cum)).astype(BF16)
            kt = (kf * jnp.exp(-cum)).astype(BF16)
            kw = (kf * jnp.exp(tot - cum)).astype(BF16)
            qs, a2 = [], []
            for p in range(2):
                sl = slice(p * LANES, (p + 1) * LANES)
                qs.append(_stack_halves(qt[:, sl]))
                a2.append(jnp.where(tri2[d], _dot_nt(qs[p], kt[:, sl]), 0.0).astype(BF16))
            prepped.append((qs, a2, kw, jnp.exp(tot)))

        intra, update = [], []
        for (d, r0), (qs, a2, kw, dec) in zip(chains, prepped):
            av, upd = [], []
            for p in range(2):
                sl = slice(p * LANES, (p + 1) * LANES)
                halves = []
                for hh in range(2):
                    vh = v_ref[0, pl.ds(r0, ck), (2 * p + hh) * GLA_DV:(2 * p + hh + 1) * GLA_DV]
                    av.append(_dot(a2[p][hh * ck:(hh + 1) * ck], vh))
                    halves.append(_dot_tn(vh, kw[:, sl]))
                upd.append(jnp.where(lane < GLA_DK, halves[0], halves[1]))
            intra.append(av)
            update.append(upd)

        states = [[st_s[2 * d + p] for p in range(2)] for d in range(2)]
        for (d, r0), (qs, a2, kw, dec), av, upd in zip(chains, prepped, intra, update):
            for p in range(2):
                inter = _dot_nt(qs[p], states[d][p].astype(BF16))
                for hh in range(2):
                    vs = slice((2 * p + hh) * GLA_DV, (2 * p + hh + 1) * GLA_DV)
                    acc_refs[d][pl.ds(r0, ck), vs] = av[2 * p + hh] + inter[hh * ck:(hh + 1) * ck]
                states[d][p] = states[d][p] * dec[:, p * LANES:(p + 1) * LANES] + upd[p]
        for d in range(2):
            for p in range(2):
                st_s[2 * d + p] = states[d][p]
        return 0

    lax.fori_loop(0, n_groups, body, 0)

    def finish(i, _):
        r0 = pl.multiple_of(i * TQ, TQ)
        for h in range(GLA_HEADS):
            vs = slice(h * GLA_DV, (h + 1) * GLA_DV)
            o = accf_s[pl.ds(r0, TQ), vs] + accb_s[pl.ds(r0, TQ), vs]
            y = o * lax.rsqrt(jnp.mean(o * o, axis=-1, keepdims=True) + EPS) * gn_ref[:, vs]
            o_ref[0, pl.ds(r0, TQ), vs] = y.astype(BF16)
        return 0

    lax.fori_loop(0, total // TQ, finish, 0)


def _gla_mixer(p_plain, wg, bg, gn, n_ctx):
    b, l, _ = p_plain.shape
    kern = functools.partial(_gla_kernel, total=l, n_ctx=n_ctx)
    const = lambda i: (0, 0)
    return pl.pallas_call(
        kern,
        grid=(b,),
        in_specs=[pl.BlockSpec((1, l, 256), lambda i: (i, 0, PL_AQ)),
                  pl.BlockSpec((1, l, 256), lambda i: (i, 0, PL_AK)),
                  pl.BlockSpec((1, l, 512), lambda i: (i, 0, PL_AV)),
                  pl.BlockSpec((1, l, LANES), lambda i: (i, 0, PL_AG)),
                  pl.BlockSpec((LANES, 512), const),
                  pl.BlockSpec((1, 512), const),
                  pl.BlockSpec((1, 512), const)],
        out_specs=pl.BlockSpec((1, l, BRANCH_W), lambda i: (i, 0, 0)),
        out_shape=jax.ShapeDtypeStruct((b, l, BRANCH_W), BF16),
        scratch_shapes=[pltpu.VMEM((l, BRANCH_W), F32), pltpu.VMEM((l, BRANCH_W), F32),
                        pltpu.VMEM((4, GLA_DV, LANES), F32)],
        compiler_params=_cparams(("parallel",)),
        name="gla_mixer",
    )(p_plain, p_plain, p_plain, p_plain, wg, bg, gn)


def _merge_kernel(h_ref, oa_ref, ob_ref, oc_ref, od_ref, za_ref, zb_ref, zc_ref, zd_ref, *rest, alpha, off):
    *x_refs, mod_ref, wm_ref, wup_ref, wout_ref, lng_ref, lnb_ref, out_ref = rest
    h = h_ref[0]
    acc = None
    for i, (o_ref, z_ref) in enumerate(((oa_ref, za_ref), (ob_ref, zb_ref), (oc_ref, zc_ref), (od_ref, zd_ref))):
        z = z_ref[0].astype(F32)
        br = (o_ref[0].astype(F32) * (z * _sigmoid(z))).astype(BF16)
        term = _sigmoid(_dot(h, wm_ref[i])) * _dot(br, wup_ref[i])
        acc = term if acc is None else acc + term
    y = _dot(acc.astype(BF16), wout_ref[...])
    r = alpha * _stream_tile(x_refs, off) + mod_ref[0, 0, 2:3, :] * y
    mu = jnp.mean(r, axis=-1, keepdims=True)
    rc = r - mu
    var = jnp.mean(rc * rc, axis=-1, keepdims=True)
    out_ref[0] = rc * lax.rsqrt(var + EPS) * lng_ref[...] + lnb_ref[...]


def _merge(h, outs, p_plain, xs, mod, wm, wup, wout, ln_g, ln_b, layer, skip_ctx):
    b, l, d = h.shape
    off = 1 if skip_ctx else 0
    nt = l // TQ - off
    row = lambda i, j: (i, j + off, 0)
    zspec = lambda blk: pl.BlockSpec((1, TQ, BRANCH_W), lambda i, j: (i, j + off, blk))
    c2 = lambda i, j: (0, 0)
    in_specs = ([pl.BlockSpec((1, TQ, d), row)]
                + [pl.BlockSpec((1, TQ, BRANCH_W), row)] * 4
                + [zspec(PL_AZ), zspec(PL_BZ), zspec(PL_CZ), zspec(PL_DZ)]
                + _stream_specs(xs, d, off)
                + [pl.BlockSpec((1, 1, 3, d), lambda i, j: (i, jnp.minimum(j + off, 1), 0, 0)),
                   pl.BlockSpec((None, 4, d, d), lambda i, j: (layer, 0, 0, 0)),
                   pl.BlockSpec((None, 4, BRANCH_W, d), lambda i, j: (layer, 0, 0, 0)),
                   pl.BlockSpec((None, d, d), lambda i, j: (layer, 0, 0)),
                   pl.BlockSpec((1, d), c2), pl.BlockSpec((1, d), c2)])
    return pl.pallas_call(
        functools.partial(_merge_kernel, alpha=(2 * DEPTH) ** 0.25, off=off),
        grid=(b, nt),
        in_specs=in_specs,
        out_specs=pl.BlockSpec((1, TQ, d), lambda i, j: (i, j, 0)),
        out_shape=jax.ShapeDtypeStruct((b, nt * TQ, d), F32),
        compiler_params=_cparams(("parallel", "parallel")),
        name="merge",
    )(h, *outs, p_plain, p_plain, p_plain, p_plain, *xs, mod, wm, wup, wout,
      ln_g.reshape(1, d), ln_b.reshape(1, d))


def _perm_window(a, axis):
    a = jnp.moveaxis(a, axis, -1)
    lead = a.shape[:-1]
    a = a.reshape(lead + (WIN_KV, WIN_HEADS // WIN_KV, HEAD_DIM)).swapaxes(-3, -2).reshape(lead + (BRANCH_W,))
    return jnp.moveaxis(a, -1, axis)


def _perm_global(a, axis):
    a = jnp.moveaxis(a, axis, -1)
    lead = a.shape[:-1]
    a = a.reshape(lead + (GLB_KV // 2, 2, GLB_HEADS // GLB_KV, HEAD_DIM)).swapaxes(-3, -2).reshape(lead + (BRANCH_W,))
    return jnp.moveaxis(a, -1, axis)


def _rotary_lanes(a):
    lead = a.shape[:-1]
    a = a.reshape(lead + (a.shape[-1] // LANES, 2, 2, 2, ROPE_HALF))
    return jnp.moveaxis(a, -2, -4).reshape(lead + (-1,))


def _split_w_in(w):
    w = w.astype(BF16)
    cols = lambda off, n: w[..., off:off + n]
    pad = jnp.zeros(w.shape[:-1] + (LANES - 2 * GLA_RANK,), w.dtype)
    w_plain = jnp.concatenate([
        cols(_A_V, 512), cols(_A_Z, 512), _perm_window(cols(_B_Z, 512), -1), _perm_global(cols(_C_Z, 512), -1),
        cols(_D_Z, 512), cols(_A_Q, 256) * GLA_DK ** -0.5, cols(_A_K, 256), cols(_C_V, 256), cols(_D_V, 256),
        cols(_B_V, 128), cols(_A_GF, 2 * GLA_RANK), pad], axis=-1)
    w_rope = _rotary_lanes(jnp.concatenate([
        _perm_window(cols(_B_Q, 512), -1) * HEAD_DIM ** -0.5, cols(_D_Q, 512) * HEAD_DIM ** -0.5,
        cols(_D_K, 256), cols(_B_K, 128)], axis=-1))
    w_nr = _rotary_lanes(jnp.concatenate([_perm_global(cols(_C_Q, 512), -1), cols(_C_K, 256)], axis=-1))
    return w_plain, w_rope, w_nr


def _rope_tables(n_ctx, n_lat):
    t = np.arange(n_lat)
    freqs = ROPE_BASE ** (-np.arange(ROPE_HALF, dtype=np.float32) / ROPE_HALF)
    pos = np.stack([(t // GRID_W).astype(np.float32), (t % GRID_W).astype(np.float32)], axis=1)
    ang = jnp.asarray(pos[:, :, None] * freqs[None, None, :], F32).reshape(n_lat, 2 * ROPE_HALF)
    cos = jnp.tile(jnp.cos(ang), (1, LANES // (2 * ROPE_HALF)))
    sign = np.where(np.arange(LANES) < LANES // 2, -1.0, 1.0).astype(np.float32)
    sin = jnp.tile(jnp.sin(ang), (1, LANES // (2 * ROPE_HALF))) * sign
    cos_t = jnp.concatenate([jnp.ones((n_ctx, LANES), F32), cos], axis=0)
    sin_t = jnp.concatenate([jnp.zeros((n_ctx, LANES), F32), sin], axis=0)
    return cos_t, sin_t


def kernel(x, c, ctx, c_ctx, w_ada, b_ada, w_in, gla_w_gate, gla_b_gate, gla_norm, win_sink, glb_q_norm,
           glb_k_norm, diff_lambda, diff_norm, w_merge, w_up, w_out, ln_g, ln_b):
    b, n_lat, d = x.shape
    n_ctx = ctx.shape[1]
    assert n_ctx == TQ and n_lat % TK == 0 and d == 1024
    xs = (ctx, x)
    cs = jnp.zeros((16, d), F32).at[0:b].set(c).at[b].set(c_ctx)
    cos_t, sin_t = _rope_tables(n_ctx, n_lat)
    head_of_lane = (np.arange(LANES) // (HEAD_DIM // 2)) % 2
    seg = jnp.asarray(head_of_lane[:, None] == head_of_lane[None, :], BF16)

    w_plain, w_rope, w_nr = _split_w_in(w_in)
    gain_nr = _rotary_lanes(jnp.concatenate([jnp.tile(glb_q_norm * (HEAD_DIM ** -0.5 * LOG2E), (1, GLB_HEADS)),
                                             jnp.tile(glb_k_norm, (1, GLB_KV))], axis=1))
    wg = jnp.zeros((DEPTH, LANES, 2 * GLA_HEADS * GLA_DK), F32)
    wg = wg.at[:, 0:GLA_RANK, 0:256].set(gla_w_gate[:, 0]).at[:, GLA_RANK:2 * GLA_RANK, 256:512].set(gla_w_gate[:, 1])
    wg = wg.astype(BF16)
    wup = w_up.astype(BF16)
    wup = jnp.stack([wup[:, 0], _perm_window(wup[:, 1], -2), _perm_global(wup[:, 2], -2), wup[:, 3]], axis=1)
    wm, wout = w_merge.astype(BF16), w_out.astype(BF16)

    ada = _ada(cs, w_ada, b_ada)
    mod_x = ada[:, 0:b].reshape(DEPTH, b, 1, 3, d)
    mod_c = jnp.broadcast_to(ada[:, b].reshape(DEPTH, 1, 1, 3, d), (DEPTH, b, 1, 3, d))
    mods = jnp.concatenate([mod_c, mod_x], axis=2)

    for layer in range(DEPTH):
        last = layer == DEPTH - 1
        lam_init = 0.8 - 0.6 * math.exp(-0.3 * layer)
        mod = mods[layer]
        h = _ln_mod(xs, mod)
        p_plain = _inproj(h, w_plain, layer, "plain")
        p_rope = _inproj(h, w_rope, layer, "rope", cos_t, sin_t)
        p_nr = _inproj(h, w_nr, layer, "nr", cos_t, sin_t, gain_nr[layer].reshape(1, NR_W), seg)

        o_a = _gla_mixer(p_plain, wg[layer], gla_b_gate[layer].reshape(1, 512), gla_norm[layer].reshape(1, BRANCH_W),
                         n_ctx)
        o_b = _window_mixer(p_rope, p_plain, win_sink[layer], n_ctx)
        o_c, o_d = _dense_mixers(p_nr, p_rope, p_plain, diff_lambda[layer], diff_norm[layer], lam_init, n_ctx)

        xs = (_merge(h, (o_a, o_b, o_c, o_d), p_plain, xs, mod, wm, wup, wout, ln_g[layer], ln_b[layer], layer,
                     skip_ctx=last),)
    return xs[0]
```

```python
import functools
import math

import numpy as np
import jax
import jax.numpy as jnp
from jax import lax
from jax.experimental import pallas as pl
from jax.experimental.pallas import tpu as pltpu

F32 = jnp.float32
BF16 = jnp.bfloat16

DEPTH = 2
GRID_W = 64
HEAD_DIM = 64
ROPE_HALF = HEAD_DIM // 4
ROPE_BASE = 10000.0
EPS = 1e-6
NEG_INF = -1e30
GLA_HEADS, GLA_DK, GLA_DV, GLA_RANK, GLA_TAU, GLA_CHUNK = 4, 64, 128, 16, 16.0, 64
WIN_HEADS, WIN_KV, WINDOW = 8, 2, 128
GLB_HEADS, GLB_KV = 8, 4
DIF_HEADS, DIF_KV, DIF_DV = 4, 2, 128
BRANCH_W = 512

LANES = 128
TQ = 256
TK = 512
VMEM_LIMIT = 56 << 20

_A_Q, _A_K, _A_V, _A_GF, _A_Z = 0, 256, 512, 1024, 1056
_B_Q, _B_K, _B_V, _B_Z = 1568, 2080, 2208, 2336
_C_Q, _C_K, _C_V, _C_Z = 2848, 3360, 3616, 3872
_D_Q, _D_K, _D_V, _D_Z = 4384, 4896, 5152, 5408

PL_AV, PL_AZ, PL_BZ, PL_CZ, PL_DZ = 0, 1, 2, 3, 4
PL_AQ, PL_AK, PL_CV, PL_DV = 10, 11, 12, 13
PL_BV, PL_AG = 28, 29
PLAIN_W, ROPE_W, NR_W = 3840, 1408, 768
ROPE_QW = 1024
LOG2E = math.log2(math.e)


def _cparams(sem):
    return pltpu.CompilerParams(dimension_semantics=sem, vmem_limit_bytes=VMEM_LIMIT)


def _sigmoid(x):
    return 1.0 / (1.0 + jnp.exp(-x))


def _dot(a, b):
    return jnp.dot(a, b, preferred_element_type=F32)


def _dot_nt(a, b):
    return lax.dot_general(a, b, (((1,), (1,)), ((), ())), preferred_element_type=F32)


def _dot_tn(a, b):
    return lax.dot_general(a, b, (((0,), (0,)), ((), ())), preferred_element_type=F32)


def _ada_kernel(c_ref, w_ref, b_ref, o_ref):
    c = c_ref[...]
    s = c * _sigmoid(c)
    o_ref[...] = jnp.dot(s, w_ref[...], preferred_element_type=F32,
                         precision=lax.Precision.HIGHEST) + b_ref[...]


def _ada(cs, w_ada, b_ada):
    r, d = cs.shape
    depth, _, n = w_ada.shape
    bn = 1024
    return pl.pallas_call(
        _ada_kernel,
        grid=(depth, n // bn),
        in_specs=[pl.BlockSpec((r, d), lambda l, j: (0, 0)),
                  pl.BlockSpec((None, d, bn), lambda l, j: (l, 0, j)),
                  pl.BlockSpec((None, 1, bn), lambda l, j: (l, 0, j))],
        out_specs=pl.BlockSpec((None, r, bn), lambda l, j: (l, 0, j)),
        out_shape=jax.ShapeDtypeStruct((depth, r, n), F32),
        compiler_params=_cparams(("arbitrary", "arbitrary")),
        name="ada",
    )(cs, w_ada, b_ada.reshape(depth, 1, n))


def _stream_specs(xs, d, off=0):
    if len(xs) == 1:
        return [pl.BlockSpec((1, TQ, d), lambda i, j: (i, j + off, 0))]
    return [pl.BlockSpec((1, TQ, d), lambda i, j: (i, 0, 0)),
            pl.BlockSpec((1, TQ, d), lambda i, j: (i, jnp.maximum(j + off - 1, 0), 0))]


def _stream_tile(refs, off=0):
    if len(refs) == 1:
        return refs[0][0]
    return jnp.where(pl.program_id(1) + off == 0, refs[0][0], refs[1][0])


def _ln_kernel(*refs):
    *x_refs, mod_ref, h_ref = refs
    x = _stream_tile(x_refs)
    mu = jnp.mean(x, axis=-1, keepdims=True)
    xc = x - mu
    var = jnp.mean(xc * xc, axis=-1, keepdims=True)
    y = xc * lax.rsqrt(var + EPS)
    shift = mod_ref[0, 0, 0:1, :]
    scale = mod_ref[0, 0, 1:2, :]
    h_ref[0] = (y * (1.0 + scale) + shift).astype(BF16)


def _ln_mod(xs, mod):
    b, _, d = xs[0].shape
    l = sum(a.shape[1] for a in xs)
    return pl.pallas_call(
        _ln_kernel,
        grid=(b, l // TQ),
        in_specs=_stream_specs(xs, d) + [pl.BlockSpec((1, 1, 3, d), lambda i, j: (i, jnp.minimum(j, 1), 0, 0))],
        out_specs=pl.BlockSpec((1, TQ, d), lambda i, j: (i, j, 0)),
        out_shape=jax.ShapeDtypeStruct((b, l, d), BF16),
        compiler_params=_cparams(("parallel", "parallel")),
        name="ln_mod",
    )(*xs, mod)


def _rope(y, cos, sin):
    return y * cos + pltpu.roll(y, LANES // 2, 1) * sin


def _inproj_plain_kernel(h_ref, w_ref, o_ref):
    o_ref[0] = _dot(h_ref[0], w_ref[...]).astype(BF16)


def _inproj_rope_kernel(h_ref, w_ref, cos_ref, sin_ref, o_ref, *, width, q_width):
    acc = _dot(h_ref[0], w_ref[...])
    cos, sin = cos_ref[...], sin_ref[...]
    for c in range(width // LANES):
        y = _rope(acc[:, c * LANES:(c + 1) * LANES], cos, sin)
        if c * LANES < q_width:
            y = y * LOG2E
        o_ref[0, :, c * LANES:(c + 1) * LANES] = y.astype(BF16)


def _inproj_nr_kernel(h_ref, w_ref, cos_ref, sin_ref, gain_ref, seg_ref, o_ref, *, width):
    acc = _dot(h_ref[0], w_ref[...])
    cos, sin = cos_ref[...], sin_ref[...]
    for c in range(width // LANES):
        y = acc[:, c * LANES:(c + 1) * LANES]
        ss = _dot((y * y).astype(BF16), seg_ref[...])
        y = y * lax.rsqrt(ss * (1.0 / HEAD_DIM) + EPS) * gain_ref[:, c * LANES:(c + 1) * LANES]
        o_ref[0, :, c * LANES:(c + 1) * LANES] = _rope(y, cos, sin).astype(BF16)


def _inproj(h, w, layer, mode, cos=None, sin=None, gain=None, seg=None):
    b, l, d = h.shape
    n = w.shape[2]
    bm = l // 2
    if mode == "plain":
        bn = 768
        return pl.pallas_call(
            _inproj_plain_kernel,
            grid=(b, l // bm, n // bn),
            in_specs=[pl.BlockSpec((1, bm, d), lambda i, r, j: (i, r, 0)),
                      pl.BlockSpec((None, d, bn), lambda i, r, j: (layer, 0, j))],
            out_specs=pl.BlockSpec((1, bm, bn), lambda i, r, j: (i, r, j)),
            out_shape=jax.ShapeDtypeStruct((b, l, n), BF16),
            compiler_params=_cparams(("parallel", "parallel", "arbitrary")),
            name="inproj_plain",
        )(h, w)
    tab = pl.BlockSpec((bm, LANES), lambda i, r: (r, 0))
    in_specs = [pl.BlockSpec((1, bm, d), lambda i, r: (i, r, 0)),
                pl.BlockSpec((None, d, n), lambda i, r: (layer, 0, 0)), tab, tab]
    args = [h, w, cos, sin]
    if mode == "rope":
        body = functools.partial(_inproj_rope_kernel, width=n, q_width=ROPE_QW)
    else:
        body = functools.partial(_inproj_nr_kernel, width=n)
        in_specs += [pl.BlockSpec((1, n), lambda i, r: (0, 0)),
                     pl.BlockSpec((LANES, LANES), lambda i, r: (0, 0))]
        args += [gain, seg]
    return pl.pallas_call(
        body,
        grid=(b, l // bm),
        in_specs=in_specs,
        out_specs=pl.BlockSpec((1, bm, n), lambda i, r: (i, r, 0)),
        out_shape=jax.ShapeDtypeStruct((b, l, n), BF16),
        compiler_params=_cparams(("parallel", "parallel")),
        name="inproj_" + mode,
    )(*args)


def _stack_halves(q, rotary=False):
    lane = lax.broadcasted_iota(jnp.int32, q.shape, 1)
    first = ((lane // (HEAD_DIM // 2)) % 2 == 0) if rotary else (lane < HEAD_DIM)
    zero = jnp.zeros_like(q)
    return jnp.concatenate([jnp.where(first, q, zero), jnp.where(first, zero, q)], axis=0)


FLASH_ROWS = 2 * TQ


FLASH_REFS = 6


def _flash_scratch(n_streams):
    stat = pltpu.VMEM((FLASH_ROWS, LANES), F32)
    one = [pltpu.VMEM((FLASH_ROWS, LANES), BF16), stat, stat, stat,
           pltpu.VMEM((FLASH_ROWS, TK), F32), pltpu.VMEM((FLASH_ROWS, TK), BF16)]
    return one * n_streams


def _fold_chunk(stream, s, v, first):
    _, m_s, l_s, acc_s, _, p_s = stream
    tk = s.shape[1]
    m_new = jnp.broadcast_to(jnp.max(s, axis=1, keepdims=True), (FLASH_ROWS, LANES))
    if not first:
        m_old = m_s[...]
        m_new = jnp.maximum(m_old, m_new)
        alpha = jnp.exp2(m_old - m_new)
    p = jnp.exp2(s - jnp.concatenate([m_new] * (tk // LANES), axis=1))
    p_s[:, 0:tk] = p.astype(BF16)
    psum = p[:, 0:LANES]
    for t in range(1, tk // LANES):
        psum = psum + p[:, t * LANES:(t + 1) * LANES]
    pv = _dot(p_s[:, 0:tk], v)
    m_s[...] = m_new
    l_s[...] = psum if first else alpha * l_s[...] + psum
    acc_s[...] = pv if first else alpha * acc_s[...] + pv


def _flash(q_blocks, kv_srcs, n_ctx, n_lat_chunks, scratch):
    ns = len(q_blocks)
    streams = [scratch[FLASH_REFS * i:FLASH_REFS * (i + 1)] for i in range(ns)]
    kv_lanes = [slice(kb * LANES, (kb + 1) * LANES) for _, _, kb in kv_srcs]

    def scores(i, r0, tk):
        qs_s, _, _, _, s_s, _ = streams[i]
        s_s[:, 0:tk] = _dot_nt(qs_s[...], kv_srcs[i][0][0, pl.ds(r0, tk), kv_lanes[i]])

    def accumulate(i, r0, tk, first=False):
        s_s = streams[i][4]
        _fold_chunk(streams[i], s_s[:, 0:tk], kv_srcs[i][1][0, pl.ds(r0, tk), kv_lanes[i]], first)

    def stage(r0, tk, nxt, first=False):
        for i in range(ns):
            accumulate(i, r0, tk, first)
            if nxt is not None:
                scores(i, *nxt)

    def head():
        for i in range(ns):
            streams[i][0][...] = _stack_halves(q_blocks[i](), rotary=True)
            scores(i, 0, n_ctx)
        stage(0, n_ctx, (n_ctx, TK), first=True)

    def body(c, _):
        r0 = pl.multiple_of(n_ctx + c * TK, LANES)
        stage(r0, TK, (r0 + TK, TK))
        return 0

    def latent():
        lax.fori_loop(0, n_lat_chunks - 1, body, 0)
        stage(n_ctx + (n_lat_chunks - 1) * TK, TK, None)

    def results():
        return [acc_s[...] / jnp.sum(l_s[...], axis=1, keepdims=True) for _, _, l_s, acc_s, _, _ in streams]

    return head, latent, results


N_GLB_BLOCKS = GLB_HEADS // 2
N_DIF_BLOCKS = DIF_HEADS


def _dense_kernel(qc_ref, kc_ref, vc_ref, qd_ref, kd_ref, vd_ref, lam_ref, g_ref, oc_ref, od_ref, *scratch,
                  n_ctx, n_lat, n_tiles, lam_init):
    load = lambda ref, c: (lambda: ref[0, :, c * LANES:(c + 1) * LANES])
    q_blocks = [load(qc_ref, c) for c in range(N_GLB_BLOCKS)] + [load(qd_ref, c) for c in range(N_DIF_BLOCKS)]
    kv_srcs = ([(kc_ref, vc_ref, c // 2) for c in range(N_GLB_BLOCKS)]
               + [(kd_ref, vd_ref, c // 2) for c in range(N_DIF_BLOCKS)])
    head, latent, results = _flash(q_blocks, kv_srcs, n_ctx, n_lat // TK, scratch)

    def write_out():
        outs = results()
        lane = lax.broadcasted_iota(jnp.int32, (TQ, LANES), 1)
        for c, o in enumerate(outs[:N_GLB_BLOCKS]):
            oc_ref[0, :, c * LANES:(c + 1) * LANES] = jnp.where(lane < HEAD_DIM, o[0:TQ], o[TQ:2 * TQ]).astype(BF16)
        lp = lam_ref[...]
        lam = (jnp.exp(jnp.sum(lp[0:1] * lp[1:2], axis=1, keepdims=True))
               - jnp.exp(jnp.sum(lp[2:3] * lp[3:4], axis=1, keepdims=True)) + lam_init)
        for c, o in enumerate(outs[N_GLB_BLOCKS:]):
            o = o[0:TQ] - lam * o[TQ:2 * TQ]
            y = o * lax.rsqrt(jnp.mean(o * o, axis=-1, keepdims=True) + EPS) * g_ref[...]
            od_ref[0, :, c * LANES:(c + 1) * LANES] = (y * (1.0 - lam_init)).astype(BF16)

    j = pl.program_id(1)

    @pl.when(j == 0)
    def _():
        head()

    @pl.when((j > 0) & (j < n_tiles))
    def _():
        write_out()
        head()
        latent()

    @pl.when(j == n_tiles)
    def _():
        write_out()


def _dense_mixers(p_nr, p_rope, p_plain, lam_p, sub_g, lam_init, n_ctx):
    b, l, _ = p_nr.shape
    nt = l // TQ
    kern = functools.partial(_dense_kernel, n_ctx=n_ctx, n_lat=l - n_ctx, n_tiles=nt, lam_init=lam_init)
    tile = lambda blk: pl.BlockSpec((1, TQ, 512), lambda i, j: (i, jnp.minimum(j, nt - 1), blk))
    lagged = pl.BlockSpec((1, TQ, 512), lambda i, j: (i, jnp.maximum(j - 1, 0), 0))
    full = lambda blk: pl.BlockSpec((1, l, 256), lambda i, j: (i, 0, blk))
    out = jax.ShapeDtypeStruct((b, l, BRANCH_W), BF16)
    return pl.pallas_call(
        kern,
        grid=(b, nt + 1),
        in_specs=[tile(0), full(2), full(PL_CV),
                  tile(1), full(4), full(PL_DV),
                  pl.BlockSpec((4, HEAD_DIM), lambda i, j: (0, 0)),
                  pl.BlockSpec((1, DIF_DV), lambda i, j: (0, 0))],
        out_specs=[lagged, lagged],
        out_shape=[out, out],
        scratch_shapes=_flash_scratch(N_GLB_BLOCKS + N_DIF_BLOCKS),
        compiler_params=_cparams(("parallel", "arbitrary")),
        name="dense_mixers",
    )(p_nr, p_nr, p_plain, p_rope, p_rope, p_plain, lam_p, sub_g.reshape(1, DIF_DV))


WIN_SPAN = TQ + 2 * WINDOW
WIN_BLOCKS = WIN_HEADS // WIN_KV


def _window_start(j, total):
    return jnp.clip(j * TQ - WINDOW, 0, total - WIN_SPAN)


def _window_bias(n_ctx, total):
    row = np.arange(2 * TQ)[:, None] % TQ
    col = np.arange(WIN_SPAN)[None, :]
    patterns, index = [], []
    for j in range(total // TQ):
        start = int(np.clip(j * TQ - WINDOW, 0, total - WIN_SPAN))
        qpos, kpos = j * TQ + row, start + col
        valid = (kpos >= n_ctx) & (qpos >= n_ctx) & (np.abs(kpos - qpos) <= WINDOW)
        bias = np.where(valid, 0.0, NEG_INF).astype(np.float32)
        for k, seen in enumerate(patterns):
            if np.array_equal(seen, bias):
                index.append(k)
                break
        else:
            index.append(len(patterns))
            patterns.append(bias)
    return np.stack(patterns), np.asarray(index, np.int32)


def _window_kernel(pattern_ref, sink_ref, q_ref, k_ref, v_ref, bias_ref, o_ref, *scratch, n_ctx, total):
    j = pl.program_id(1)
    start = pl.multiple_of(_window_start(j, total), LANES)
    streams = [scratch[3 * g:3 * (g + 1)] for g in range(WIN_BLOCKS)]
    for g, (qs_s, s_s, _) in enumerate(streams):
        qs_s[...] = _stack_halves(q_ref[0, :, g * LANES:(g + 1) * LANES], rotary=True)
        s_s[:, 0:n_ctx] = _dot_nt(qs_s[...], k_ref[0, 0:n_ctx, :])
        s_s[:, n_ctx:] = _dot_nt(qs_s[...], k_ref[0, pl.ds(start, WIN_SPAN), :])

    srow = lax.broadcasted_iota(jnp.int32, (2 * TQ, LANES), 0)
    lane = lax.broadcasted_iota(jnp.int32, (TQ, LANES), 1)
    width = n_ctx + WIN_SPAN
    for g, (_, s_s, p_s) in enumerate(streams):
        sink = jnp.where(srow < TQ, sink_ref[g], sink_ref[WIN_BLOCKS + g]) * LOG2E
        s_c = s_s[:, 0:n_ctx]
        s_w = s_s[:, n_ctx:] + bias_ref[0]
        m = jnp.maximum(jnp.maximum(jnp.max(s_c, axis=1, keepdims=True), jnp.max(s_w, axis=1, keepdims=True)), sink)
        p_c = jnp.exp2(s_c - jnp.concatenate([m] * (n_ctx // LANES), axis=1))
        p_w = jnp.exp2(s_w - jnp.concatenate([m] * (WIN_SPAN // LANES), axis=1))
        p_s[:, 0:n_ctx] = p_c.astype(BF16)
        p_s[:, n_ctx:] = p_w.astype(BF16)
        psum = p_c[:, 0:LANES]
        for t in range(1, n_ctx // LANES):
            psum = psum + p_c[:, t * LANES:(t + 1) * LANES]
        for t in range(WIN_SPAN // LANES):
            psum = psum + p_w[:, t * LANES:(t + 1) * LANES]
        den = jnp.sum(psum, axis=1, keepdims=True) + jnp.exp2(sink - m)
        o = (_dot(p_s[:, 0:n_ctx], v_ref[0, 0:n_ctx, :])
             + _dot(p_s[:, n_ctx:width], v_ref[0, pl.ds(start, WIN_SPAN), :])) / den
        o_ref[0, :, g * LANES:(g + 1) * LANES] = jnp.where(lane < HEAD_DIM, o[0:TQ], o[TQ:2 * TQ]).astype(BF16)


def _window_mixer(p_rope, p_plain, sink, n_ctx):
    b, l, _ = p_rope.shape
    kern = functools.partial(_window_kernel, n_ctx=n_ctx, total=l)
    bias, pattern = _window_bias(n_ctx, l)
    grid_spec = pltpu.PrefetchScalarGridSpec(
        num_scalar_prefetch=2,
        grid=(b, l // TQ),
        in_specs=[pl.BlockSpec((1, TQ, 512), lambda i, j, pat, s: (i, j, 0)),
                  pl.BlockSpec((1, l, LANES), lambda i, j, pat, s: (i, 0, 10)),
                  pl.BlockSpec((1, l, LANES), lambda i, j, pat, s: (i, 0, PL_BV)),
                  pl.BlockSpec((1, 2 * TQ, WIN_SPAN), lambda i, j, pat, s: (pat[j], 0, 0))],
        out_specs=pl.BlockSpec((1, TQ, 512), lambda i, j, pat, s: (i, j, 0)),
        scratch_shapes=[pltpu.VMEM((2 * TQ, LANES), BF16), pltpu.VMEM((2 * TQ, n_ctx + WIN_SPAN), F32),
                        pltpu.VMEM((2 * TQ, n_ctx + WIN_SPAN), BF16)] * WIN_BLOCKS,
    )
    return pl.pallas_call(
        kern,
        grid_spec=grid_spec,
        out_shape=jax.ShapeDtypeStruct((b, l, BRANCH_W), BF16),
        compiler_params=_cparams(("parallel", "arbitrary")),
        name="window_mixer",
    )(jnp.asarray(pattern), sink, p_rope, p_rope, p_plain, jnp.asarray(bias))


GLA_GROUP = 4


def _gla_kernel(q_ref, k_ref, v_ref, g_ref, wg_ref, bg_ref, gn_ref, o_ref, accf_s, accb_s, st_s, *, total, n_ctx):
    ck = GLA_CHUNK
    n_groups = total // (ck * GLA_GROUP)
    ctx_groups = n_ctx // (ck * GLA_GROUP)
    hw = GLA_HEADS * GLA_DK
    acc_refs = (accf_s, accb_s)

    st_s[...] = jnp.zeros_like(st_s)

    rr = lax.broadcasted_iota(jnp.int32, (ck, ck), 0)
    cc = lax.broadcasted_iota(jnp.int32, (ck, ck), 1)
    tri = (rr >= cc, rr <= cc)
    tri_b = tuple(jnp.where(t, 1.0, 0.0).astype(BF16) for t in tri)
    tri2 = tuple(jnp.concatenate([t, t], axis=0) for t in tri)
    lane = lax.broadcasted_iota(jnp.int32, (LANES, LANES), 1)

    def body(i, _):
        gf = i
        gb = jnp.where(i < ctx_groups, ctx_groups - 1 - i, n_groups - 1 - (i - ctx_groups))
        chains = []
        for u in range(GLA_GROUP):
            for d, grp in ((0, gf), (1, gb)):
                c = grp * GLA_GROUP + (u if d == 0 else GLA_GROUP - 1 - u)
                chains.append((d, pl.multiple_of(c * ck, ck)))

        gate = [_dot(g_ref[0, pl.ds(r0, ck), :], wg_ref[:, d * hw:(d + 1) * hw]) + bg_ref[:, d * hw:(d + 1) * hw]
                for d, r0 in chains]
        cums = []
        for (d, r0), y in zip(chains, gate):
            la = (jnp.minimum(y, 0.0) - jnp.log(1.0 + jnp.exp(-jnp.abs(y)))) * (1.0 / GLA_TAU)
            hi = la.astype(BF16)
            lo = (la - hi.astype(F32)).astype(BF16)
            cums.append(_dot(tri_b[d], hi) + _dot(tri_b[d], lo))

        prepped = []
        for (d, r0), cum in zip(chains, cums):
            tot = cum[ck - 1:ck, :] if d == 0 else cum[0:1, :]
            qf = q_ref[0, pl.ds(r0, ck), :].astype(F32)
            kf = k_ref[0, pl.ds(r0, ck), :].astype(F32)
            qt = (qf * jnp.exp(cum)).astype(BF16)
            kt = (kf * jnp.exp(-cum)).astype(BF16)
            kw = (kf * jnp.exp(tot - cum)).astype(BF16)
            qs, a2 = [], []
            for p in range(2):
                sl = slice(p * LANES, (p + 1) * LANES)
                qs.append(_stack_halves(qt[:, sl]))
                a2.append(jnp.where(tri2[d], _dot_nt(qs[p], kt[:, sl]), 0.0).astype(BF16))
            prepped.append((qs, a2, kw, jnp.exp(tot)))

        intra, update = [], []
        for (d, r0), (qs, a2, kw, dec) in zip(chains, prepped):
            av, upd = [], []
            for p in range(2):
                sl = slice(p * LANES, (p + 1) * LANES)
                halves = []
                for hh in range(2):
                    vh = v_ref[0, pl.ds(r0, ck), (2 * p + hh) * GLA_DV:(2 * p + hh + 1) * GLA_DV]
                    av.append(_dot(a2[p][hh * ck:(hh + 1) * ck], vh))
                    halves.append(_dot_tn(vh, kw[:, sl]))
                upd.append(jnp.where(lane < GLA_DK, halves[0], halves[1]))
            intra.append(av)
            update.append(upd)

        states = [[st_s[2 * d + p] for p in range(2)] for d in range(2)]
        for (d, r0), (qs, a2, kw, dec), av, upd in zip(chains, prepped, intra, update):
            for p in range(2):
                inter = _dot_nt(qs[p], states[d][p].astype(BF16))
                for hh in range(2):
                    vs = slice((2 * p + hh) * GLA_DV, (2 * p + hh + 1) * GLA_DV)
                    acc_refs[d][pl.ds(r0, ck), vs] = av[2 * p + hh] + inter[hh * ck:(hh + 1) * ck]
                states[d][p] = states[d][p] * dec[:, p * LANES:(p + 1) * LANES] + upd[p]
        for d in range(2):
            for p in range(2):
                st_s[2 * d + p] = states[d][p]
        return 0

    lax.fori_loop(0, n_groups, body, 0)

    def finish(i, _):
        r0 = pl.multiple_of(i * TQ, TQ)
        for h in range(GLA_HEADS):
            vs = slice(h * GLA_DV, (h + 1) * GLA_DV)
            o = accf_s[pl.ds(r0, TQ), vs] + accb_s[pl.ds(r0, TQ), vs]
            y = o * lax.rsqrt(jnp.mean(o * o, axis=-1, keepdims=True) + EPS) * gn_ref[:, vs]
            o_ref[0, pl.ds(r0, TQ), vs] = y.astype(BF16)
        return 0

    lax.fori_loop(0, total // TQ, finish, 0)


def _gla_mixer(p_plain, wg, bg, gn, n_ctx):
    b, l, _ = p_plain.shape
    kern = functools.partial(_gla_kernel, total=l, n_ctx=n_ctx)
    const = lambda i: (0, 0)
    return pl.pallas_call(
        kern,
        grid=(b,),
        in_specs=[pl.BlockSpec((1, l, 256), lambda i: (i, 0, PL_AQ)),
                  pl.BlockSpec((1, l, 256), lambda i: (i, 0, PL_AK)),
                  pl.BlockSpec((1, l, 512), lambda i: (i, 0, PL_AV)),
                  pl.BlockSpec((1, l, LANES), lambda i: (i, 0, PL_AG)),
                  pl.BlockSpec((LANES, 512), const),
                  pl.BlockSpec((1, 512), const),
                  pl.BlockSpec((1, 512), const)],
        out_specs=pl.BlockSpec((1, l, BRANCH_W), lambda i: (i, 0, 0)),
        out_shape=jax.ShapeDtypeStruct((b, l, BRANCH_W), BF16),
        scratch_shapes=[pltpu.VMEM((l, BRANCH_W), F32), pltpu.VMEM((l, BRANCH_W), F32),
                        pltpu.VMEM((4, GLA_DV, LANES), F32)],
        compiler_params=_cparams(("parallel",)),
        name="gla_mixer",
    )(p_plain, p_plain, p_plain, p_plain, wg, bg, gn)


def _merge_kernel(h_ref, oa_ref, ob_ref, oc_ref, od_ref, za_ref, zb_ref, zc_ref, zd_ref, *rest, alpha, off):
    *x_refs, mod_ref, wm_ref, wup_ref, wout_ref, lng_ref, lnb_ref, out_ref = rest
    h = h_ref[0]
    acc = None
    for i, (o_ref, z_ref) in enumerate(((oa_ref, za_ref), (ob_ref, zb_ref), (oc_ref, zc_ref), (od_ref, zd_ref))):
        z = z_ref[0].astype(F32)
        br = (o_ref[0].astype(F32) * (z * _sigmoid(z))).astype(BF16)
        term = _sigmoid(_dot(h, wm_ref[i])) * _dot(br, wup_ref[i])
        acc = term if acc is None else acc + term
    y = _dot(acc.astype(BF16), wout_ref[...])
    r = alpha * _stream_tile(x_refs, off) + mod_ref[0, 0, 2:3, :] * y
    mu = jnp.mean(r, axis=-1, keepdims=True)
    rc = r - mu
    var = jnp.mean(rc * rc, axis=-1, keepdims=True)
    out_ref[0] = rc * lax.rsqrt(var + EPS) * lng_ref[...] + lnb_ref[...]


def _merge(h, outs, p_plain, xs, mod, wm, wup, wout, ln_g, ln_b, layer, skip_ctx):
    b, l, d = h.shape
    off = 1 if skip_ctx else 0
    nt = l // TQ - off
    row = lambda i, j: (i, j + off, 0)
    zspec = lambda blk: pl.BlockSpec((1, TQ, BRANCH_W), lambda i, j: (i, j + off, blk))
    c2 = lambda i, j: (0, 0)
    in_specs = ([pl.BlockSpec((1, TQ, d), row)]
                + [pl.BlockSpec((1, TQ, BRANCH_W), row)] * 4
                + [zspec(PL_AZ), zspec(PL_BZ), zspec(PL_CZ), zspec(PL_DZ)]
                + _stream_specs(xs, d, off)
                + [pl.BlockSpec((1, 1, 3, d), lambda i, j: (i, jnp.minimum(j + off, 1), 0, 0)),
                   pl.BlockSpec((None, 4, d, d), lambda i, j: (layer, 0, 0, 0)),
                   pl.BlockSpec((None, 4, BRANCH_W, d), lambda i, j: (layer, 0, 0, 0)),
                   pl.BlockSpec((None, d, d), lambda i, j: (layer, 0, 0)),
                   pl.BlockSpec((1, d), c2), pl.BlockSpec((1, d), c2)])
    return pl.pallas_call(
        functools.partial(_merge_kernel, alpha=(2 * DEPTH) ** 0.25, off=off),
        grid=(b, nt),
        in_specs=in_specs,
        out_specs=pl.BlockSpec((1, TQ, d), lambda i, j: (i, j, 0)),
        out_shape=jax.ShapeDtypeStruct((b, nt * TQ, d), F32),
        compiler_params=_cparams(("parallel", "parallel")),
        name="merge",
    )(h, *outs, p_plain, p_plain, p_plain, p_plain, *xs, mod, wm, wup, wout,
      ln_g.reshape(1, d), ln_b.reshape(1, d))


def _perm_window(a, axis):
    a = jnp.moveaxis(a, axis, -1)
    lead = a.shape[:-1]
    a = a.reshape(lead + (WIN_KV, WIN_HEADS // WIN_KV, HEAD_DIM)).swapaxes(-3, -2).reshape(lead + (BRANCH_W,))
    return jnp.moveaxis(a, -1, axis)


def _perm_global(a, axis):
    a = jnp.moveaxis(a, axis, -1)
    lead = a.shape[:-1]
    a = a.reshape(lead + (GLB_KV // 2, 2, GLB_HEADS // GLB_KV, HEAD_DIM)).swapaxes(-3, -2).reshape(lead + (BRANCH_W,))
    return jnp.moveaxis(a, -1, axis)


def _rotary_lanes(a):
    lead = a.shape[:-1]
    a = a.reshape(lead + (a.shape[-1] // LANES, 2, 2, 2, ROPE_HALF))
    return jnp.moveaxis(a, -2, -4).reshape(lead + (-1,))


def _split_w_in(w):
    w = w.astype(BF16)
    cols = lambda off, n: w[..., off:off + n]
    pad = jnp.zeros(w.shape[:-1] + (LANES - 2 * GLA_RANK,), w.dtype)
    w_plain = jnp.concatenate([
        cols(_A_V, 512), cols(_A_Z, 512), _perm_window(cols(_B_Z, 512), -1), _perm_global(cols(_C_Z, 512), -1),
        cols(_D_Z, 512), cols(_A_Q, 256) * GLA_DK ** -0.5, cols(_A_K, 256), cols(_C_V, 256), cols(_D_V, 256),
        cols(_B_V, 128), cols(_A_GF, 2 * GLA_RANK), pad], axis=-1)
    w_rope = _rotary_lanes(jnp.concatenate([
        _perm_window(cols(_B_Q, 512), -1) * HEAD_DIM ** -0.5, cols(_D_Q, 512) * HEAD_DIM ** -0.5,
        cols(_D_K, 256), cols(_B_K, 128)], axis=-1))
    w_nr = _rotary_lanes(jnp.concatenate([_perm_global(cols(_C_Q, 512), -1), cols(_C_K, 256)], axis=-1))
    return w_plain, w_rope, w_nr


def _rope_tables(n_ctx, n_lat):
    t = np.arange(n_lat)
    freqs = ROPE_BASE ** (-np.arange(ROPE_HALF, dtype=np.float32) / ROPE_HALF)
    pos = np.stack([(t // GRID_W).astype(np.float32), (t % GRID_W).astype(np.float32)], axis=1)
    ang = jnp.asarray(pos[:, :, None] * freqs[None, None, :], F32).reshape(n_lat, 2 * ROPE_HALF)
    cos = jnp.tile(jnp.cos(ang), (1, LANES // (2 * ROPE_HALF)))
    sign = np.where(np.arange(LANES) < LANES // 2, -1.0, 1.0).astype(np.float32)
    sin = jnp.tile(jnp.sin(ang), (1, LANES // (2 * ROPE_HALF))) * sign
    cos_t = jnp.concatenate([jnp.ones((n_ctx, LANES), F32), cos], axis=0)
    sin_t = jnp.concatenate([jnp.zeros((n_ctx, LANES), F32), sin], axis=0)
    return cos_t, sin_t


def kernel(x, c, ctx, c_ctx, w_ada, b_ada, w_in, gla_w_gate, gla_b_gate, gla_norm, win_sink, glb_q_norm,
           glb_k_norm, diff_lambda, diff_norm, w_merge, w_up, w_out, ln_g, ln_b):
    b, n_lat, d = x.shape
    n_ctx = ctx.shape[1]
    assert n_ctx == TQ and n_lat % TK == 0 and d == 1024
    xs = (ctx, x)
    cs = jnp.zeros((16, d), F32).at[0:b].set(c).at[b].set(c_ctx)
    cos_t, sin_t = _rope_tables(n_ctx, n_lat)
    head_of_lane = (np.arange(LANES) // (HEAD_DIM // 2)) % 2
    seg = jnp.asarray(head_of_lane[:, None] == head_of_lane[None, :], BF16)

    w_plain, w_rope, w_nr = _split_w_in(w_in)
    gain_nr = _rotary_lanes(jnp.concatenate([jnp.tile(glb_q_norm * (HEAD_DIM ** -0.5 * LOG2E), (1, GLB_HEADS)),
                                             jnp.tile(glb_k_norm, (1, GLB_KV))], axis=1))
    wg = jnp.zeros((DEPTH, LANES, 2 * GLA_HEADS * GLA_DK), F32)
    wg = wg.at[:, 0:GLA_RANK, 0:256].set(gla_w_gate[:, 0]).at[:, GLA_RANK:2 * GLA_RANK, 256:512].set(gla_w_gate[:, 1])
    wg = wg.astype(BF16)
    wup = w_up.astype(BF16)
    wup = jnp.stack([wup[:, 0], _perm_window(wup[:, 1], -2), _perm_global(wup[:, 2], -2), wup[:, 3]], axis=1)
    wm, wout = w_merge.astype(BF16), w_out.astype(BF16)

    ada = _ada(cs, w_ada, b_ada)
    mod_x = ada[:, 0:b].reshape(DEPTH, b, 1, 3, d)
    mod_c = jnp.broadcast_to(ada[:, b].reshape(DEPTH, 1, 1, 3, d), (DEPTH, b, 1, 3, d))
    mods = jnp.concatenate([mod_c, mod_x], axis=2)

    for layer in range(DEPTH):
        last = layer == DEPTH - 1
        lam_init = 0.8 - 0.6 * math.exp(-0.3 * layer)
        mod = mods[layer]
        h = _ln_mod(xs, mod)
        p_plain = _inproj(h, w_plain, layer, "plain")
        p_rope = _inproj(h, w_rope, layer, "rope", cos_t, sin_t)
        p_nr = _inproj(h, w_nr, layer, "nr", cos_t, sin_t, gain_nr[layer].reshape(1, NR_W), seg)

        o_a = _gla_mixer(p_plain, wg[layer], gla_b_gate[layer].reshape(1, 512), gla_norm[layer].reshape(1, BRANCH_W),
                         n_ctx)
        o_b = _window_mixer(p_rope, p_plain, win_sink[layer], n_ctx)
        o_c, o_d = _dense_mixers(p_nr, p_rope, p_plain, diff_lambda[layer], diff_norm[layer], lam_init, n_ctx)

        xs = (_merge(h, (o_a, o_b, o_c, o_d), p_plain, xs, mod, wm, wup, wout, ln_g[layer], ln_b[layer], layer,
                     skip_ctx=last),)
    return xs[0]
```

```python
import functools
import math

import numpy as np
import jax
import jax.numpy as jnp
from jax import lax
from jax.experimental import pallas as pl
from jax.experimental.pallas import tpu as pltpu

F32 = jnp.float32
BF16 = jnp.bfloat16

DEPTH = 2
GRID_W = 64
HEAD_DIM = 64
ROPE_HALF = HEAD_DIM // 4
ROPE_BASE = 10000.0
EPS = 1e-6
NEG_INF = -1e30
GLA_HEADS, GLA_DK, GLA_DV, GLA_RANK, GLA_TAU, GLA_CHUNK = 4, 64, 128, 16, 16.0, 64
WIN_HEADS, WIN_KV, WINDOW = 8, 2, 128
GLB_HEADS, GLB_KV = 8, 4
DIF_HEADS, DIF_KV, DIF_DV = 4, 2, 128
BRANCH_W = 512

LANES = 128
TQ = 256
TK = 512
VMEM_LIMIT = 56 << 20

_A_Q, _A_K, _A_V, _A_GF, _A_Z = 0, 256, 512, 1024, 1056
_B_Q, _B_K, _B_V, _B_Z = 1568, 2080, 2208, 2336
_C_Q, _C_K, _C_V, _C_Z = 2848, 3360, 3616, 3872
_D_Q, _D_K, _D_V, _D_Z = 4384, 4896, 5152, 5408

PL_AV, PL_AZ, PL_BZ, PL_CZ, PL_DZ = 0, 1, 2, 3, 4
PL_AQ, PL_AK, PL_CV, PL_DV = 10, 11, 12, 13
PL_BV, PL_AG = 28, 29
PLAIN_W, ROPE_W, NR_W = 3840, 1408, 768
ROPE_QW = 1024
LOG2E = math.log2(math.e)


def _cparams(sem):
    return pltpu.CompilerParams(dimension_semantics=sem, vmem_limit_bytes=VMEM_LIMIT)


def _sigmoid(x):
    return 1.0 / (1.0 + jnp.exp(-x))


def _dot(a, b):
    return jnp.dot(a, b, preferred_element_type=F32)


def _dot_nt(a, b):
    return lax.dot_general(a, b, (((1,), (1,)), ((), ())), preferred_element_type=F32)


def _dot_tn(a, b):
    return lax.dot_general(a, b, (((0,), (0,)), ((), ())), preferred_element_type=F32)


def _ada_kernel(c_ref, w_ref, b_ref, o_ref):
    c = c_ref[...]
    s = c * _sigmoid(c)
    o_ref[...] = jnp.dot(s, w_ref[...], preferred_element_type=F32,
                         precision=lax.Precision.HIGHEST) + b_ref[...]


def _ada(cs, w_ada, b_ada):
    r, d = cs.shape
    depth, _, n = w_ada.shape
    bn = 1024
    return pl.pallas_call(
        _ada_kernel,
        grid=(depth, n // bn),
        in_specs=[pl.BlockSpec((r, d), lambda l, j: (0, 0)),
                  pl.BlockSpec((None, d, bn), lambda l, j: (l, 0, j)),
                  pl.BlockSpec((None, 1, bn), lambda l, j: (l, 0, j))],
        out_specs=pl.BlockSpec((None, r, bn), lambda l, j: (l, 0, j)),
        out_shape=jax.ShapeDtypeStruct((depth, r, n), F32),
        compiler_params=_cparams(("arbitrary", "arbitrary")),
        name="ada",
    )(cs, w_ada, b_ada.reshape(depth, 1, n))


def _stream_specs(xs, d, off=0):
    if len(xs) == 1:
        return [pl.BlockSpec((1, TQ, d), lambda i, j: (i, j + off, 0))]
    return [pl.BlockSpec((1, TQ, d), lambda i, j: (i, 0, 0)),
            pl.BlockSpec((1, TQ, d), lambda i, j: (i, jnp.maximum(j + off - 1, 0), 0))]


def _stream_tile(refs, off=0):
    if len(refs) == 1:
        return refs[0][0]
    return jnp.where(pl.program_id(1) + off == 0, refs[0][0], refs[1][0])


def _ln_kernel(*refs):
    *x_refs, mod_ref, h_ref = refs
    x = _stream_tile(x_refs)
    mu = jnp.mean(x, axis=-1, keepdims=True)
    xc = x - mu
    var = jnp.mean(xc * xc, axis=-1, keepdims=True)
    y = xc * lax.rsqrt(var + EPS)
    shift = mod_ref[0, 0, 0:1, :]
    scale = mod_ref[0, 0, 1:2, :]
    h_ref[0] = (y * (1.0 + scale) + shift).astype(BF16)


def _ln_mod(xs, mod):
    b, _, d = xs[0].shape
    l = sum(a.shape[1] for a in xs)
    return pl.pallas_call(
        _ln_kernel,
        grid=(b, l // TQ),
        in_specs=_stream_specs(xs, d) + [pl.BlockSpec((1, 1, 3, d), lambda i, j: (i, jnp.minimum(j, 1), 0, 0))],
        out_specs=pl.BlockSpec((1, TQ, d), lambda i, j: (i, j, 0)),
        out_shape=jax.ShapeDtypeStruct((b, l, d), BF16),
        compiler_params=_cparams(("parallel", "parallel")),
        name="ln_mod",
    )(*xs, mod)


def _rope(y, cos, sin):
    return y * cos + pltpu.roll(y, LANES // 2, 1) * sin


def _inproj_plain_kernel(h_ref, w_ref, o_ref):
    o_ref[0] = _dot(h_ref[0], w_ref[...]).astype(BF16)


def _inproj_rope_kernel(h_ref, w_ref, cos_ref, sin_ref, o_ref, *, width, q_width):
    acc = _dot(h_ref[0], w_ref[...])
    cos, sin = cos_ref[...], sin_ref[...]
    for c in range(width // LANES):
        y = _rope(acc[:, c * LANES:(c + 1) * LANES], cos, sin)
        if c * LANES < q_width:
            y = y * LOG2E
        o_ref[0, :, c * LANES:(c + 1) * LANES] = y.astype(BF16)


def _inproj_nr_kernel(h_ref, w_ref, cos_ref, sin_ref, gain_ref, seg_ref, o_ref, *, width):
    acc = _dot(h_ref[0], w_ref[...])
    cos, sin = cos_ref[...], sin_ref[...]
    for c in range(width // LANES):
        y = acc[:, c * LANES:(c + 1) * LANES]
        ss = _dot((y * y).astype(BF16), seg_ref[...])
        y = y * lax.rsqrt(ss * (1.0 / HEAD_DIM) + EPS) * gain_ref[:, c * LANES:(c + 1) * LANES]
        o_ref[0, :, c * LANES:(c + 1) * LANES] = _rope(y, cos, sin).astype(BF16)


def _inproj(h, w, layer, mode, cos=None, sin=None, gain=None, seg=None):
    b, l, d = h.shape
    n = w.shape[2]
    bm = l // 2
    if mode == "plain":
        bn = 768
        return pl.pallas_call(
            _inproj_plain_kernel,
            grid=(b, l // bm, n // bn),
            in_specs=[pl.BlockSpec((1, bm, d), lambda i, r, j: (i, r, 0)),
                      pl.BlockSpec((None, d, bn), lambda i, r, j: (layer, 0, j))],
            out_specs=pl.BlockSpec((1, bm, bn), lambda i, r, j: (i, r, j)),
            out_shape=jax.ShapeDtypeStruct((b, l, n), BF16),
            compiler_params=_cparams(("parallel", "parallel", "arbitrary")),
            name="inproj_plain",
        )(h, w)
    tab = pl.BlockSpec((bm, LANES), lambda i, r: (r, 0))
    in_specs = [pl.BlockSpec((1, bm, d), lambda i, r: (i, r, 0)),
                pl.BlockSpec((None, d, n), lambda i, r: (layer, 0, 0)), tab, tab]
    args = [h, w, cos, sin]
    if mode == "rope":
        body = functools.partial(_inproj_rope_kernel, width=n, q_width=ROPE_QW)
    else:
        body = functools.partial(_inproj_nr_kernel, width=n)
        in_specs += [pl.BlockSpec((1, n), lambda i, r: (0, 0)),
                     pl.BlockSpec((LANES, LANES), lambda i, r: (0, 0))]
        args += [gain, seg]
    return pl.pallas_call(
        body,
        grid=(b, l // bm),
        in_specs=in_specs,
        out_specs=pl.BlockSpec((1, bm, n), lambda i, r: (i, r, 0)),
        out_shape=jax.ShapeDtypeStruct((b, l, n), BF16),
        compiler_params=_cparams(("parallel", "parallel")),
        name="inproj_" + mode,
    )(*args)


def _stack_halves(q, rotary=False):
    lane = lax.broadcasted_iota(jnp.int32, q.shape, 1)
    first = ((lane // (HEAD_DIM // 2)) % 2 == 0) if rotary else (lane < HEAD_DIM)
    zero = jnp.zeros_like(q)
    return jnp.concatenate([jnp.where(first, q, zero), jnp.where(first, zero, q)], axis=0)


FLASH_ROWS = 2 * TQ


FLASH_REFS = 6


def _flash_scratch(n_streams):
    stat = pltpu.VMEM((FLASH_ROWS, LANES), F32)
    one = [pltpu.VMEM((FLASH_ROWS, LANES), BF16), stat, stat, stat,
           pltpu.VMEM((FLASH_ROWS, TK), F32), pltpu.VMEM((FLASH_ROWS, TK), BF16)]
    return one * n_streams


def _fold_chunk(stream, s, v, first):
    _, m_s, l_s, acc_s, _, p_s = stream
    tk = s.shape[1]
    m_new = jnp.broadcast_to(jnp.max(s, axis=1, keepdims=True), (FLASH_ROWS, LANES))
    if not first:
        m_old = m_s[...]
        m_new = jnp.maximum(m_old, m_new)
        alpha = jnp.exp2(m_old - m_new)
    p = jnp.exp2(s - jnp.concatenate([m_new] * (tk // LANES), axis=1))
    p_s[:, 0:tk] = p.astype(BF16)
    psum = p[:, 0:LANES]
    for t in range(1, tk // LANES):
        psum = psum + p[:, t * LANES:(t + 1) * LANES]
    pv = _dot(p_s[:, 0:tk], v)
    m_s[...] = m_new
    l_s[...] = psum if first else alpha * l_s[...] + psum
    acc_s[...] = pv if first else alpha * acc_s[...] + pv


def _flash(q_blocks, kv_srcs, n_ctx, n_lat_chunks, scratch):
    ns = len(q_blocks)
    streams = [scratch[FLASH_REFS * i:FLASH_REFS * (i + 1)] for i in range(ns)]
    kv_lanes = [slice(kb * LANES, (kb + 1) * LANES) for _, _, kb in kv_srcs]

    def scores(i, r0, tk):
        qs_s, _, _, _, s_s, _ = streams[i]
        s_s[:, 0:tk] = _dot_nt(qs_s[...], kv_srcs[i][0][0, pl.ds(r0, tk), kv_lanes[i]])

    def accumulate(i, r0, tk, first=False):
        s_s = streams[i][4]
        _fold_chunk(streams[i], s_s[:, 0:tk], kv_srcs[i][1][0, pl.ds(r0, tk), kv_lanes[i]], first)

    def stage(r0, tk, nxt, first=False):
        for i in range(ns):
            accumulate(i, r0, tk, first)
            if nxt is not None:
                scores(i, *nxt)

    def head():
        for i in range(ns):
            streams[i][0][...] = _stack_halves(q_blocks[i](), rotary=True)
            scores(i, 0, n_ctx)
        stage(0, n_ctx, (n_ctx, TK), first=True)

    def body(c, _):
        r0 = pl.multiple_of(n_ctx + c * TK, LANES)
        stage(r0, TK, (r0 + TK, TK))
        return 0

    def latent():
        lax.fori_loop(0, n_lat_chunks - 1, body, 0)
        stage(n_ctx + (n_lat_chunks - 1) * TK, TK, None)

    def results():
        return [acc_s[...] / jnp.sum(l_s[...], axis=1, keepdims=True) for _, _, l_s, acc_s, _, _ in streams]

    return head, latent, results


N_GLB_BLOCKS = GLB_HEADS // 2
N_DIF_BLOCKS = DIF_HEADS


def _dense_kernel(qc_ref, kc_ref, vc_ref, qd_ref, kd_ref, vd_ref, lam_ref, g_ref, oc_ref, od_ref, *scratch,
                  n_ctx, n_lat, n_tiles, lam_init):
    load = lambda ref, c: (lambda: ref[0, :, c * LANES:(c + 1) * LANES])
    q_blocks = [load(qc_ref, c) for c in range(N_GLB_BLOCKS)] + [load(qd_ref, c) for c in range(N_DIF_BLOCKS)]
    kv_srcs = ([(kc_ref, vc_ref, c // 2) for c in range(N_GLB_BLOCKS)]
               + [(kd_ref, vd_ref, c // 2) for c in range(N_DIF_BLOCKS)])
    head, latent, results = _flash(q_blocks, kv_srcs, n_ctx, n_lat // TK, scratch)

    def write_out():
        outs = results()
        lane = lax.broadcasted_iota(jnp.int32, (TQ, LANES), 1)
        for c, o in enumerate(outs[:N_GLB_BLOCKS]):
            oc_ref[0, :, c * LANES:(c + 1) * LANES] = jnp.where(lane < HEAD_DIM, o[0:TQ], o[TQ:2 * TQ]).astype(BF16)
        lp = lam_ref[...]
        lam = (jnp.exp(jnp.sum(lp[0:1] * lp[1:2], axis=1, keepdims=True))
               - jnp.exp(jnp.sum(lp[2:3] * lp[3:4], axis=1, keepdims=True)) + lam_init)
        for c, o in enumerate(outs[N_GLB_BLOCKS:]):
            o = o[0:TQ] - lam * o[TQ:2 * TQ]
            y = o * lax.rsqrt(jnp.mean(o * o, axis=-1, keepdims=True) + EPS) * g_ref[...]
            od_ref[0, :, c * LANES:(c + 1) * LANES] = (y * (1.0 - lam_init)).astype(BF16)

    j = pl.program_id(1)

    @pl.when(j == 0)
    def _():
        head()

    @pl.when((j > 0) & (j < n_tiles))
    def _():
        write_out()
        head()
        latent()

    @pl.when(j == n_tiles)
    def _():
        write_out()


def _dense_mixers(p_nr, p_rope, p_plain, lam_p, sub_g, lam_init, n_ctx):
    b, l, _ = p_nr.shape
    nt = l // TQ
    kern = functools.partial(_dense_kernel, n_ctx=n_ctx, n_lat=l - n_ctx, n_tiles=nt, lam_init=lam_init)
    tile = lambda blk: pl.BlockSpec((1, TQ, 512), lambda i, j: (i, jnp.minimum(j, nt - 1), blk))
    lagged = pl.BlockSpec((1, TQ, 512), lambda i, j: (i, jnp.maximum(j - 1, 0), 0))
    full = lambda blk: pl.BlockSpec((1, l, 256), lambda i, j: (i, 0, blk))
    out = jax.ShapeDtypeStruct((b, l, BRANCH_W), BF16)
    return pl.pallas_call(
        kern,
        grid=(b, nt + 1),
        in_specs=[tile(0), full(2), full(PL_CV),
                  tile(1), full(4), full(PL_DV),
                  pl.BlockSpec((4, HEAD_DIM), lambda i, j: (0, 0)),
                  pl.BlockSpec((1, DIF_DV), lambda i, j: (0, 0))],
        out_specs=[lagged, lagged],
        out_shape=[out, out],
        scratch_shapes=_flash_scratch(N_GLB_BLOCKS + N_DIF_BLOCKS),
        compiler_params=_cparams(("parallel", "arbitrary")),
        name="dense_mixers",
    )(p_nr, p_nr, p_plain, p_rope, p_rope, p_plain, lam_p, sub_g.reshape(1, DIF_DV))


WIN_SPAN = TQ + 2 * WINDOW
WIN_BLOCKS = WIN_HEADS // WIN_KV


def _window_start(j, total):
    return jnp.clip(j * TQ - WINDOW, 0, total - WIN_SPAN)


def _window_bias(n_ctx, total):
    row = np.arange(2 * TQ)[:, None] % TQ
    col = np.arange(WIN_SPAN)[None, :]
    patterns, index = [], []
    for j in range(total // TQ):
        start = int(np.clip(j * TQ - WINDOW, 0, total - WIN_SPAN))
        qpos, kpos = j * TQ + row, start + col
        valid = (kpos >= n_ctx) & (qpos >= n_ctx) & (np.abs(kpos - qpos) <= WINDOW)
        bias = np.where(valid, 0.0, NEG_INF).astype(np.float32)
        for k, seen in enumerate(patterns):
            if np.array_equal(seen, bias):
                index.append(k)
                break
        else:
            index.append(len(patterns))
            patterns.append(bias)
    return np.stack(patterns), np.asarray(index, np.int32)


def _window_kernel(pattern_ref, sink_ref, q_ref, k_ref, v_ref, bias_ref, o_ref, *scratch, n_ctx, total):
    j = pl.program_id(1)
    start = pl.multiple_of(_window_start(j, total), LANES)
    streams = [scratch[3 * g:3 * (g + 1)] for g in range(WIN_BLOCKS)]
    for g, (qs_s, s_s, _) in enumerate(streams):
        qs_s[...] = _stack_halves(q_ref[0, :, g * LANES:(g + 1) * LANES], rotary=True)
        s_s[:, 0:n_ctx] = _dot_nt(qs_s[...], k_ref[0, 0:n_ctx, :])
        s_s[:, n_ctx:] = _dot_nt(qs_s[...], k_ref[0, pl.ds(start, WIN_SPAN), :])

    srow = lax.broadcasted_iota(jnp.int32, (2 * TQ, LANES), 0)
    lane = lax.broadcasted_iota(jnp.int32, (TQ, LANES), 1)
    width = n_ctx + WIN_SPAN
    for g, (_, s_s, p_s) in enumerate(streams):
        sink = jnp.where(srow < TQ, sink_ref[g], sink_ref[WIN_BLOCKS + g]) * LOG2E
        s_c = s_s[:, 0:n_ctx]
        s_w = s_s[:, n_ctx:] + bias_ref[0]
        m = jnp.maximum(jnp.maximum(jnp.max(s_c, axis=1, keepdims=True), jnp.max(s_w, axis=1, keepdims=True)), sink)
        p_c = jnp.exp2(s_c - jnp.concatenate([m] * (n_ctx // LANES), axis=1))
        p_w = jnp.exp2(s_w - jnp.concatenate([m] * (WIN_SPAN // LANES), axis=1))
        p_s[:, 0:n_ctx] = p_c.astype(BF16)
        p_s[:, n_ctx:] = p_w.astype(BF16)
        psum = p_c[:, 0:LANES]
        for t in range(1, n_ctx // LANES):
            psum = psum + p_c[:, t * LANES:(t + 1) * LANES]
        for t in range(WIN_SPAN // LANES):
            psum = psum + p_w[:, t * LANES:(t + 1) * LANES]
        den = jnp.sum(psum, axis=1, keepdims=True) + jnp.exp2(sink - m)
        o = (_dot(p_s[:, 0:n_ctx], v_ref[0, 0:n_ctx, :])
             + _dot(p_s[:, n_ctx:width], v_ref[0, pl.ds(start, WIN_SPAN), :])) / den
        o_ref[0, :, g * LANES:(g + 1) * LANES] = jnp.where(lane < HEAD_DIM, o[0:TQ], o[TQ:2 * TQ]).astype(BF16)


def _window_mixer(p_rope, p_plain, sink, n_ctx):
    b, l, _ = p_rope.shape
    kern = functools.partial(_window_kernel, n_ctx=n_ctx, total=l)
    bias, pattern = _window_bias(n_ctx, l)
    grid_spec = pltpu.PrefetchScalarGridSpec(
        num_scalar_prefetch=2,
        grid=(b, l // TQ),
        in_specs=[pl.BlockSpec((1, TQ, 512), lambda i, j, pat, s: (i, j, 0)),
                  pl.BlockSpec((1, l, LANES), lambda i, j, pat, s: (i, 0, 10)),
                  pl.BlockSpec((1, l, LANES), lambda i, j, pat, s: (i, 0, PL_BV)),
                  pl.BlockSpec((1, 2 * TQ, WIN_SPAN), lambda i, j, pat, s: (pat[j], 0, 0))],
        out_specs=pl.BlockSpec((1, TQ, 512), lambda i, j, pat, s: (i, j, 0)),
        scratch_shapes=[pltpu.VMEM((2 * TQ, LANES), BF16), pltpu.VMEM((2 * TQ, n_ctx + WIN_SPAN), F32),
                        pltpu.VMEM((2 * TQ, n_ctx + WIN_SPAN), BF16)] * WIN_BLOCKS,
    )
    return pl.pallas_call(
        kern,
        grid_spec=grid_spec,
        out_shape=jax.ShapeDtypeStruct((b, l, BRANCH_W), BF16),
        compiler_params=_cparams(("parallel", "arbitrary")),
        name="window_mixer",
    )(jnp.asarray(pattern), sink, p_rope, p_rope, p_plain, jnp.asarray(bias))


GLA_GROUP = 4


def _gla_kernel(q_ref, k_ref, v_ref, g_ref, wg_ref, bg_ref, gn_ref, o_ref, accf_s, accb_s, st_s, *, total, n_ctx):
    ck = GLA_CHUNK
    n_groups = total // (ck * GLA_GROUP)
    ctx_groups = n_ctx // (ck * GLA_GROUP)
    hw = GLA_HEADS * GLA_DK
    acc_refs = (accf_s, accb_s)

    st_s[...] = jnp.zeros_like(st_s)

    rr = lax.broadcasted_iota(jnp.int32, (ck, ck), 0)
    cc = lax.broadcasted_iota(jnp.int32, (ck, ck), 1)
    tri = (rr >= cc, rr <= cc)
    tri_b = tuple(jnp.where(t, 1.0, 0.0).astype(BF16) for t in tri)
    tri2 = tuple(jnp.concatenate([t, t], axis=0) for t in tri)
    lane = lax.broadcasted_iota(jnp.int32, (LANES, LANES), 1)

    def body(i, _):
        gf = i
        gb = jnp.where(i < ctx_groups, ctx_groups - 1 - i, n_groups - 1 - (i - ctx_groups))
        chains = []
        for u in range(GLA_GROUP):
            for d, grp in ((0, gf), (1, gb)):
                c = grp * GLA_GROUP + (u if d == 0 else GLA_GROUP - 1 - u)
                chains.append((d, pl.multiple_of(c * ck, ck)))

        gate = [_dot(g_ref[0, pl.ds(r0, ck), :], wg_ref[:, d * hw:(d + 1) * hw]) + bg_ref[:, d * hw:(d + 1) * hw]
                for d, r0 in chains]
        cums = []
        for (d, r0), y in zip(chains, gate):
            la = (jnp.minimum(y, 0.0) - jnp.log(1.0 + jnp.exp(-jnp.abs(y)))) * (1.0 / GLA_TAU)
            hi = la.astype(BF16)
            lo = (la - hi.astype(F32)).astype(BF16)
            cums.append(_dot(tri_b[d], hi) + _dot(tri_b[d], lo))

        prepped = []
        for (d, r0), cum in zip(chains, cums):
            tot = cum[ck - 1:ck, :] if d == 0 else cum[0:1, :]
            qf = q_ref[0, pl.ds(r0, ck), :].astype(F32)
            kf = k_ref[0, pl.ds(r0, ck), :].astype(F32)
            qt = (qf * jnp.exp(cum)).astype(BF16)
            kt = (kf * jnp.exp(-cum)).astype(BF16)
            kw = (kf * jnp.exp(tot - cum)).astype(BF16)
            qs, a2 = [], []
            for p in range(2):
                sl = slice(p * LANES, (p + 1) * LANES)
                qs.append(_stack_halves(qt[:, sl]))
                a2.append(jnp.where(tri2[d], _dot_nt(qs[p], kt[:, sl]), 0.0).astype(BF16))
            prepped.append((qs, a2, kw, jnp.exp(tot)))

        intra, update = [], []
        for (d, r0), (qs, a2, kw, dec) in zip(chains, prepped):
            av, upd = [], []
            for p in range(2):
                sl = slice(p * LANES, (p + 1) * LANES)
                halves = []
                for hh in range(2):
                    vh = v_ref[0, pl.ds(r0, ck), (2 * p + hh) * GLA_DV:(2 * p + hh + 1) * GLA_DV]
                    av.append(_dot(a2[p][hh * ck:(hh + 1) * ck], vh))
                    halves.append(_dot_tn(vh, kw[:, sl]))
                upd.append(jnp.where(lane < GLA_DK, halves[0], halves[1]))
            intra.append(av)
            update.append(upd)

        states = [[st_s[2 * d + p] for p in range(2)] for d in range(2)]
        for (d, r0), (qs, a2, kw, dec), av, upd in zip(chains, prepped, intra, update):
            for p in range(2):
                inter = _dot_nt(qs[p], states[d][p].astype(BF16))
                for hh in range(2):
                    vs = slice((2 * p + hh) * GLA_DV, (2 * p + hh + 1) * GLA_DV)
                    acc_refs[d][pl.ds(r0, ck), vs] = av[2 * p + hh] + inter[hh * ck:(hh + 1) * ck]
                states[d][p] = states[d][p] * dec[:, p * LANES:(p + 1) * LANES] + upd[p]
        for d in range(2):
            for p in range(2):
                st_s[2 * d + p] = states[d][p]
        return 0

    lax.fori_loop(0, n_groups, body, 0)

    def finish(i, _):
        r0 = pl.multiple_of(i * TQ, TQ)
        for h in range(GLA_HEADS):
            vs = slice(h * GLA_DV, (h + 1) * GLA_DV)
            o = accf_s[pl.ds(r0, TQ), vs] + accb_s[pl.ds(r0, TQ), vs]
            y = o * lax.rsqrt(jnp.mean(o * o, axis=-1, keepdims=True) + EPS) * gn_ref[:, vs]
            o_ref[0, pl.ds(r0, TQ), vs] = y.astype(BF16)
        return 0

    lax.fori_loop(0, total // TQ, finish, 0)


def _gla_mixer(p_plain, wg, bg, gn, n_ctx):
    b, l, _ = p_plain.shape
    kern = functools.partial(_gla_kernel, total=l, n_ctx=n_ctx)
    const = lambda i: (0, 0)
    return pl.pallas_call(
        kern,
        grid=(b,),
        in_specs=[pl.BlockSpec((1, l, 256), lambda i: (i, 0, PL_AQ)),
                  pl.BlockSpec((1, l, 256), lambda i: (i, 0, PL_AK)),
                  pl.BlockSpec((1, l, 512), lambda i: (i, 0, PL_AV)),
                  pl.BlockSpec((1, l, LANES), lambda i: (i, 0, PL_AG)),
                  pl.BlockSpec((LANES, 512), const),
                  pl.BlockSpec((1, 512), const),
                  pl.BlockSpec((1, 512), const)],
        out_specs=pl.BlockSpec((1, l, BRANCH_W), lambda i: (i, 0, 0)),
        out_shape=jax.ShapeDtypeStruct((b, l, BRANCH_W), BF16),
        scratch_shapes=[pltpu.VMEM((l, BRANCH_W), F32), pltpu.VMEM((l, BRANCH_W), F32),
                        pltpu.VMEM((4, GLA_DV, LANES), F32)],
        compiler_params=_cparams(("parallel",)),
        name="gla_mixer",
    )(p_plain, p_plain, p_plain, p_plain, wg, bg, gn)


def _merge_kernel(h_ref, oa_ref, ob_ref, oc_ref, od_ref, za_ref, zb_ref, zc_ref, zd_ref, *rest, alpha, off,
                  emit_next):
    if emit_next:
        *x_refs, mod_ref, wm_ref, wup_ref, wout_ref, lng_ref, lnb_ref, next_mod_ref, out_ref, next_h_ref = rest
    else:
        *x_refs, mod_ref, wm_ref, wup_ref, wout_ref, lng_ref, lnb_ref, out_ref = rest
    h = h_ref[0]
    acc = None
    for i, (o_ref, z_ref) in enumerate(((oa_ref, za_ref), (ob_ref, zb_ref), (oc_ref, zc_ref), (od_ref, zd_ref))):
        z = z_ref[0].astype(F32)
        br = (o_ref[0].astype(F32) * (z * _sigmoid(z))).astype(BF16)
        term = _sigmoid(_dot(h, wm_ref[i])) * _dot(br, wup_ref[i])
        acc = term if acc is None else acc + term
    y = _dot(acc.astype(BF16), wout_ref[...])
    r = alpha * _stream_tile(x_refs, off) + mod_ref[0, 0, 2:3, :] * y
    mu = jnp.mean(r, axis=-1, keepdims=True)
    rc = r - mu
    var = jnp.mean(rc * rc, axis=-1, keepdims=True)
    x_new = rc * lax.rsqrt(var + EPS) * lng_ref[...] + lnb_ref[...]
    out_ref[0] = x_new
    if emit_next:
        mu = jnp.mean(x_new, axis=-1, keepdims=True)
        xc = x_new - mu
        var = jnp.mean(xc * xc, axis=-1, keepdims=True)
        y = xc * lax.rsqrt(var + EPS)
        next_h_ref[0] = (y * (1.0 + next_mod_ref[0, 0, 1:2, :]) + next_mod_ref[0, 0, 0:1, :]).astype(BF16)


def _merge(h, outs, p_plain, xs, mod, wm, wup, wout, ln_g, ln_b, layer, skip_ctx, next_mod=None):
    b, l, d = h.shape
    off = 1 if skip_ctx else 0
    nt = l // TQ - off
    row = lambda i, j: (i, j + off, 0)
    zspec = lambda blk: pl.BlockSpec((1, TQ, BRANCH_W), lambda i, j: (i, j + off, blk))
    c2 = lambda i, j: (0, 0)
    in_specs = ([pl.BlockSpec((1, TQ, d), row)]
                + [pl.BlockSpec((1, TQ, BRANCH_W), row)] * 4
                + [zspec(PL_AZ), zspec(PL_BZ), zspec(PL_CZ), zspec(PL_DZ)]
                + _stream_specs(xs, d, off)
                + [pl.BlockSpec((1, 1, 3, d), lambda i, j: (i, jnp.minimum(j + off, 1), 0, 0)),
                   pl.BlockSpec((None, 4, d, d), lambda i, j: (layer, 0, 0, 0)),
                   pl.BlockSpec((None, 4, BRANCH_W, d), lambda i, j: (layer, 0, 0, 0)),
                   pl.BlockSpec((None, d, d), lambda i, j: (layer, 0, 0)),
                   pl.BlockSpec((1, d), c2), pl.BlockSpec((1, d), c2)])
    args = [h, *outs, p_plain, p_plain, p_plain, p_plain, *xs, mod, wm, wup, wout,
            ln_g.reshape(1, d), ln_b.reshape(1, d)]
    tile_out = pl.BlockSpec((1, TQ, d), lambda i, j: (i, j, 0))
    out_specs, out_shape = tile_out, jax.ShapeDtypeStruct((b, nt * TQ, d), F32)
    if next_mod is not None:
        in_specs.append(pl.BlockSpec((1, 1, 3, d), lambda i, j: (i, jnp.minimum(j + off, 1), 0, 0)))
        args.append(next_mod)
        out_specs, out_shape = [tile_out, tile_out], [out_shape, jax.ShapeDtypeStruct((b, nt * TQ, d), BF16)]
    return pl.pallas_call(
        functools.partial(_merge_kernel, alpha=(2 * DEPTH) ** 0.25, off=off, emit_next=next_mod is not None),
        grid=(b, nt),
        in_specs=in_specs,
        out_specs=out_specs,
        out_shape=out_shape,
        compiler_params=_cparams(("parallel", "parallel")),
        name="merge",
    )(*args)


def _perm_window(a, axis):
    a = jnp.moveaxis(a, axis, -1)
    lead = a.shape[:-1]
    a = a.reshape(lead + (WIN_KV, WIN_HEADS // WIN_KV, HEAD_DIM)).swapaxes(-3, -2).reshape(lead + (BRANCH_W,))
    return jnp.moveaxis(a, -1, axis)


def _perm_global(a, axis):
    a = jnp.moveaxis(a, axis, -1)
    lead = a.shape[:-1]
    a = a.reshape(lead + (GLB_KV // 2, 2, GLB_HEADS // GLB_KV, HEAD_DIM)).swapaxes(-3, -2).reshape(lead + (BRANCH_W,))
    return jnp.moveaxis(a, -1, axis)


def _rotary_lanes(a):
    lead = a.shape[:-1]
    a = a.reshape(lead + (a.shape[-1] // LANES, 2, 2, 2, ROPE_HALF))
    return jnp.moveaxis(a, -2, -4).reshape(lead + (-1,))


def _split_w_in(w):
    w = w.astype(BF16)
    cols = lambda off, n: w[..., off:off + n]
    pad = jnp.zeros(w.shape[:-1] + (LANES - 2 * GLA_RANK,), w.dtype)
    w_plain = jnp.concatenate([
        cols(_A_V, 512), cols(_A_Z, 512), _perm_window(cols(_B_Z, 512), -1), _perm_global(cols(_C_Z, 512), -1),
        cols(_D_Z, 512), cols(_A_Q, 256) * GLA_DK ** -0.5, cols(_A_K, 256), cols(_C_V, 256), cols(_D_V, 256),
        cols(_B_V, 128), cols(_A_GF, 2 * GLA_RANK), pad], axis=-1)
    w_rope = _rotary_lanes(jnp.concatenate([
        _perm_window(cols(_B_Q, 512), -1) * HEAD_DIM ** -0.5, cols(_D_Q, 512) * HEAD_DIM ** -0.5,
        cols(_D_K, 256), cols(_B_K, 128)], axis=-1))
    w_nr = _rotary_lanes(jnp.concatenate([_perm_global(cols(_C_Q, 512), -1), cols(_C_K, 256)], axis=-1))
    return w_plain, w_rope, w_nr


def _rope_tables(n_ctx, n_lat):
    t = np.arange(n_lat)
    freqs = ROPE_BASE ** (-np.arange(ROPE_HALF, dtype=np.float32) / ROPE_HALF)
    pos = np.stack([(t // GRID_W).astype(np.float32), (t % GRID_W).astype(np.float32)], axis=1)
    ang = jnp.asarray(pos[:, :, None] * freqs[None, None, :], F32).reshape(n_lat, 2 * ROPE_HALF)
    cos = jnp.tile(jnp.cos(ang), (1, LANES // (2 * ROPE_HALF)))
    sign = np.where(np.arange(LANES) < LANES // 2, -1.0, 1.0).astype(np.float32)
    sin = jnp.tile(jnp.sin(ang), (1, LANES // (2 * ROPE_HALF))) * sign
    cos_t = jnp.concatenate([jnp.ones((n_ctx, LANES), F32), cos], axis=0)
    sin_t = jnp.concatenate([jnp.zeros((n_ctx, LANES), F32), sin], axis=0)
    return cos_t, sin_t


def kernel(x, c, ctx, c_ctx, w_ada, b_ada, w_in, gla_w_gate, gla_b_gate, gla_norm, win_sink, glb_q_norm,
           glb_k_norm, diff_lambda, diff_norm, w_merge, w_up, w_out, ln_g, ln_b):
    b, n_lat, d = x.shape
    n_ctx = ctx.shape[1]
    assert n_ctx == TQ and n_lat % TK == 0 and d == 1024
    xs = (ctx, x)
    cs = jnp.zeros((16, d), F32).at[0:b].set(c).at[b].set(c_ctx)
    cos_t, sin_t = _rope_tables(n_ctx, n_lat)
    head_of_lane = (np.arange(LANES) // (HEAD_DIM // 2)) % 2
    seg = jnp.asarray(head_of_lane[:, None] == head_of_lane[None, :], BF16)

    w_plain, w_rope, w_nr = _split_w_in(w_in)
    gain_nr = _rotary_lanes(jnp.concatenate([jnp.tile(glb_q_norm * (HEAD_DIM ** -0.5 * LOG2E), (1, GLB_HEADS)),
                                             jnp.tile(glb_k_norm, (1, GLB_KV))], axis=1))
    wg = jnp.zeros((DEPTH, LANES, 2 * GLA_HEADS * GLA_DK), F32)
    wg = wg.at[:, 0:GLA_RANK, 0:256].set(gla_w_gate[:, 0]).at[:, GLA_RANK:2 * GLA_RANK, 256:512].set(gla_w_gate[:, 1])
    wg = wg.astype(BF16)
    wup = w_up.astype(BF16)
    wup = jnp.stack([wup[:, 0], _perm_window(wup[:, 1], -2), _perm_global(wup[:, 2], -2), wup[:, 3]], axis=1)
    wm, wout = w_merge.astype(BF16), w_out.astype(BF16)

    ada = _ada(cs, w_ada, b_ada)
    mod_x = ada[:, 0:b].reshape(DEPTH, b, 1, 3, d)
    mod_c = jnp.broadcast_to(ada[:, b].reshape(DEPTH, 1, 1, 3, d), (DEPTH, b, 1, 3, d))
    mods = jnp.concatenate([mod_c, mod_x], axis=2)

    h = _ln_mod(xs, mods[0])
    for layer in range(DEPTH):
        last = layer == DEPTH - 1
        lam_init = 0.8 - 0.6 * math.exp(-0.3 * layer)
        mod = mods[layer]
        p_plain = _inproj(h, w_plain, layer, "plain")
        p_rope = _inproj(h, w_rope, layer, "rope", cos_t, sin_t)
        p_nr = _inproj(h, w_nr, layer, "nr", cos_t, sin_t, gain_nr[layer].reshape(1, NR_W), seg)

        o_a = _gla_mixer(p_plain, wg[layer], gla_b_gate[layer].reshape(1, 512), gla_norm[layer].reshape(1, BRANCH_W),
                         n_ctx)
        o_b = _window_mixer(p_rope, p_plain, win_sink[layer], n_ctx)
        o_c, o_d = _dense_mixers(p_nr, p_rope, p_plain, diff_lambda[layer], diff_norm[layer], lam_init, n_ctx)

        merged = _merge(h, (o_a, o_b, o_c, o_d), p_plain, xs, mod, wm, wup, wout, ln_g[layer], ln_b[layer], layer,
                        skip_ctx=last, next_mod=None if last else mods[layer + 1])
        xs, h = ((merged,), None) if last else ((merged[0],), merged[1])
    return xs[0]
```

```python
import functools
import math

import numpy as np
import jax
import jax.numpy as jnp
from jax import lax
from jax.experimental import pallas as pl
from jax.experimental.pallas import tpu as pltpu

F32 = jnp.float32
BF16 = jnp.bfloat16

DEPTH = 2
GRID_W = 64
HEAD_DIM = 64
ROPE_HALF = HEAD_DIM // 4
ROPE_BASE = 10000.0
EPS = 1e-6
NEG_INF = -1e30
GLA_HEADS, GLA_DK, GLA_DV, GLA_RANK, GLA_TAU, GLA_CHUNK = 4, 64, 128, 16, 16.0, 64
WIN_HEADS, WIN_KV, WINDOW = 8, 2, 128
GLB_HEADS, GLB_KV = 8, 4
DIF_HEADS, DIF_KV, DIF_DV = 4, 2, 128
BRANCH_W = 512

LANES = 128
TQ = 256
TK = 512
VMEM_LIMIT = 56 << 20

_A_Q, _A_K, _A_V, _A_GF, _A_Z = 0, 256, 512, 1024, 1056
_B_Q, _B_K, _B_V, _B_Z = 1568, 2080, 2208, 2336
_C_Q, _C_K, _C_V, _C_Z = 2848, 3360, 3616, 3872
_D_Q, _D_K, _D_V, _D_Z = 4384, 4896, 5152, 5408

PL_AV, PL_AZ, PL_BZ, PL_CZ, PL_DZ = 0, 1, 2, 3, 4
PL_AQ, PL_AK, PL_CV, PL_DV = 10, 11, 12, 13
PL_BV, PL_AG = 28, 29
PLAIN_W, ROPE_W, NR_W = 3840, 1408, 768
ROPE_QW = 1024
LOG2E = math.log2(math.e)


def _cparams(sem):
    return pltpu.CompilerParams(dimension_semantics=sem, vmem_limit_bytes=VMEM_LIMIT)


def _sigmoid(x):
    return 0.5 * jnp.tanh(0.5 * x) + 0.5


def _dot(a, b):
    return jnp.dot(a, b, preferred_element_type=F32)


def _dot_nt(a, b):
    return lax.dot_general(a, b, (((1,), (1,)), ((), ())), preferred_element_type=F32)


def _dot_tn(a, b):
    return lax.dot_general(a, b, (((0,), (0,)), ((), ())), preferred_element_type=F32)


def _ada_kernel(c_ref, w_ref, b_ref, o_ref):
    c = c_ref[...]
    s = c * _sigmoid(c)
    o_ref[...] = jnp.dot(s, w_ref[...], preferred_element_type=F32,
                         precision=lax.Precision.HIGHEST) + b_ref[...]


def _ada(cs, w_ada, b_ada):
    r, d = cs.shape
    depth, _, n = w_ada.shape
    bn = 1024
    return pl.pallas_call(
        _ada_kernel,
        grid=(depth, n // bn),
        in_specs=[pl.BlockSpec((r, d), lambda l, j: (0, 0)),
                  pl.BlockSpec((None, d, bn), lambda l, j: (l, 0, j)),
                  pl.BlockSpec((None, 1, bn), lambda l, j: (l, 0, j))],
        out_specs=pl.BlockSpec((None, r, bn), lambda l, j: (l, 0, j)),
        out_shape=jax.ShapeDtypeStruct((depth, r, n), F32),
        compiler_params=_cparams(("arbitrary", "arbitrary")),
        name="ada",
    )(cs, w_ada, b_ada.reshape(depth, 1, n))


def _stream_specs(xs, d, off=0):
    if len(xs) == 1:
        return [pl.BlockSpec((1, TQ, d), lambda i, j: (i, j + off, 0))]
    return [pl.BlockSpec((1, TQ, d), lambda i, j: (i, 0, 0)),
            pl.BlockSpec((1, TQ, d), lambda i, j: (i, jnp.maximum(j + off - 1, 0), 0))]


def _stream_tile(refs, off=0):
    if len(refs) == 1:
        return refs[0][0]
    return jnp.where(pl.program_id(1) + off == 0, refs[0][0], refs[1][0])


def _ln_kernel(*refs):
    *x_refs, mod_ref, h_ref = refs
    x = _stream_tile(x_refs)
    mu = jnp.mean(x, axis=-1, keepdims=True)
    xc = x - mu
    var = jnp.mean(xc * xc, axis=-1, keepdims=True)
    y = xc * lax.rsqrt(var + EPS)
    shift = mod_ref[0, 0, 0:1, :]
    scale = mod_ref[0, 0, 1:2, :]
    h_ref[0] = (y * (1.0 + scale) + shift).astype(BF16)


def _ln_mod(xs, mod):
    b, _, d = xs[0].shape
    l = sum(a.shape[1] for a in xs)
    return pl.pallas_call(
        _ln_kernel,
        grid=(b, l // TQ),
        in_specs=_stream_specs(xs, d) + [pl.BlockSpec((1, 1, 3, d), lambda i, j: (i, jnp.minimum(j, 1), 0, 0))],
        out_specs=pl.BlockSpec((1, TQ, d), lambda i, j: (i, j, 0)),
        out_shape=jax.ShapeDtypeStruct((b, l, d), BF16),
        compiler_params=_cparams(("parallel", "parallel")),
        name="ln_mod",
    )(*xs, mod)


def _rope(y, cos, sin):
    return y * cos + pltpu.roll(y, LANES // 2, 1) * sin


def _inproj_plain_kernel(h_ref, w_ref, o_ref):
    o_ref[0] = _dot(h_ref[0], w_ref[...]).astype(BF16)


def _inproj_rope_kernel(h_ref, w_ref, cos_ref, sin_ref, o_ref, *, width, q_width):
    acc = _dot(h_ref[0], w_ref[...])
    cos, sin = cos_ref[...], sin_ref[...]
    for c in range(width // LANES):
        y = _rope(acc[:, c * LANES:(c + 1) * LANES], cos, sin)
        if c * LANES < q_width:
            y = y * LOG2E
        o_ref[0, :, c * LANES:(c + 1) * LANES] = y.astype(BF16)


def _inproj_nr_kernel(h_ref, w_ref, cos_ref, sin_ref, gain_ref, seg_ref, o_ref, *, width):
    acc = _dot(h_ref[0], w_ref[...])
    cos, sin = cos_ref[...], sin_ref[...]
    for c in range(width // LANES):
        y = acc[:, c * LANES:(c + 1) * LANES]
        ss = _dot((y * y).astype(BF16), seg_ref[...])
        y = y * lax.rsqrt(ss * (1.0 / HEAD_DIM) + EPS) * gain_ref[:, c * LANES:(c + 1) * LANES]
        o_ref[0, :, c * LANES:(c + 1) * LANES] = _rope(y, cos, sin).astype(BF16)


def _inproj(h, w, layer, mode, cos=None, sin=None, gain=None, seg=None):
    b, l, d = h.shape
    n = w.shape[2]
    bm = l // 2
    if mode == "plain":
        bn = 768
        return pl.pallas_call(
            _inproj_plain_kernel,
            grid=(b, l // bm, n // bn),
            in_specs=[pl.BlockSpec((1, bm, d), lambda i, r, j: (i, r, 0)),
                      pl.BlockSpec((None, d, bn), lambda i, r, j: (layer, 0, j))],
            out_specs=pl.BlockSpec((1, bm, bn), lambda i, r, j: (i, r, j)),
            out_shape=jax.ShapeDtypeStruct((b, l, n), BF16),
            compiler_params=_cparams(("parallel", "parallel", "arbitrary")),
            name="inproj_plain",
        )(h, w)
    tab = pl.BlockSpec((bm, LANES), lambda i, r: (r, 0))
    in_specs = [pl.BlockSpec((1, bm, d), lambda i, r: (i, r, 0)),
                pl.BlockSpec((None, d, n), lambda i, r: (layer, 0, 0)), tab, tab]
    args = [h, w, cos, sin]
    if mode == "rope":
        body = functools.partial(_inproj_rope_kernel, width=n, q_width=ROPE_QW)
    else:
        body = functools.partial(_inproj_nr_kernel, width=n)
        in_specs += [pl.BlockSpec((1, n), lambda i, r: (0, 0)),
                     pl.BlockSpec((LANES, LANES), lambda i, r: (0, 0))]
        args += [gain, seg]
    return pl.pallas_call(
        body,
        grid=(b, l // bm),
        in_specs=in_specs,
        out_specs=pl.BlockSpec((1, bm, n), lambda i, r: (i, r, 0)),
        out_shape=jax.ShapeDtypeStruct((b, l, n), BF16),
        compiler_params=_cparams(("parallel", "parallel")),
        name="inproj_" + mode,
    )(*args)


def _stack_halves(q, rotary=False):
    lane = lax.broadcasted_iota(jnp.int32, q.shape, 1)
    first = ((lane // (HEAD_DIM // 2)) % 2 == 0) if rotary else (lane < HEAD_DIM)
    zero = jnp.zeros_like(q)
    return jnp.concatenate([jnp.where(first, q, zero), jnp.where(first, zero, q)], axis=0)


FLASH_ROWS = 2 * TQ


FLASH_REFS = 6


def _flash_scratch(n_streams):
    stat = pltpu.VMEM((FLASH_ROWS, LANES), F32)
    one = [pltpu.VMEM((FLASH_ROWS, LANES), BF16), stat, stat, stat,
           pltpu.VMEM((FLASH_ROWS, TK), F32), pltpu.VMEM((FLASH_ROWS, TK), BF16)]
    return one * n_streams


def _fold_chunk(stream, s, v, first):
    _, m_s, l_s, acc_s, _, p_s = stream
    tk = s.shape[1]
    m_new = jnp.broadcast_to(jnp.max(s, axis=1, keepdims=True), (FLASH_ROWS, LANES))
    if not first:
        m_old = m_s[...]
        m_new = jnp.maximum(m_old, m_new)
        alpha = jnp.exp2(m_old - m_new)
    p = jnp.exp2(s - jnp.concatenate([m_new] * (tk // LANES), axis=1))
    p_s[:, 0:tk] = p.astype(BF16)
    psum = p[:, 0:LANES]
    for t in range(1, tk // LANES):
        psum = psum + p[:, t * LANES:(t + 1) * LANES]
    pv = _dot(p_s[:, 0:tk], v)
    m_s[...] = m_new
    l_s[...] = psum if first else alpha * l_s[...] + psum
    acc_s[...] = pv if first else alpha * acc_s[...] + pv


def _flash(q_blocks, kv_srcs, n_ctx, n_lat_chunks, scratch):
    ns = len(q_blocks)
    streams = [scratch[FLASH_REFS * i:FLASH_REFS * (i + 1)] for i in range(ns)]
    kv_lanes = [slice(kb * LANES, (kb + 1) * LANES) for _, _, kb in kv_srcs]

    def scores(i, r0, tk):
        qs_s, _, _, _, s_s, _ = streams[i]
        s_s[:, 0:tk] = _dot_nt(qs_s[...], kv_srcs[i][0][0, pl.ds(r0, tk), kv_lanes[i]])

    def accumulate(i, r0, tk, first=False):
        s_s = streams[i][4]
        _fold_chunk(streams[i], s_s[:, 0:tk], kv_srcs[i][1][0, pl.ds(r0, tk), kv_lanes[i]], first)

    def stage(r0, tk, nxt, first=False):
        for i in range(ns):
            accumulate(i, r0, tk, first)
            if nxt is not None:
                scores(i, *nxt)

    def head():
        for i in range(ns):
            streams[i][0][...] = _stack_halves(q_blocks[i](), rotary=True)
            scores(i, 0, n_ctx)
        stage(0, n_ctx, (n_ctx, TK), first=True)

    def body(c, _):
        r0 = pl.multiple_of(n_ctx + c * TK, LANES)
        stage(r0, TK, (r0 + TK, TK))
        return 0

    def latent():
        lax.fori_loop(0, n_lat_chunks - 1, body, 0)
        stage(n_ctx + (n_lat_chunks - 1) * TK, TK, None)

    def results():
        return [acc_s[...] / jnp.sum(l_s[...], axis=1, keepdims=True) for _, _, l_s, acc_s, _, _ in streams]

    return head, latent, results


N_GLB_BLOCKS = GLB_HEADS // 2
N_DIF_BLOCKS = DIF_HEADS


def _dense_kernel(qc_ref, kc_ref, vc_ref, qd_ref, kd_ref, vd_ref, lam_ref, g_ref, oc_ref, od_ref, *scratch,
                  n_ctx, n_lat, n_tiles, lam_init):
    load = lambda ref, c: (lambda: ref[0, :, c * LANES:(c + 1) * LANES])
    q_blocks = [load(qc_ref, c) for c in range(N_GLB_BLOCKS)] + [load(qd_ref, c) for c in range(N_DIF_BLOCKS)]
    kv_srcs = ([(kc_ref, vc_ref, c // 2) for c in range(N_GLB_BLOCKS)]
               + [(kd_ref, vd_ref, c // 2) for c in range(N_DIF_BLOCKS)])
    head, latent, results = _flash(q_blocks, kv_srcs, n_ctx, n_lat // TK, scratch)

    def write_out():
        outs = results()
        lane = lax.broadcasted_iota(jnp.int32, (TQ, LANES), 1)
        for c, o in enumerate(outs[:N_GLB_BLOCKS]):
            oc_ref[0, :, c * LANES:(c + 1) * LANES] = jnp.where(lane < HEAD_DIM, o[0:TQ], o[TQ:2 * TQ]).astype(BF16)
        lp = lam_ref[...]
        lam = (jnp.exp(jnp.sum(lp[0:1] * lp[1:2], axis=1, keepdims=True))
               - jnp.exp(jnp.sum(lp[2:3] * lp[3:4], axis=1, keepdims=True)) + lam_init)
        for c, o in enumerate(outs[N_GLB_BLOCKS:]):
            o = o[0:TQ] - lam * o[TQ:2 * TQ]
            y = o * lax.rsqrt(jnp.mean(o * o, axis=-1, keepdims=True) + EPS) * g_ref[...]
            od_ref[0, :, c * LANES:(c + 1) * LANES] = (y * (1.0 - lam_init)).astype(BF16)

    j = pl.program_id(1)

    @pl.when(j == 0)
    def _():
        head()

    @pl.when((j > 0) & (j < n_tiles))
    def _():
        write_out()
        head()
        latent()

    @pl.when(j == n_tiles)
    def _():
        write_out()


def _dense_mixers(p_nr, p_rope, p_plain, lam_p, sub_g, lam_init, n_ctx):
    b, l, _ = p_nr.shape
    nt = l // TQ
    kern = functools.partial(_dense_kernel, n_ctx=n_ctx, n_lat=l - n_ctx, n_tiles=nt, lam_init=lam_init)
    tile = lambda blk: pl.BlockSpec((1, TQ, 512), lambda i, j: (i, jnp.minimum(j, nt - 1), blk))
    lagged = pl.BlockSpec((1, TQ, 512), lambda i, j: (i, jnp.maximum(j - 1, 0), 0))
    full = lambda blk: pl.BlockSpec((1, l, 256), lambda i, j: (i, 0, blk))
    out = jax.ShapeDtypeStruct((b, l, BRANCH_W), BF16)
    return pl.pallas_call(
        kern,
        grid=(b, nt + 1),
        in_specs=[tile(0), full(2), full(PL_CV),
                  tile(1), full(4), full(PL_DV),
                  pl.BlockSpec((4, HEAD_DIM), lambda i, j: (0, 0)),
                  pl.BlockSpec((1, DIF_DV), lambda i, j: (0, 0))],
        out_specs=[lagged, lagged],
        out_shape=[out, out],
        scratch_shapes=_flash_scratch(N_GLB_BLOCKS + N_DIF_BLOCKS),
        compiler_params=_cparams(("parallel", "arbitrary")),
        name="dense_mixers",
    )(p_nr, p_nr, p_plain, p_rope, p_rope, p_plain, lam_p, sub_g.reshape(1, DIF_DV))


WIN_SPAN = TQ + 2 * WINDOW
WIN_BLOCKS = WIN_HEADS // WIN_KV


def _window_start(j, total):
    return jnp.clip(j * TQ - WINDOW, 0, total - WIN_SPAN)


def _window_bias(n_ctx, total):
    row = np.arange(2 * TQ)[:, None] % TQ
    col = np.arange(WIN_SPAN)[None, :]
    patterns, index = [], []
    for j in range(total // TQ):
        start = int(np.clip(j * TQ - WINDOW, 0, total - WIN_SPAN))
        qpos, kpos = j * TQ + row, start + col
        valid = (kpos >= n_ctx) & (qpos >= n_ctx) & (np.abs(kpos - qpos) <= WINDOW)
        bias = np.where(valid, 0.0, NEG_INF).astype(np.float32)
        for k, seen in enumerate(patterns):
            if np.array_equal(seen, bias):
                index.append(k)
                break
        else:
            index.append(len(patterns))
            patterns.append(bias)
    return np.stack(patterns), np.asarray(index, np.int32)


def _window_kernel(pattern_ref, sink_ref, q_ref, k_ref, v_ref, bias_ref, o_ref, *scratch, n_ctx, total):
    j = pl.program_id(1)
    start = pl.multiple_of(_window_start(j, total), LANES)
    streams = [scratch[3 * g:3 * (g + 1)] for g in range(WIN_BLOCKS)]
    for g, (qs_s, s_s, _) in enumerate(streams):
        qs_s[...] = _stack_halves(q_ref[0, :, g * LANES:(g + 1) * LANES], rotary=True)
        s_s[:, 0:n_ctx] = _dot_nt(qs_s[...], k_ref[0, 0:n_ctx, :])
        s_s[:, n_ctx:] = _dot_nt(qs_s[...], k_ref[0, pl.ds(start, WIN_SPAN), :])

    srow = lax.broadcasted_iota(jnp.int32, (2 * TQ, LANES), 0)
    lane = lax.broadcasted_iota(jnp.int32, (TQ, LANES), 1)
    width = n_ctx + WIN_SPAN
    for g, (_, s_s, p_s) in enumerate(streams):
        sink = jnp.where(srow < TQ, sink_ref[g], sink_ref[WIN_BLOCKS + g]) * LOG2E
        s_c = s_s[:, 0:n_ctx]
        s_w = s_s[:, n_ctx:] + bias_ref[0]
        m = jnp.maximum(jnp.maximum(jnp.max(s_c, axis=1, keepdims=True), jnp.max(s_w, axis=1, keepdims=True)), sink)
        p_c = jnp.exp2(s_c - jnp.concatenate([m] * (n_ctx // LANES), axis=1))
        p_w = jnp.exp2(s_w - jnp.concatenate([m] * (WIN_SPAN // LANES), axis=1))
        p_s[:, 0:n_ctx] = p_c.astype(BF16)
        p_s[:, n_ctx:] = p_w.astype(BF16)
        psum = p_c[:, 0:LANES]
        for t in range(1, n_ctx // LANES):
            psum = psum + p_c[:, t * LANES:(t + 1) * LANES]
        for t in range(WIN_SPAN // LANES):
            psum = psum + p_w[:, t * LANES:(t + 1) * LANES]
        den = jnp.sum(psum, axis=1, keepdims=True) + jnp.exp2(sink - m)
        o = (_dot(p_s[:, 0:n_ctx], v_ref[0, 0:n_ctx, :])
             + _dot(p_s[:, n_ctx:width], v_ref[0, pl.ds(start, WIN_SPAN), :])) / den
        o_ref[0, :, g * LANES:(g + 1) * LANES] = jnp.where(lane < HEAD_DIM, o[0:TQ], o[TQ:2 * TQ]).astype(BF16)


def _window_mixer(p_rope, p_plain, sink, n_ctx):
    b, l, _ = p_rope.shape
    kern = functools.partial(_window_kernel, n_ctx=n_ctx, total=l)
    bias, pattern = _window_bias(n_ctx, l)
    grid_spec = pltpu.PrefetchScalarGridSpec(
        num_scalar_prefetch=2,
        grid=(b, l // TQ),
        in_specs=[pl.BlockSpec((1, TQ, 512), lambda i, j, pat, s: (i, j, 0)),
                  pl.BlockSpec((1, l, LANES), lambda i, j, pat, s: (i, 0, 10)),
                  pl.BlockSpec((1, l, LANES), lambda i, j, pat, s: (i, 0, PL_BV)),
                  pl.BlockSpec((1, 2 * TQ, WIN_SPAN), lambda i, j, pat, s: (pat[j], 0, 0))],
        out_specs=pl.BlockSpec((1, TQ, 512), lambda i, j, pat, s: (i, j, 0)),
        scratch_shapes=[pltpu.VMEM((2 * TQ, LANES), BF16), pltpu.VMEM((2 * TQ, n_ctx + WIN_SPAN), F32),
                        pltpu.VMEM((2 * TQ, n_ctx + WIN_SPAN), BF16)] * WIN_BLOCKS,
    )
    return pl.pallas_call(
        kern,
        grid_spec=grid_spec,
        out_shape=jax.ShapeDtypeStruct((b, l, BRANCH_W), BF16),
        compiler_params=_cparams(("parallel", "arbitrary")),
        name="window_mixer",
    )(jnp.asarray(pattern), sink, p_rope, p_rope, p_plain, jnp.asarray(bias))


GLA_GROUP = 4


def _gla_kernel(q_ref, k_ref, v_ref, g_ref, wg_ref, bg_ref, gn_ref, o_ref, accf_s, accb_s, st_s, *, total, n_ctx):
    ck = GLA_CHUNK
    n_groups = total // (ck * GLA_GROUP)
    ctx_groups = n_ctx // (ck * GLA_GROUP)
    hw = GLA_HEADS * GLA_DK
    acc_refs = (accf_s, accb_s)

    st_s[...] = jnp.zeros_like(st_s)

    rr = lax.broadcasted_iota(jnp.int32, (ck, ck), 0)
    cc = lax.broadcasted_iota(jnp.int32, (ck, ck), 1)
    tri = (rr >= cc, rr <= cc)
    tri_b = tuple(jnp.where(t, 1.0, 0.0).astype(BF16) for t in tri)
    tri2 = tuple(jnp.concatenate([t, t], axis=0) for t in tri)
    lane = lax.broadcasted_iota(jnp.int32, (LANES, LANES), 1)

    def body(i, _):
        gf = i
        gb = jnp.where(i < ctx_groups, ctx_groups - 1 - i, n_groups - 1 - (i - ctx_groups))
        chains = []
        for u in range(GLA_GROUP):
            for d, grp in ((0, gf), (1, gb)):
                c = grp * GLA_GROUP + (u if d == 0 else GLA_GROUP - 1 - u)
                chains.append((d, pl.multiple_of(c * ck, ck)))

        gate = [_dot(g_ref[0, pl.ds(r0, ck), :], wg_ref[:, d * hw:(d + 1) * hw]) + bg_ref[:, d * hw:(d + 1) * hw]
                for d, r0 in chains]
        cums = []
        for (d, r0), y in zip(chains, gate):
            la = (jnp.minimum(y, 0.0) - jnp.log(1.0 + jnp.exp(-jnp.abs(y)))) * (1.0 / GLA_TAU)
            hi = la.astype(BF16)
            lo = (la - hi.astype(F32)).astype(BF16)
            cums.append(_dot(tri_b[d], hi) + _dot(tri_b[d], lo))

        prepped = []
        for (d, r0), cum in zip(chains, cums):
            tot = cum[ck - 1:ck, :] if d == 0 else cum[0:1, :]
            qf = q_ref[0, pl.ds(r0, ck), :].astype(F32)
            kf = k_ref[0, pl.ds(r0, ck), :].astype(F32)
            qt = (qf * jnp.exp(cum)).astype(BF16)
            kt = (kf * jnp.exp(-cum)).astype(BF16)
            kw = (kf * jnp.exp(tot - cum)).astype(BF16)
            qs, a2 = [], []
            for p in range(2):
                sl = slice(p * LANES, (p + 1) * LANES)
                qs.append(_stack_halves(qt[:, sl]))
                a2.append(jnp.where(tri2[d], _dot_nt(qs[p], kt[:, sl]), 0.0).astype(BF16))
            prepped.append((qs, a2, kw, jnp.exp(tot)))

        intra, update = [], []
        for (d, r0), (qs, a2, kw, dec) in zip(chains, prepped):
            av, upd = [], []
            for p in range(2):
                sl = slice(p * LANES, (p + 1) * LANES)
                halves = []
                for hh in range(2):
                    vh = v_ref[0, pl.ds(r0, ck), (2 * p + hh) * GLA_DV:(2 * p + hh + 1) * GLA_DV]
                    av.append(_dot(a2[p][hh * ck:(hh + 1) * ck], vh))
                    halves.append(_dot_tn(vh, kw[:, sl]))
                upd.append(jnp.where(lane < GLA_DK, halves[0], halves[1]))
            intra.append(av)
            update.append(upd)

        states = [[st_s[2 * d + p] for p in range(2)] for d in range(2)]
        for (d, r0), (qs, a2, kw, dec), av, upd in zip(chains, prepped, intra, update):
            for p in range(2):
                inter = _dot_nt(qs[p], states[d][p].astype(BF16))
                for hh in range(2):
                    vs = slice((2 * p + hh) * GLA_DV, (2 * p + hh + 1) * GLA_DV)
                    acc_refs[d][pl.ds(r0, ck), vs] = av[2 * p + hh] + inter[hh * ck:(hh + 1) * ck]
                states[d][p] = states[d][p] * dec[:, p * LANES:(p + 1) * LANES] + upd[p]
        for d in range(2):
            for p in range(2):
                st_s[2 * d + p] = states[d][p]
        return 0

    lax.fori_loop(0, n_groups, body, 0)

    def finish(i, _):
        r0 = pl.multiple_of(i * TQ, TQ)
        for h in range(GLA_HEADS):
            vs = slice(h * GLA_DV, (h + 1) * GLA_DV)
            o = accf_s[pl.ds(r0, TQ), vs] + accb_s[pl.ds(r0, TQ), vs]
            y = o * lax.rsqrt(jnp.mean(o * o, axis=-1, keepdims=True) + EPS) * gn_ref[:, vs]
            o_ref[0, pl.ds(r0, TQ), vs] = y.astype(BF16)
        return 0

    lax.fori_loop(0, total // TQ, finish, 0)


def _gla_mixer(p_plain, wg, bg, gn, n_ctx):
    b, l, _ = p_plain.shape
    kern = functools.partial(_gla_kernel, total=l, n_ctx=n_ctx)
    const = lambda i: (0, 0)
    return pl.pallas_call(
        kern,
        grid=(b,),
        in_specs=[pl.BlockSpec((1, l, 256), lambda i: (i, 0, PL_AQ)),
                  pl.BlockSpec((1, l, 256), lambda i: (i, 0, PL_AK)),
                  pl.BlockSpec((1, l, 512), lambda i: (i, 0, PL_AV)),
                  pl.BlockSpec((1, l, LANES), lambda i: (i, 0, PL_AG)),
                  pl.BlockSpec((LANES, 512), const),
                  pl.BlockSpec((1, 512), const),
                  pl.BlockSpec((1, 512), const)],
        out_specs=pl.BlockSpec((1, l, BRANCH_W), lambda i: (i, 0, 0)),
        out_shape=jax.ShapeDtypeStruct((b, l, BRANCH_W), BF16),
        scratch_shapes=[pltpu.VMEM((l, BRANCH_W), F32), pltpu.VMEM((l, BRANCH_W), F32),
                        pltpu.VMEM((4, GLA_DV, LANES), F32)],
        compiler_params=_cparams(("parallel",)),
        name="gla_mixer",
    )(p_plain, p_plain, p_plain, p_plain, wg, bg, gn)


def _merge_kernel(h_ref, oa_ref, ob_ref, oc_ref, od_ref, za_ref, zb_ref, zc_ref, zd_ref, *rest, alpha, off,
                  emit_next):
    if emit_next:
        *x_refs, mod_ref, wm_ref, wup_ref, wout_ref, lng_ref, lnb_ref, next_mod_ref, out_ref, next_h_ref = rest
    else:
        *x_refs, mod_ref, wm_ref, wup_ref, wout_ref, lng_ref, lnb_ref, out_ref = rest
    h = h_ref[0]
    acc = None
    for i, (o_ref, z_ref) in enumerate(((oa_ref, za_ref), (ob_ref, zb_ref), (oc_ref, zc_ref), (od_ref, zd_ref))):
        z = z_ref[0].astype(F32)
        br = (o_ref[0].astype(F32) * (z * _sigmoid(z))).astype(BF16)
        term = _sigmoid(_dot(h, wm_ref[i])) * _dot(br, wup_ref[i])
        acc = term if acc is None else acc + term
    y = _dot(acc.astype(BF16), wout_ref[...])
    r = alpha * _stream_tile(x_refs, off) + mod_ref[0, 0, 2:3, :] * y
    mu = jnp.mean(r, axis=-1, keepdims=True)
    rc = r - mu
    var = jnp.mean(rc * rc, axis=-1, keepdims=True)
    x_new = rc * lax.rsqrt(var + EPS) * lng_ref[...] + lnb_ref[...]
    out_ref[0] = x_new
    if emit_next:
        mu = jnp.mean(x_new, axis=-1, keepdims=True)
        xc = x_new - mu
        var = jnp.mean(xc * xc, axis=-1, keepdims=True)
        y = xc * lax.rsqrt(var + EPS)
        next_h_ref[0] = (y * (1.0 + next_mod_ref[0, 0, 1:2, :]) + next_mod_ref[0, 0, 0:1, :]).astype(BF16)


def _merge(h, outs, p_plain, xs, mod, wm, wup, wout, ln_g, ln_b, layer, skip_ctx, next_mod=None):
    b, l, d = h.shape
    off = 1 if skip_ctx else 0
    nt = l // TQ - off
    row = lambda i, j: (i, j + off, 0)
    zspec = lambda blk: pl.BlockSpec((1, TQ, BRANCH_W), lambda i, j: (i, j + off, blk))
    c2 = lambda i, j: (0, 0)
    in_specs = ([pl.BlockSpec((1, TQ, d), row)]
                + [pl.BlockSpec((1, TQ, BRANCH_W), row)] * 4
                + [zspec(PL_AZ), zspec(PL_BZ), zspec(PL_CZ), zspec(PL_DZ)]
                + _stream_specs(xs, d, off)
                + [pl.BlockSpec((1, 1, 3, d), lambda i, j: (i, jnp.minimum(j + off, 1), 0, 0)),
                   pl.BlockSpec((None, 4, d, d), lambda i, j: (layer, 0, 0, 0)),
                   pl.BlockSpec((None, 4, BRANCH_W, d), lambda i, j: (layer, 0, 0, 0)),
                   pl.BlockSpec((None, d, d), lambda i, j: (layer, 0, 0)),
                   pl.BlockSpec((1, d), c2), pl.BlockSpec((1, d), c2)])
    args = [h, *outs, p_plain, p_plain, p_plain, p_plain, *xs, mod, wm, wup, wout,
            ln_g.reshape(1, d), ln_b.reshape(1, d)]
    tile_out = pl.BlockSpec((1, TQ, d), lambda i, j: (i, j, 0))
    out_specs, out_shape = tile_out, jax.ShapeDtypeStruct((b, nt * TQ, d), F32)
    if next_mod is not None:
        in_specs.append(pl.BlockSpec((1, 1, 3, d), lambda i, j: (i, jnp.minimum(j + off, 1), 0, 0)))
        args.append(next_mod)
        out_specs, out_shape = [tile_out, tile_out], [out_shape, jax.ShapeDtypeStruct((b, nt * TQ, d), BF16)]
    return pl.pallas_call(
        functools.partial(_merge_kernel, alpha=(2 * DEPTH) ** 0.25, off=off, emit_next=next_mod is not None),
        grid=(b, nt),
        in_specs=in_specs,
        out_specs=out_specs,
        out_shape=out_shape,
        compiler_params=_cparams(("parallel", "parallel")),
        name="merge",
    )(*args)


def _perm_window(a, axis):
    a = jnp.moveaxis(a, axis, -1)
    lead = a.shape[:-1]
    a = a.reshape(lead + (WIN_KV, WIN_HEADS // WIN_KV, HEAD_DIM)).swapaxes(-3, -2).reshape(lead + (BRANCH_W,))
    return jnp.moveaxis(a, -1, axis)


def _perm_global(a, axis):
    a = jnp.moveaxis(a, axis, -1)
    lead = a.shape[:-1]
    a = a.reshape(lead + (GLB_KV // 2, 2, GLB_HEADS // GLB_KV, HEAD_DIM)).swapaxes(-3, -2).reshape(lead + (BRANCH_W,))
    return jnp.moveaxis(a, -1, axis)


def _rotary_lanes(a):
    lead = a.shape[:-1]
    a = a.reshape(lead + (a.shape[-1] // LANES, 2, 2, 2, ROPE_HALF))
    return jnp.moveaxis(a, -2, -4).reshape(lead + (-1,))


def _split_w_in(w):
    w = w.astype(BF16)
    cols = lambda off, n: w[..., off:off + n]
    pad = jnp.zeros(w.shape[:-1] + (LANES - 2 * GLA_RANK,), w.dtype)
    w_plain = jnp.concatenate([
        cols(_A_V, 512), cols(_A_Z, 512), _perm_window(cols(_B_Z, 512), -1), _perm_global(cols(_C_Z, 512), -1),
        cols(_D_Z, 512), cols(_A_Q, 256) * GLA_DK ** -0.5, cols(_A_K, 256), cols(_C_V, 256), cols(_D_V, 256),
        cols(_B_V, 128), cols(_A_GF, 2 * GLA_RANK), pad], axis=-1)
    w_rope = _rotary_lanes(jnp.concatenate([
        _perm_window(cols(_B_Q, 512), -1) * HEAD_DIM ** -0.5, cols(_D_Q, 512) * HEAD_DIM ** -0.5,
        cols(_D_K, 256), cols(_B_K, 128)], axis=-1))
    w_nr = _rotary_lanes(jnp.concatenate([_perm_global(cols(_C_Q, 512), -1), cols(_C_K, 256)], axis=-1))
    return w_plain, w_rope, w_nr


def _rope_tables(n_ctx, n_lat):
    t = np.arange(n_lat)
    freqs = ROPE_BASE ** (-np.arange(ROPE_HALF, dtype=np.float32) / ROPE_HALF)
    pos = np.stack([(t // GRID_W).astype(np.float32), (t % GRID_W).astype(np.float32)], axis=1)
    ang = jnp.asarray(pos[:, :, None] * freqs[None, None, :], F32).reshape(n_lat, 2 * ROPE_HALF)
    cos = jnp.tile(jnp.cos(ang), (1, LANES // (2 * ROPE_HALF)))
    sign = np.where(np.arange(LANES) < LANES // 2, -1.0, 1.0).astype(np.float32)
    sin = jnp.tile(jnp.sin(ang), (1, LANES // (2 * ROPE_HALF))) * sign
    cos_t = jnp.concatenate([jnp.ones((n_ctx, LANES), F32), cos], axis=0)
    sin_t = jnp.concatenate([jnp.zeros((n_ctx, LANES), F32), sin], axis=0)
    return cos_t, sin_t


def kernel(x, c, ctx, c_ctx, w_ada, b_ada, w_in, gla_w_gate, gla_b_gate, gla_norm, win_sink, glb_q_norm,
           glb_k_norm, diff_lambda, diff_norm, w_merge, w_up, w_out, ln_g, ln_b):
    b, n_lat, d = x.shape
    n_ctx = ctx.shape[1]
    assert n_ctx == TQ and n_lat % TK == 0 and d == 1024
    xs = (ctx, x)
    cs = jnp.zeros((16, d), F32).at[0:b].set(c).at[b].set(c_ctx)
    cos_t, sin_t = _rope_tables(n_ctx, n_lat)
    head_of_lane = (np.arange(LANES) // (HEAD_DIM // 2)) % 2
    seg = jnp.asarray(head_of_lane[:, None] == head_of_lane[None, :], BF16)

    w_plain, w_rope, w_nr = _split_w_in(w_in)
    gain_nr = _rotary_lanes(jnp.concatenate([jnp.tile(glb_q_norm * (HEAD_DIM ** -0.5 * LOG2E), (1, GLB_HEADS)),
                                             jnp.tile(glb_k_norm, (1, GLB_KV))], axis=1))
    wg = jnp.zeros((DEPTH, LANES, 2 * GLA_HEADS * GLA_DK), F32)
    wg = wg.at[:, 0:GLA_RANK, 0:256].set(gla_w_gate[:, 0]).at[:, GLA_RANK:2 * GLA_RANK, 256:512].set(gla_w_gate[:, 1])
    wg = wg.astype(BF16)
    wup = w_up.astype(BF16)
    wup = jnp.stack([wup[:, 0], _perm_window(wup[:, 1], -2), _perm_global(wup[:, 2], -2), wup[:, 3]], axis=1)
    wm, wout = w_merge.astype(BF16), w_out.astype(BF16)

    ada = _ada(cs, w_ada, b_ada)
    mod_x = ada[:, 0:b].reshape(DEPTH, b, 1, 3, d)
    mod_c = jnp.broadcast_to(ada[:, b].reshape(DEPTH, 1, 1, 3, d), (DEPTH, b, 1, 3, d))
    mods = jnp.concatenate([mod_c, mod_x], axis=2)

    h = _ln_mod(xs, mods[0])
    for layer in range(DEPTH):
        last = layer == DEPTH - 1
        lam_init = 0.8 - 0.6 * math.exp(-0.3 * layer)
        mod = mods[layer]
        p_plain = _inproj(h, w_plain, layer, "plain")
        p_rope = _inproj(h, w_rope, layer, "rope", cos_t, sin_t)
        p_nr = _inproj(h, w_nr, layer, "nr", cos_t, sin_t, gain_nr[layer].reshape(1, NR_W), seg)

        o_a = _gla_mixer(p_plain, wg[layer], gla_b_gate[layer].reshape(1, 512), gla_norm[layer].reshape(1, BRANCH_W),
                         n_ctx)
        o_b = _window_mixer(p_rope, p_plain, win_sink[layer], n_ctx)
        o_c, o_d = _dense_mixers(p_nr, p_rope, p_plain, diff_lambda[layer], diff_norm[layer], lam_init, n_ctx)

        merged = _merge(h, (o_a, o_b, o_c, o_d), p_plain, xs, mod, wm, wup, wout, ln_g[layer], ln_b[layer], layer,
                        skip_ctx=last, next_mod=None if last else mods[layer + 1])
        xs, h = ((merged,), None) if last else ((merged[0],), merged[1])
    return xs[0]
```
